```python
import jax
import jax.numpy as jnp
from jax import lax
import numpy as np

D_MODEL = 2048
BATCH = 1
SEQ = 16384
DEPTH = 2

D_MIX = D_MODEL
D_FF = 5632
MACARON_WEIGHT = 0.5
NORM_EPS = 1e-6
ROPE_THETA = 10000.0
Q_BLOCK = 128

MLA_HEADS = 8
MLA_NOPE_DIM = 128
MLA_ROPE_DIM = 64
MLA_QK_DIM = MLA_NOPE_DIM + MLA_ROPE_DIM
MLA_V_DIM = 128
MLA_Q_LORA = 512
MLA_KV_LORA = 256

RWKV_HEAD = 64
RWKV_WIDTH = D_MIX - MLA_HEADS * MLA_V_DIM
RWKV_HEADS = RWKV_WIDTH // RWKV_HEAD
RWKV_W_LORA = 64
RWKV_A_LORA = 64
RWKV_G_LORA = 160
RWKV_GN_EPS = 64e-5

SWA_HEADS = 8
SWA_HEAD_DIM = 128
SWA_PATTERNS = ((128, 1), (512, 4), (2048, 16))

GLA_HEADS = 4
GLA_DK = 128
GLA_DV = (D_MIX - SWA_HEADS * SWA_HEAD_DIM) // GLA_HEADS
GLA_GATE_LORA = 16
GLA_GATE_NORMALIZER = 16.0
GLA_CHUNK = 64

MLA_COLS = (MLA_Q_LORA, MLA_KV_LORA, MLA_ROPE_DIM)
RWKV_COLS = (RWKV_WIDTH, RWKV_WIDTH, RWKV_WIDTH, RWKV_W_LORA, RWKV_A_LORA, RWKV_G_LORA)
SWA_COLS = (SWA_HEADS * SWA_HEAD_DIM,) * 3
GLA_COLS = (GLA_HEADS * GLA_DK, GLA_HEADS * GLA_DK, GLA_HEADS * GLA_DV, GLA_GATE_LORA, GLA_HEADS * GLA_DV)
RWKV_IN = sum(RWKV_COLS)
EVEN_IN = sum(MLA_COLS) + RWKV_IN
ODD_IN = sum(SWA_COLS) + sum(GLA_COLS)
N_EVEN = (DEPTH + 1) // 2
N_ODD = DEPTH // 2

kernel_name = 'hybrid_mla_rwkv7_dilated_gla_macaron'


def split_cols(p, widths):
    offs = [int(o) for o in np.cumsum(widths)[:-1]]
    return jnp.split(p, offs, axis=-1)


def rmsnorm(x, g):
    xf = x.astype(jnp.float32)
    y = xf * lax.rsqrt(jnp.mean(xf * xf, axis=-1, keepdims=True) + NORM_EPS)
    return (y * g.astype(jnp.float32)).astype(x.dtype)


def head_groupnorm(x, w, b, eps):
    xf = x.astype(jnp.float32)
    mu = jnp.mean(xf, axis=-1, keepdims=True)
    var = jnp.mean(jnp.square(xf - mu), axis=-1, keepdims=True)
    y = ((xf - mu) * lax.rsqrt(var + eps)).reshape(x.shape[0], x.shape[1], -1)
    return (y * w.astype(jnp.float32) + b.astype(jnp.float32)).astype(x.dtype)


def rope(x, positions):
    half = x.shape[-1] // 2
    inv_freq = ROPE_THETA ** (-jnp.arange(half, dtype=jnp.float32) / half)
    ang = positions.astype(jnp.float32)[..., None] * inv_freq
    cos, sin = jnp.cos(ang)[:, :, None, :], jnp.sin(ang)[:, :, None, :]
    xf = x.astype(jnp.float32)
    x1, x2 = xf[..., :half], xf[..., half:]
    return jnp.concatenate([x1 * cos - x2 * sin, x1 * sin + x2 * cos], axis=-1).astype(x.dtype)


def token_shift(p):
    return jnp.pad(p, ((0, 0), (1, 0), (0, 0)))[:, :-1]


def swiglu(x, w_gate, w_up, w_down):
    return (jax.nn.silu(x @ w_gate) * (x @ w_up)) @ w_down


def causal_blocked_attention(q, k, v, scale):
    B, S, H, Dk = q.shape
    nb = S // Q_BLOCK
    qb = q.reshape(B, nb, Q_BLOCK, H, Dk).transpose(1, 0, 2, 3, 4)
    kpos = jnp.arange(S)

    def one_block(args):
        qblk, start = args
        s = jnp.einsum('bqhd,bkhd->bhqk', qblk, k).astype(jnp.float32) * scale
        qpos = start + jnp.arange(Q_BLOCK)
        s = jnp.where(kpos[None, :] <= qpos[:, None], s, -jnp.inf)
        p = jax.nn.softmax(s, axis=-1)
        return jnp.einsum('bhqk,bkhd->bqhd', p.astype(v.dtype), v)

    out = lax.map(one_block, (qb, jnp.arange(nb) * Q_BLOCK))
    return out.transpose(1, 0, 2, 3, 4).reshape(B, S, H, v.shape[-1])


def rwkv7_recurrence(r, decay, k, v, a, b):
    B, S, H, N = r.shape

    def step(state, inp):
        r_t, w_t, k_t, v_t, a_t, b_t = inp
        sa = jnp.einsum('bhij,bhj->bhi', state, a_t)
        state = state * w_t[:, :, None, :] + sa[..., None] * b_t[:, :, None, :] + v_t[..., None] * k_t[:, :, None, :]
        return state, jnp.einsum('bhij,bhj->bhi', state, r_t)

    xs = tuple(t.astype(jnp.float32).transpose(1, 0, 2, 3) for t in (r, decay, k, v, a, b))
    _, y = lax.scan(step, jnp.zeros((B, H, N, N), jnp.float32), xs)
    return y.transpose(1, 0, 2, 3).astype(v.dtype)


def dilated_window_attention(q, k, v, window, dilation):
    B, S, H, Dh = q.shape
    blk = window // dilation
    span = blk * dilation
    s_pad = -(-S // span) * span
    L = s_pad // dilation
    nb = L // blk

    def to_blocks(t):
        t = jnp.pad(t, ((0, 0), (0, s_pad - S), (0, 0), (0, 0)))
        t = t.reshape(B, L, dilation, H, Dh).transpose(0, 2, 1, 3, 4)
        return t.reshape(B, dilation, nb, blk, H, Dh)

    def with_prev(t):
        prev = jnp.pad(t, ((0, 0), (0, 0), (1, 0), (0, 0), (0, 0), (0, 0)))[:, :, :-1]
        return jnp.concatenate([prev, t], axis=3)

    qb = to_blocks(q)
    kc, vc = with_prev(to_blocks(k)), with_prev(to_blocks(v))
    s = jnp.einsum('brnqhd,brnkhd->brnhqk', qb, kc).astype(jnp.float32) * Dh ** -0.5
    qi = jnp.arange(blk)[:, None]
    ki = jnp.arange(2 * blk)[None, :]
    dist = blk + qi - ki
    band = (dist >= 0) & (dist <= blk)
    has_prev = (jnp.arange(nb) > 0)[:, None, None] | (ki >= blk)[None]
    mask = band[None] & has_prev
    s = jnp.where(mask[None, None, :, None], s, -jnp.inf)
    lse = jax.nn.logsumexp(s, axis=-1)
    p = jnp.exp(s - lse[..., None])
    o = jnp.einsum('brnhqk,brnkhd->brnqhd', p.astype(v.dtype), vc)
    o = o.reshape(B, dilation, L, H, Dh).transpose(0, 2, 1, 3, 4).reshape(B, s_pad, H, Dh)[:, :S]
    lse = lse.transpose(0, 1, 2, 4, 3).reshape(B, dilation, L, H).transpose(0, 2, 1, 3).reshape(B, s_pad, H)[:, :S]
    return o, lse


def gla_chunked(q, k, v, log_g):
    B, S, H, DK = q.shape
    DV = v.shape[-1]
    nc = S // GLA_CHUNK

    def chunks(t):
        return t.reshape(B, nc, GLA_CHUNK, H, t.shape[-1]).transpose(1, 0, 3, 2, 4)

    causal = jnp.tril(jnp.ones((GLA_CHUNK, GLA_CHUNK), dtype=bool))

    def step(state, inp):
        qc, kc, vc, gc = inp
        bcum = jnp.cumsum(gc.astype(jnp.float32), axis=2)
        o_inter = jnp.einsum('bhtk,bhkv->bhtv', qc * jnp.exp(bcum), state)
        rel = bcum[:, :, :, None, :] - bcum[:, :, None, :, :]
        dec = jnp.exp(jnp.where(causal[:, :, None], rel, -jnp.inf))
        att = jnp.einsum('bhtk,bhsk,bhtsk->bhts', qc, kc, dec)
        o_intra = jnp.einsum('bhts,bhsv->bhtv', att, vc)
        b_last = bcum[:, :, -1:, :]
        state = state * jnp.exp(b_last)[:, :, 0, :, None] + jnp.einsum('bhsk,bhsv->bhkv', kc * jnp.exp(b_last - bcum), vc)
        return state, o_inter + o_intra

    _, o = lax.scan(step, jnp.zeros((B, H, DK, DV), jnp.float32), (chunks(q), chunks(k), chunks(v), chunks(log_g)))
    return o.transpose(1, 0, 3, 2, 4).reshape(B, S, H, DV).astype(v.dtype)


def mla_rwkv_mixer(h, positions, w_in, q_norm, w_uq, kv_norm, w_ukv, q_head_norm, k_head_norm,
                   mu, w0, w2, a0, a2, g2, k_k, k_a, r_k, ln_w, ln_b, w_out):
    B, S, _ = h.shape
    p = h @ w_in
    c_q, c_kv, k_r, p_rwkv = split_cols(p, MLA_COLS + (RWKV_IN,))
    q = (rmsnorm(c_q, q_norm) @ w_uq).reshape(B, S, MLA_HEADS, MLA_QK_DIM)
    kv = (rmsnorm(c_kv, kv_norm) @ w_ukv).reshape(B, S, MLA_HEADS, MLA_NOPE_DIM + MLA_V_DIM)
    k_nope, v_a = kv[..., :MLA_NOPE_DIM], kv[..., MLA_NOPE_DIM:]
    k_rope = jnp.broadcast_to(k_r[:, :, None, :], (B, S, MLA_HEADS, MLA_ROPE_DIM))
    q = rmsnorm(q, q_head_norm)
    k = rmsnorm(jnp.concatenate([k_nope, k_rope], axis=-1), k_head_norm)
    q = jnp.concatenate([q[..., :MLA_NOPE_DIM], rope(q[..., MLA_NOPE_DIM:], positions)], axis=-1)
    k = jnp.concatenate([k[..., :MLA_NOPE_DIM], rope(k[..., MLA_NOPE_DIM:], positions)], axis=-1)
    o_a = causal_blocked_attention(q, k, v_a, MLA_QK_DIM ** -0.5).reshape(B, S, -1)
    xs = p_rwkv + (token_shift(p_rwkv) - p_rwkv) * mu
    r, k_b, v_b, xw, xa, xg = split_cols(xs, RWKV_COLS)
    w = -jax.nn.softplus(-(w0 + jnp.tanh(xw) @ w2)) - 0.5
    decay = jnp.exp(-jnp.exp(w.astype(jnp.float32)))
    a = jax.nn.sigmoid(a0 + xa @ a2)
    g = jax.nn.sigmoid(xg) @ g2

    def heads(t):
        return t.reshape(B, S, RWKV_HEADS, RWKV_HEAD)

    kk = heads(k_b * k_k)
    kk = kk / jnp.maximum(jnp.linalg.norm(kk, axis=-1, keepdims=True), 1e-12)
    k_h = heads(k_b * (1.0 + (a - 1.0) * k_a))
    r_h, v_h, a_h = heads(r), heads(v_b), heads(a)
    y = rwkv7_recurrence(r_h, heads(decay), k_h, v_h, -kk, kk * a_h)
    y = head_groupnorm(y, ln_w, ln_b, RWKV_GN_EPS)
    bonus = (jnp.sum(r_h * k_h * r_k, axis=-1, keepdims=True) * v_h).reshape(B, S, -1)
    o_b = (y + bonus) * g
    return jnp.concatenate([o_a, o_b], axis=-1) @ w_out


def swa_gla_mixer(h, positions, w_in, q_head_norm, k_head_norm, w_gate_up, b_gate, gla_norm, w_out):
    B, S, _ = h.shape
    p = h @ w_in
    q_c, k_c, v_c, q_d, k_d, v_d, g_lr, r_d = split_cols(p, SWA_COLS + GLA_COLS)
    def sh(t):
        return t.reshape(B, S, SWA_HEADS, SWA_HEAD_DIM)

    q_c = rope(rmsnorm(sh(q_c), q_head_norm), positions)
    k_c = rope(rmsnorm(sh(k_c), k_head_norm), positions)
    v_c = sh(v_c)
    outs, lses = [], []
    for window, dilation in SWA_PATTERNS:
        o_i, lse_i = dilated_window_attention(q_c, k_c, v_c, window, dilation)
        outs.append(o_i)
        lses.append(lse_i)
    wts = jax.nn.softmax(jnp.stack(lses, axis=0), axis=0)
    o_c = jnp.sum(wts[..., None] * jnp.stack(outs, axis=0).astype(jnp.float32), axis=0).astype(h.dtype).reshape(B, S, -1)
    def gh(t, n):
        return t.reshape(B, S, GLA_HEADS, n)

    log_g = jax.nn.log_sigmoid((g_lr @ w_gate_up + b_gate).astype(jnp.float32)) / GLA_GATE_NORMALIZER
    o_d = gla_chunked(gh(q_d, GLA_DK) * GLA_DK ** -0.5, gh(k_d, GLA_DK), gh(v_d, GLA_DV), gh(log_g, GLA_DK))
    o_d = rmsnorm(o_d, gla_norm).reshape(B, S, -1) * jax.nn.silu(r_d)
    return jnp.concatenate([o_c, o_d], axis=-1) @ w_out


def setup_inputs(seed: int = 0) -> dict:
    key = jax.random.key(seed)
    ks = iter(jax.random.split(key, 40))

    def dense(shape):
        return jax.random.normal(next(ks), shape, jnp.float32) * shape[-2] ** -0.5

    def around(shape, center, s):
        return center + s * jax.random.normal(next(ks), shape, jnp.float32)

    def unif(shape, lo, hi):
        return jax.random.uniform(next(ks), shape, jnp.float32, lo, hi)

    return {
        'x': jax.random.normal(next(ks), (BATCH, SEQ, D_MODEL), jnp.float32),
        'positions': jnp.broadcast_to(jnp.arange(SEQ, dtype=jnp.int32)[None, :], (BATCH, SEQ)),
        'ffn_norm': around((DEPTH, 2, D_MODEL), 1.0, 0.02),
        'ffn_w_gate': dense((DEPTH, 2, D_MODEL, D_FF)),
        'ffn_w_up': dense((DEPTH, 2, D_MODEL, D_FF)),
        'ffn_w_down': dense((DEPTH, 2, D_FF, D_MODEL)),
        'mix_norm': around((DEPTH, D_MODEL), 1.0, 0.02),
        'mla_rwkv_w_in': dense((N_EVEN, D_MODEL, EVEN_IN)),
        'mla_q_norm': around((N_EVEN, MLA_Q_LORA), 1.0, 0.02),
        'mla_w_uq': dense((N_EVEN, MLA_Q_LORA, MLA_HEADS * MLA_QK_DIM)),
        'mla_kv_norm': around((N_EVEN, MLA_KV_LORA), 1.0, 0.02),
        'mla_w_ukv': dense((N_EVEN, MLA_KV_LORA, MLA_HEADS * (MLA_NOPE_DIM + MLA_V_DIM))),
        'mla_q_head_norm': around((N_EVEN, MLA_QK_DIM), 1.0, 0.02),
        'mla_k_head_norm': around((N_EVEN, MLA_QK_DIM), 1.0, 0.02),
        'rwkv_mu': unif((N_EVEN, RWKV_IN), 0.0, 1.0),
        'rwkv_w0': unif((N_EVEN, RWKV_WIDTH), -6.0, 1.0),
        'rwkv_w2': dense((N_EVEN, RWKV_W_LORA, RWKV_WIDTH)),
        'rwkv_a0': around((N_EVEN, RWKV_WIDTH), 0.0, 0.5),
        'rwkv_a2': dense((N_EVEN, RWKV_A_LORA, RWKV_WIDTH)),
        'rwkv_g2': dense((N_EVEN, RWKV_G_LORA, RWKV_WIDTH)),
        'rwkv_k_k': around((N_EVEN, RWKV_WIDTH), 0.85, 0.05),
        'rwkv_k_a': around((N_EVEN, RWKV_WIDTH), 1.0, 0.05),
        'rwkv_r_k': around((N_EVEN, RWKV_HEADS, RWKV_HEAD), 0.0, 0.1),
        'rwkv_ln_w': around((N_EVEN, RWKV_WIDTH), 1.0, 0.02),
        'rwkv_ln_b': around((N_EVEN, RWKV_WIDTH), 0.0, 0.02),
        'mla_rwkv_w_out': dense((N_EVEN, D_MIX, D_MODEL)),
        'swa_gla_w_in': dense((N_ODD, D_MODEL, ODD_IN)),
        'swa_q_head_norm': around((N_ODD, SWA_HEAD_DIM), 1.0, 0.02),
        'swa_k_head_norm': around((N_ODD, SWA_HEAD_DIM), 1.0, 0.02),
        'gla_w_gate_up': dense((N_ODD, GLA_GATE_LORA, GLA_HEADS * GLA_DK)),
        'gla_b_gate': around((N_ODD, GLA_HEADS * GLA_DK), 0.0, 0.1),
        'gla_norm': around((N_ODD, GLA_HEADS, GLA_DV), 1.0, 0.02),
        'swa_gla_w_out': dense((N_ODD, D_MIX, D_MODEL)),
    }


def reference(x, positions, ffn_norm, ffn_w_gate, ffn_w_up, ffn_w_down, mix_norm,
              mla_rwkv_w_in, mla_q_norm, mla_w_uq, mla_kv_norm, mla_w_ukv, mla_q_head_norm, mla_k_head_norm,
              rwkv_mu, rwkv_w0, rwkv_w2, rwkv_a0, rwkv_a2, rwkv_g2, rwkv_k_k, rwkv_k_a, rwkv_r_k,
              rwkv_ln_w, rwkv_ln_b, mla_rwkv_w_out,
              swa_gla_w_in, swa_q_head_norm, swa_k_head_norm, gla_w_gate_up, gla_b_gate, gla_norm, swa_gla_w_out):
    h = x
    for layer in range(DEPTH):
        i = layer // 2
        h = h + MACARON_WEIGHT * swiglu(rmsnorm(h, ffn_norm[layer, 0]), ffn_w_gate[layer, 0], ffn_w_up[layer, 0], ffn_w_down[layer, 0])
        hn = rmsnorm(h, mix_norm[layer])
        if layer % 2 == 0:
            h = h + mla_rwkv_mixer(hn, positions, mla_rwkv_w_in[i], mla_q_norm[i], mla_w_uq[i], mla_kv_norm[i],
                                   mla_w_ukv[i], mla_q_head_norm[i], mla_k_head_norm[i],
                                   rwkv_mu[i], rwkv_w0[i], rwkv_w2[i], rwkv_a0[i], rwkv_a2[i], rwkv_g2[i],
                                   rwkv_k_k[i], rwkv_k_a[i], rwkv_r_k[i], rwkv_ln_w[i], rwkv_ln_b[i], mla_rwkv_w_out[i])
        else:
            h = h + swa_gla_mixer(hn, positions, swa_gla_w_in[i], swa_q_head_norm[i], swa_k_head_norm[i],
                                  gla_w_gate_up[i], gla_b_gate[i], gla_norm[i], swa_gla_w_out[i])
        h = h + MACARON_WEIGHT * swiglu(rmsnorm(h, ffn_norm[layer, 1]), ffn_w_gate[layer, 1], ffn_w_up[layer, 1], ffn_w_down[layer, 1])
    return h
```

```python
import functools

import numpy as np
import jax
import jax.numpy as jnp
from jax import lax
from jax.experimental import pallas as pl
from jax.experimental.pallas import tpu as pltpu

F32 = jnp.float32
BF16 = jnp.bfloat16

LANES = 128
VMEM_LIMIT = 56 * 1024 * 1024

D_MODEL = 2048
D_FF = 5632
MACARON_WEIGHT = 0.5
NORM_EPS = 1e-6
ROPE_THETA = 10000.0

MLA_HEADS = 8
MLA_NOPE = 128
MLA_ROPE = 64
MLA_QK = MLA_NOPE + MLA_ROPE
MLA_QK_PAD = 256
MLA_V = 128
MLA_Q_LORA = 512
MLA_KV_LORA = 256

RWKV_HEAD = 64
RWKV_WIDTH = 1024
RWKV_HEADS = 16
RWKV_W_LORA = 64
RWKV_A_LORA = 64
RWKV_G_LORA = 160
RWKV_GN_EPS = 64e-5
RWKV_SEC = 3584
RWKV_CHUNK = 64

SWA_HEADS = 8
SWA_DIM = 128
SWA_PATTERNS = ((128, 1), (512, 4), (2048, 16))

GLA_HEADS = 4
GLA_DK = 128
GLA_DV = 256
GLA_LORA = 16
GLA_NORMALIZER = 16.0
GLA_CHUNK = 64
GLA_SUB = 16


def _cparams(sem):
    return pltpu.CompilerParams(dimension_semantics=sem, vmem_limit_bytes=VMEM_LIMIT)


def _rms(x, g):
    return x * lax.rsqrt(jnp.mean(x * x, axis=-1, keepdims=True) + NORM_EPS) * g


def _dot(a, b):
    return jnp.dot(a.astype(BF16), b.astype(BF16), preferred_element_type=F32)


def _dot_nt(a, b):
    return lax.dot_general(a.astype(BF16), b.astype(BF16), (((1,), (1,)), ((), ())),
                           preferred_element_type=F32)


def _dot_split(a, b):
    b0 = b.astype(BF16)
    r1 = b - b0.astype(F32)
    b1 = r1.astype(BF16)
    b2 = (r1 - b1.astype(F32)).astype(BF16)
    a = a.astype(BF16)
    return (jnp.dot(a, b0, preferred_element_type=F32) + jnp.dot(a, b1, preferred_element_type=F32)
            + jnp.dot(a, b2, preferred_element_type=F32))


def _split_dot(a, b):
    a0 = a.astype(BF16)
    r1 = a - a0.astype(F32)
    a1 = r1.astype(BF16)
    a2 = (r1 - a1.astype(F32)).astype(BF16)
    b = b.astype(BF16)
    return (jnp.dot(a0, b, preferred_element_type=F32) + jnp.dot(a1, b, preferred_element_type=F32)
            + jnp.dot(a2, b, preferred_element_type=F32))


def _sigmoid(x):
    return 1.0 / (1.0 + jnp.exp(-x))


def _softplus(x):
    return jnp.maximum(x, 0.0) + jnp.log(1.0 + jnp.exp(-jnp.abs(x)))


def _rope_apply(x, cos, sin_signed):
    return x * cos + pltpu.roll(x, 64, 1) * sin_signed


def _ffn_kernel(h_ref, g_ref, wg_ref, wu_ref, wd_ref, o_ref, xn_ref):
    j = pl.program_id(1)

    @pl.when(j == 0)
    def _():
        x = h_ref[...]
        xn_ref[...] = _rms(x, g_ref[...]).astype(BF16)
        o_ref[...] = x

    xn = xn_ref[...]
    gate = jnp.dot(xn, wg_ref[...], preferred_element_type=F32)
    up = jnp.dot(xn, wu_ref[...], preferred_element_type=F32)
    act = (gate * _sigmoid(gate) * up * MACARON_WEIGHT).astype(BF16)
    o_ref[...] += jnp.dot(act, wd_ref[...], preferred_element_type=F32)


def _ffn(h, g, wg, wu, wd, layer, k, *, tm=512, tf=512):
    S, D = h.shape
    F = wg.shape[-1]
    tm = min(tm, S)
    wspec = pl.BlockSpec((None, None, D, tf), lambda i, j: (layer, k, 0, j))
    return pl.pallas_call(
        _ffn_kernel,
        out_shape=jax.ShapeDtypeStruct((S, D), F32),
        grid=(S // tm, F // tf),
        in_specs=[pl.BlockSpec((tm, D), lambda i, j: (i, 0)),
                  pl.BlockSpec((1, D), lambda i, j: (0, 0)),
                  wspec, wspec,
                  pl.BlockSpec((None, None, tf, D), lambda i, j: (layer, k, j, 0))],
        out_specs=pl.BlockSpec((tm, D), lambda i, j: (i, 0)),
        scratch_shapes=[pltpu.VMEM((tm, D), BF16)],
        compiler_params=_cparams(("parallel", "arbitrary")),
        name="ffn",
    )(h, g[layer, k][None, :], wg, wu, wd)


def _rope_tab_kernel(pos_ref, f1_ref, s1_ref, f2_ref, s2_ref, c1_ref, n1_ref, c2_ref, n2_ref):
    pos = pos_ref[...].astype(F32)
    a1 = pos * f1_ref[...]
    c1_ref[...] = jnp.cos(a1)
    n1_ref[...] = jnp.sin(a1) * s1_ref[...]
    a2 = pos * f2_ref[...]
    c2_ref[...] = jnp.cos(a2)
    n2_ref[...] = jnp.sin(a2) * s2_ref[...]


def _rope_tables(positions, *, tm=1024):
    S = positions.shape[0]
    tm = min(tm, S)
    f32half = ROPE_THETA ** (-jnp.arange(MLA_ROPE // 2, dtype=F32) / (MLA_ROPE // 2))
    z = jnp.zeros((32,), F32)
    f1 = jnp.concatenate([f32half, z, f32half, z])[None]
    f64half = ROPE_THETA ** (-jnp.arange(SWA_DIM // 2, dtype=F32) / (SWA_DIM // 2))
    f2 = jnp.concatenate([f64half, f64half])[None]
    sign = jnp.concatenate([-jnp.ones((64,), F32), jnp.ones((64,), F32)])[None]
    row = pl.BlockSpec((1, LANES), lambda i: (0, 0))
    tab = pl.BlockSpec((tm, LANES), lambda i: (i, 0))
    shp = jax.ShapeDtypeStruct((S, LANES), F32)
    return pl.pallas_call(
        _rope_tab_kernel,
        out_shape=(shp, shp, shp, shp),
        grid=(S // tm,),
        in_specs=[pl.BlockSpec((tm, 1), lambda i: (i, 0)), row, row, row, row],
        out_specs=(tab, tab, tab, tab),
        compiler_params=_cparams(("parallel",)),
        name="rope_tables",
    )(positions.reshape(S, 1), f1, sign, f2, sign)


def _nm_kernel(x_ref, g_ref, w_ref, o_ref, xn_ref):
    @pl.when(pl.program_id(1) == 0)
    def _():
        xn_ref[...] = _rms(x_ref[...], g_ref[...]).astype(BF16)

    o_ref[...] = jnp.dot(xn_ref[...], w_ref[...], preferred_element_type=F32).astype(o_ref.dtype)


def _norm_matmul(x, g, w, *, tm=512, tn=512, out_dtype=F32):
    S, K = x.shape
    N = w.shape[1]
    tm = min(tm, S)
    return pl.pallas_call(
        _nm_kernel,
        out_shape=jax.ShapeDtypeStruct((S, N), out_dtype),
        grid=(S // tm, N // tn),
        in_specs=[pl.BlockSpec((tm, K), lambda i, j: (i, 0)),
                  pl.BlockSpec((1, K), lambda i, j: (0, 0)),
                  pl.BlockSpec((K, tn), lambda i, j: (0, j))],
        out_specs=pl.BlockSpec((tm, tn), lambda i, j: (i, j)),
        scratch_shapes=[pltpu.VMEM((tm, K), BF16)],
        compiler_params=_cparams(("parallel", "arbitrary")),
        name="norm_matmul",
    )(x, g, w)


def _mla_q_kernel(x_ref, g_ref, w_ref, hg_ref, cos_ref, sin_ref, o_ref, xn_ref):
    @pl.when(pl.program_id(1) == 0)
    def _():
        xn_ref[...] = _rms(x_ref[...], g_ref[...]).astype(BF16)

    acc = jnp.dot(xn_ref[...], w_ref[...], preferred_element_type=F32)
    ss = jnp.sum(acc * acc, axis=-1, keepdims=True) * (1.0 / MLA_QK)
    y = acc * lax.rsqrt(ss + NORM_EPS) * hg_ref[...]
    o_ref[:, :LANES] = y[:, :LANES].astype(o_ref.dtype)
    o_ref[:, LANES:] = _rope_apply(y[:, LANES:], cos_ref[...], sin_ref[...]).astype(o_ref.dtype)


def _mla_q(p, g, w, hg, cos, sin, *, tm=512):
    S = p.shape[0]
    tm = min(tm, S)
    K = MLA_Q_LORA
    cq_block = RWKV_SEC // K
    tab = pl.BlockSpec((tm, LANES), lambda i, j: (i, 0))
    return pl.pallas_call(
        _mla_q_kernel,
        out_shape=jax.ShapeDtypeStruct((S, MLA_HEADS * MLA_QK_PAD), BF16),
        grid=(S // tm, MLA_HEADS),
        in_specs=[pl.BlockSpec((tm, K), lambda i, j: (i, cq_block)),
                  pl.BlockSpec((1, K), lambda i, j: (0, 0)),
                  pl.BlockSpec((K, MLA_QK_PAD), lambda i, j: (0, j)),
                  pl.BlockSpec((1, MLA_QK_PAD), lambda i, j: (0, 0)),
                  tab, tab],
        out_specs=pl.BlockSpec((tm, MLA_QK_PAD), lambda i, j: (i, j)),
        scratch_shapes=[pltpu.VMEM((tm, K), BF16)],
        compiler_params=_cparams(("parallel", "arbitrary")),
        name="mla_q",
    )(p, g, w, hg, cos, sin)


def _mla_kv_kernel(x_ref, g_ref, w_ref, kr_ref, gn_ref, gr_ref, cos_ref, sin_ref, k_ref, v_ref, xn_ref):
    @pl.when(pl.program_id(1) == 0)
    def _():
        xn_ref[...] = _rms(x_ref[...], g_ref[...]).astype(BF16)

    acc = jnp.dot(xn_ref[...], w_ref[...], preferred_element_type=F32)
    kn = acc[:, :LANES]
    kr = kr_ref[...]
    ss = (jnp.sum(kn * kn, axis=-1, keepdims=True) + jnp.sum(kr * kr, axis=-1, keepdims=True)) * (1.0 / MLA_QK)
    rs = lax.rsqrt(ss + NORM_EPS)
    k_ref[:, :LANES] = (kn * rs * gn_ref[...]).astype(k_ref.dtype)
    k_ref[:, LANES:] = _rope_apply(kr * rs * gr_ref[...], cos_ref[...], sin_ref[...]).astype(k_ref.dtype)
    v_ref[...] = acc[:, LANES:].astype(v_ref.dtype)


def _mla_kv(p, g, w, gn, gr, cos, sin, *, tm=512):
    S = p.shape[0]
    tm = min(tm, S)
    K = MLA_KV_LORA
    ckv_block = (RWKV_SEC + MLA_Q_LORA) // K
    kr_block = (RWKV_SEC + MLA_Q_LORA + MLA_KV_LORA) // LANES
    tab = pl.BlockSpec((tm, LANES), lambda i, j: (i, 0))
    row = pl.BlockSpec((1, LANES), lambda i, j: (0, 0))
    return pl.pallas_call(
        _mla_kv_kernel,
        out_shape=(jax.ShapeDtypeStruct((S, MLA_HEADS * MLA_QK_PAD), BF16),
                   jax.ShapeDtypeStruct((S, MLA_HEADS * MLA_V), BF16)),
        grid=(S // tm, MLA_HEADS),
        in_specs=[pl.BlockSpec((tm, K), lambda i, j: (i, ckv_block)),
                  pl.BlockSpec((1, K), lambda i, j: (0, 0)),
                  pl.BlockSpec((K, MLA_QK_PAD), lambda i, j: (0, j)),
                  pl.BlockSpec((tm, LANES), lambda i, j: (i, kr_block)),
                  row, row, tab, tab],
        out_specs=(pl.BlockSpec((tm, MLA_QK_PAD), lambda i, j: (i, j)),
                   pl.BlockSpec((tm, MLA_V), lambda i, j: (i, j))),
        scratch_shapes=[pltpu.VMEM((tm, K), BF16)],
        compiler_params=_cparams(("parallel", "arbitrary")),
        name="mla_kv",
    )(p, g, w, p, gn, gr, cos, sin)


def _flash_update(s, v, m_sc, l_sc, acc_sc, weight=None):
    m_prev = m_sc[...]
    m_new = jnp.maximum(m_prev, jnp.max(s, axis=-1, keepdims=True))
    alpha = jnp.exp(m_prev - m_new)
    pr = jnp.exp(s - m_new)
    if weight is not None:
        pr = pr * weight
    l_sc[...] = alpha * l_sc[...] + jnp.sum(pr, axis=-1, keepdims=True)
    acc_sc[...] = alpha * acc_sc[...] + jnp.dot(pr.astype(BF16), v, preferred_element_type=F32)
    m_sc[...] = m_new


def _mla_attn_kernel(qi_ref, ki_ref, q_ref, k_ref, v_ref, o_ref, m_sc, l_sc, acc_sc):
    p = pl.program_id(1)
    qi = qi_ref[p]
    ki = ki_ref[p]

    @pl.when(ki == 0)
    def _():
        m_sc[...] = jnp.full(m_sc.shape, -jnp.inf, F32)
        l_sc[...] = jnp.zeros(l_sc.shape, F32)
        acc_sc[...] = jnp.zeros(acc_sc.shape, F32)

    @pl.when(ki < qi)
    def _():
        s = _dot_nt(q_ref[...], k_ref[...])
        _flash_update(s, v_ref[...], m_sc, l_sc, acc_sc)

    @pl.when(ki == qi)
    def _():
        s = _dot_nt(q_ref[...], k_ref[...])
        row = lax.broadcasted_iota(jnp.int32, s.shape, 0)
        col = lax.broadcasted_iota(jnp.int32, s.shape, 1)
        s = jnp.where(col <= row, s, -jnp.inf)
        _flash_update(s, v_ref[...], m_sc, l_sc, acc_sc)
        o_ref[...] = (acc_sc[...] / l_sc[...]).astype(o_ref.dtype)


def _mla_attention(q, k, v, *, t=512):
    S = q.shape[0]
    t = min(t, S)
    nb = S // t
    pairs = [(a, b) for a in range(nb) for b in range(a + 1)]
    qi_tab = jnp.asarray(np.array([a for a, _ in pairs], np.int32))
    ki_tab = jnp.asarray(np.array([b for _, b in pairs], np.int32))
    grid_spec = pltpu.PrefetchScalarGridSpec(
        num_scalar_prefetch=2,
        grid=(MLA_HEADS, len(pairs)),
        in_specs=[pl.BlockSpec((t, MLA_QK_PAD), lambda h, p, qt, kt: (qt[p], h)),
                  pl.BlockSpec((t, MLA_QK_PAD), lambda h, p, qt, kt: (kt[p], h)),
                  pl.BlockSpec((t, MLA_V), lambda h, p, qt, kt: (kt[p], h))],
        out_specs=pl.BlockSpec((t, MLA_V), lambda h, p, qt, kt: (qt[p], h)),
        scratch_shapes=[pltpu.VMEM((t, 1), F32), pltpu.VMEM((t, 1), F32), pltpu.VMEM((t, MLA_V), F32)],
    )
    return pl.pallas_call(
        _mla_attn_kernel,
        out_shape=jax.ShapeDtypeStruct((S, MLA_HEADS * MLA_V), BF16),
        grid_spec=grid_spec,
        compiler_params=_cparams(("parallel", "arbitrary")),
        name="mla_attention",
    )(qi_tab, ki_tab, q, k, v)


def _seg_sum(x, e_ref):
    return _split_dot(x, e_ref[...])


def _rwkv_prep_kernel(p_ref, mu_ref, w0_ref, w2_ref, a0_ref, a2_ref, g2_ref, kk_ref, ka_ref, e_ref,
                      r_o, lw_o, k_o, v_o, kkn_o, b_o, g_o, carry):
    i = pl.program_id(0)

    @pl.when(i == 0)
    def _():
        carry[...] = jnp.zeros(carry.shape, F32)

    x = p_ref[...]
    tm = x.shape[0]
    row = lax.broadcasted_iota(jnp.int32, (tm, 1), 0)
    prev = jnp.where(row == 0, carry[...], pltpu.roll(x, 1, 0))
    carry[...] = x[tm - 1:tm, :]
    xs = x + (prev - x) * mu_ref[...]
    W = RWKV_WIDTH
    r = xs[:, :W]
    kb = xs[:, W:2 * W]
    vb = xs[:, 2 * W:3 * W]
    xw = xs[:, 3 * W:3 * W + LANES]
    xa = xs[:, 3 * W + LANES:3 * W + 2 * LANES]
    xg = xs[:, 3 * W + 2 * LANES:]
    w_raw = -_softplus(-(w0_ref[...] + _dot(jnp.tanh(xw), w2_ref[...]))) - 0.5
    a = _sigmoid(a0_ref[...] + _dot(xa, a2_ref[...]))
    kk = kb * kk_ref[...]
    nrm = jnp.sqrt(_seg_sum(kk * kk, e_ref))
    kkn = kk / jnp.maximum(nrm, 1e-12)
    r_o[...] = r
    lw_o[...] = -jnp.exp(w_raw)
    k_o[...] = kb * (1.0 + (a - 1.0) * ka_ref[...])
    v_o[...] = vb
    kkn_o[...] = kkn
    b_o[...] = kkn * a
    g_o[...] = _dot(_sigmoid(xg), g2_ref[...])


def _rwkv_prep(p, mu, w0, w2, a0, a2, g2, k_k, k_a, e, *, tm=256):
    S = p.shape[0]
    tm = min(tm, S)
    W = RWKV_WIDTH
    row = pl.BlockSpec((1, W), lambda i: (0, 0))
    out = pl.BlockSpec((tm, W), lambda i: (i, 0))
    shp = jax.ShapeDtypeStruct((S, W), F32)
    return pl.pallas_call(
        _rwkv_prep_kernel,
        out_shape=(shp,) * 7,
        grid=(S // tm,),
        in_specs=[pl.BlockSpec((tm, RWKV_SEC), lambda i: (i, 0)),
                  pl.BlockSpec((1, RWKV_SEC), lambda i: (0, 0)),
                  row, pl.BlockSpec((LANES, W), lambda i: (0, 0)),
                  row, pl.BlockSpec((LANES, W), lambda i: (0, 0)),
                  pl.BlockSpec((2 * LANES, W), lambda i: (0, 0)),
                  row, row, pl.BlockSpec((W, W), lambda i: (0, 0))],
        out_specs=(out,) * 7,
        scratch_shapes=[pltpu.VMEM((1, RWKV_SEC), F32)],
        compiler_params=_cparams(("arbitrary",)),
        name="rwkv_prep",
    )(p, mu, w0, w2, a0, a2, g2, k_k, k_a, e)


def _bmm(a, b):
    return jnp.einsum('gik,gkj->gij', a.astype(BF16), b.astype(BF16), preferred_element_type=F32)


def _bmm_nt(a, b):
    return jnp.einsum('gik,gjk->gij', a.astype(BF16), b.astype(BF16), preferred_element_type=F32)


def _rwkv_kernel(r_ref, lw_ref, k_ref, v_ref, kk_ref, b_ref, g_ref, lnw_ref, lnb_ref, rk_ref, tri_ref, e_ref,
                 o_ref, st_sc, wm_sc, z_sc, y0_sc, r2_sc, mrb_sc, bh_sc, gkv_sc, pc_sc, y_sc, *, hg, G):
    C = RWKV_CHUNK
    LW = hg * RWKV_HEAD
    RW = hg * C
    c_idx = pl.program_id(1)

    @pl.when(c_idx == 0)
    def _():
        st_sc[...] = jnp.zeros(st_sc.shape, F32)

    lw = lw_ref[...]
    cum = _dot_split(tri_ref[...], lw)
    cum3 = cum.reshape(G, C, LW)
    lw3 = lw.reshape(G, C, LW)
    clast = cum3[:, C - 1:C, :]
    r3 = r_ref[...].reshape(G, C, LW)
    k3 = k_ref[...].reshape(G, C, LW)
    v3 = v_ref[...].reshape(G, C, LW)
    kk3 = kk_ref[...].reshape(G, C, LW)
    b3 = b_ref[...].reshape(G, C, LW)
    einv = jnp.exp(-cum3)
    etail = jnp.exp(clast - cum3)
    rt = r3 * jnp.exp(cum3)
    at = -kk3 * jnp.exp(cum3 - lw3)
    bt = b3 * einv
    kt = k3 * einv
    bh = b3 * etail
    kh = k3 * etail

    lane_head = lax.broadcasted_iota(jnp.int32, (1, 1, LW), 2) // RWKV_HEAD

    def stack(x):
        return jnp.concatenate([jnp.where(lane_head == h, x, 0.0) for h in range(hg)], axis=1)

    a2, b2, k2, r2, v2, bh2, kh2 = (stack(t) for t in (at, bt, kt, rt, v3, bh, kh))
    ri = lax.broadcasted_iota(jnp.int32, (1, RW, RW), 1)
    ci = lax.broadcasted_iota(jnp.int32, (1, RW, RW), 2)
    same = (ri // C) == (ci // C)
    strict = same & (ci < ri)
    incl = same & (ci <= ri)
    eye = (ri == ci).astype(F32)

    lmat = jnp.where(strict, _bmm_nt(a2, b2), 0.0)
    akm = jnp.where(strict, _bmm_nt(a2, k2), 0.0)
    rbm = jnp.where(incl, _bmm_nt(r2, b2), 0.0)
    rkm = jnp.where(incl, _bmm_nt(r2, k2), 0.0)
    tinv = eye + lmat
    pw = lmat
    for _ in range(int(np.log2(C)) - 1):
        pw = _bmm(pw, pw)
        tinv = tinv + _bmm(tinv, pw)
    wm_sc[...] = _bmm(tinv, a2)
    z_sc[...] = _bmm(tinv, _bmm(akm, v2))
    y0_sc[...] = _bmm(rkm, v2)
    r2_sc[...] = r2
    mrb_sc[...] = rbm
    bh_sc[...] = bh2
    gkv_sc[...] = _bmm(jnp.swapaxes(v2, 1, 2), kh2)
    pc_sc[...] = jnp.exp(clast)

    def chunk(c, carry):
        st = st_sc[...]
        u2 = _dot_nt(wm_sc[c], st) + z_sc[c]
        y2 = _dot_nt(r2_sc[c], st) + _dot(mrb_sc[c], u2) + y0_sc[c]
        st_sc[...] = st * pc_sc[c] + _dot(u2.T, bh_sc[c]) + gkv_sc[c]
        y = y2[0:C]
        for h in range(1, hg):
            y = y + y2[h * C:(h + 1) * C]
        y_sc[c] = y
        return carry

    lax.fori_loop(0, G, chunk, 0)

    y = y_sc[...].reshape(G * C, LW)
    inv_n = 1.0 / RWKV_HEAD
    mu = _seg_sum(y, e_ref) * inv_n
    d = y - mu
    var = _seg_sum(d * d, e_ref) * inv_n
    yn = d * lax.rsqrt(var + RWKV_GN_EPS) * lnw_ref[...] + lnb_ref[...]
    r = r_ref[...]
    bonus = _seg_sum(r * k_ref[...] * rk_ref[...], e_ref) * v_ref[...]
    o_ref[...] = ((yn + bonus) * g_ref[...]).astype(o_ref.dtype)


def _rwkv_mix(r, lw, k, v, kkn, b, g, ln_w, ln_b, r_k, *, hg=2, G=8):
    S = r.shape[0]
    C = RWKV_CHUNK
    G = min(G, S // C)
    TB = G * C
    LW = hg * RWKV_HEAD
    RW = hg * C
    ngrp = RWKV_WIDTH // LW
    t = np.arange(TB)
    tri = jnp.asarray(((t[:, None] // C == t[None, :] // C) & (t[None, :] <= t[:, None])).astype(np.float32), BF16)
    l = np.arange(LW)
    e = jnp.asarray((l[:, None] // RWKV_HEAD == l[None, :] // RWKV_HEAD).astype(np.float32), BF16)
    blk = pl.BlockSpec((TB, LW), lambda gi, c: (c, gi))
    row = pl.BlockSpec((1, LW), lambda gi, c: (0, gi))
    mat = lambda: pltpu.VMEM((G, RW, LW), F32)
    return pl.pallas_call(
        functools.partial(_rwkv_kernel, hg=hg, G=G),
        out_shape=jax.ShapeDtypeStruct((S, RWKV_WIDTH), BF16),
        grid=(ngrp, S // TB),
        in_specs=[blk] * 7 + [row] * 3 + [pl.BlockSpec((TB, TB), lambda gi, c: (0, 0)),
                                          pl.BlockSpec((LW, LW), lambda gi, c: (0, 0))],
        out_specs=blk,
        scratch_shapes=[pltpu.VMEM((LW, LW), F32), mat(), mat(), mat(), mat(),
                        pltpu.VMEM((G, RW, RW), F32), mat(), pltpu.VMEM((G, LW, LW), F32),
                        pltpu.VMEM((G, 1, LW), F32), pltpu.VMEM((G, C, LW), F32)],
        compiler_params=_cparams(("parallel", "arbitrary")),
        name="rwkv_mix",
    )(r, lw, k, v, kkn, b, g, ln_w, ln_b, r_k, tri, e)


def _outproj_kernel(h_ref, a_ref, b_ref, wa_ref, wb_ref, o_ref):
    o_ref[...] = (h_ref[...] + jnp.dot(a_ref[...], wa_ref[...], preferred_element_type=F32)
                  + jnp.dot(b_ref[...], wb_ref[...], preferred_element_type=F32))


def _outproj(h, a, b, wa, wb, *, tm=512, tn=512):
    S, D = h.shape
    tm = min(tm, S)
    Ka, Kb = a.shape[1], b.shape[1]
    return pl.pallas_call(
        _outproj_kernel,
        out_shape=jax.ShapeDtypeStruct((S, D), F32),
        grid=(S // tm, D // tn),
        in_specs=[pl.BlockSpec((tm, tn), lambda i, j: (i, j)),
                  pl.BlockSpec((tm, Ka), lambda i, j: (i, 0)),
                  pl.BlockSpec((tm, Kb), lambda i, j: (i, 0)),
                  pl.BlockSpec((Ka, tn), lambda i, j: (0, j)),
                  pl.BlockSpec((Kb, tn), lambda i, j: (0, j))],
        out_specs=pl.BlockSpec((tm, tn), lambda i, j: (i, j)),
        compiler_params=_cparams(("parallel", "arbitrary")),
        name="outproj",
    )(h, a, b, wa, wb)


def _pad_cols(w, width):
    return jnp.pad(w, ((0, 0), (0, width - w.shape[1])))


def _rope_gap_layout(w):
    z = jnp.zeros(w.shape[:-1] + (32,), w.dtype)
    return jnp.concatenate([w[..., :32], z, w[..., 32:], z], axis=-1)


def _even_mixer(h, tabs, mix_norm, w_in, q_norm, w_uq, kv_norm, w_ukv, q_head_norm, k_head_norm,
                mu, w0, w2, a0, a2, g2, k_k, k_a, r_k, ln_w, ln_b, w_out):
    cos1, sin1 = tabs[0], tabs[1]
    W = RWKV_WIDTH
    o_cq = 0
    o_ckv = MLA_Q_LORA
    o_kr = o_ckv + MLA_KV_LORA
    o_rw = o_kr + MLA_ROPE
    rw = w_in[:, o_rw:]
    mu_r = mu[None, :]

    def rwkv_layout(t):
        return jnp.concatenate([t[:, :3 * W], _pad_cols(t[:, 3 * W:3 * W + RWKV_W_LORA], LANES),
                                _pad_cols(t[:, 3 * W + RWKV_W_LORA:3 * W + RWKV_W_LORA + RWKV_A_LORA], LANES),
                                _pad_cols(t[:, 3 * W + RWKV_W_LORA + RWKV_A_LORA:], 2 * LANES)], axis=1)

    w_in_p = jnp.concatenate([rwkv_layout(rw), w_in[:, o_cq:o_ckv], w_in[:, o_ckv:o_kr],
                              _rope_gap_layout(w_in[:, o_kr:o_rw])], axis=1)
    w_in_p = _pad_cols(w_in_p, 4608).astype(BF16)
    p = _norm_matmul(h, mix_norm[None, :], w_in_p)

    wq = w_uq.reshape(MLA_Q_LORA, MLA_HEADS, MLA_QK)
    wq = jnp.concatenate([wq[..., :MLA_NOPE], _rope_gap_layout(wq[..., MLA_NOPE:])], axis=-1)
    wq = wq.reshape(MLA_Q_LORA, MLA_HEADS * MLA_QK_PAD).astype(BF16)
    scale = MLA_QK ** -0.5
    hg_q = jnp.concatenate([q_head_norm[:MLA_NOPE], _rope_gap_layout(q_head_norm[MLA_NOPE:])])[None, :] * scale
    q = _mla_q(p, q_norm[None, :], wq, hg_q, cos1, sin1)
    gn = k_head_norm[None, :MLA_NOPE]
    gr = _rope_gap_layout(k_head_norm[MLA_NOPE:])[None, :]
    kmat, vmat = _mla_kv(p, kv_norm[None, :], w_ukv.astype(BF16), gn, gr, cos1, sin1)
    o_a = _mla_attention(q, kmat, vmat)

    l = np.arange(W)
    e = jnp.asarray((l[:, None] // RWKV_HEAD == l[None, :] // RWKV_HEAD).astype(np.float32), BF16)
    w2p = jnp.pad(w2, ((0, LANES - RWKV_W_LORA), (0, 0))).astype(BF16)
    a2p = jnp.pad(a2, ((0, LANES - RWKV_A_LORA), (0, 0))).astype(BF16)
    g2p = jnp.pad(g2, ((0, 2 * LANES - RWKV_G_LORA), (0, 0))).astype(BF16)
    r, lw, k, v, kkn, b, g = _rwkv_prep(p, rwkv_layout(mu_r), w0[None, :], w2p, a0[None, :], a2p, g2p,
                                         k_k[None, :], k_a[None, :], e)
    o_b = _rwkv_mix(r, lw, k, v, kkn, b, g, ln_w[None, :], ln_b[None, :], r_k.reshape(1, W))
    wo = w_out.astype(BF16)
    return _outproj(h, o_a, o_b, wo[:MLA_HEADS * MLA_V], wo[MLA_HEADS * MLA_V:])


def _swa_qkv_kernel(x_ref, g_ref, w_ref, hg_ref, cos_ref, sin_ref, o_ref, xn_ref, *, n_normed):
    j = pl.program_id(1)

    @pl.when(j == 0)
    def _():
        xn_ref[...] = _rms(x_ref[...], g_ref[...]).astype(BF16)

    acc = jnp.dot(xn_ref[...], w_ref[...], preferred_element_type=F32)

    @pl.when(j < n_normed)
    def _():
        for c in range(acc.shape[1] // LANES):
            sl = slice(c * LANES, (c + 1) * LANES)
            y = _rms(acc[:, sl], hg_ref[:, sl])
            o_ref[:, sl] = _rope_apply(y, cos_ref[...], sin_ref[...]).astype(o_ref.dtype)

    @pl.when(j >= n_normed)
    def _():
        o_ref[...] = acc.astype(o_ref.dtype)


def _swa_qkv(h, g, w, hg, cos, sin, *, tm=512, tn=512):
    S, K = h.shape
    N = w.shape[1]
    tm = min(tm, S)
    n_normed = (2 * SWA_HEADS * SWA_DIM) // tn
    tab = pl.BlockSpec((tm, LANES), lambda i, j: (i, 0))
    return pl.pallas_call(
        functools.partial(_swa_qkv_kernel, n_normed=n_normed),
        out_shape=jax.ShapeDtypeStruct((S, N), BF16),
        grid=(S // tm, N // tn),
        in_specs=[pl.BlockSpec((tm, K), lambda i, j: (i, 0)),
                  pl.BlockSpec((1, K), lambda i, j: (0, 0)),
                  pl.BlockSpec((K, tn), lambda i, j: (0, j)),
                  pl.BlockSpec((1, tn), lambda i, j: (0, jnp.minimum(j, n_normed - 1))),
                  tab, tab],
        out_specs=pl.BlockSpec((tm, tn), lambda i, j: (i, j)),
        scratch_shapes=[pltpu.VMEM((tm, K), BF16)],
        compiler_params=_cparams(("parallel", "arbitrary")),
        name="swa_qkv",
    )(h, g, w, hg, cos, sin)


def _dilated_kernel(q_ref, k_ref, v_ref, o_ref, m_sc, l_sc, acc_sc, *, t, nrel):
    qi = pl.program_id(1)
    r = pl.program_id(2)

    @pl.when(r == 0)
    def _():
        m_sc[...] = jnp.full(m_sc.shape, -jnp.inf, F32)
        l_sc[...] = jnp.zeros(l_sc.shape, F32)
        acc_sc[...] = jnp.zeros(acc_sc.shape, F32)

    @pl.when(qi - r >= 0)
    def _():
        s = _dot_nt(q_ref[...], k_ref[...])
        row = lax.broadcasted_iota(jnp.int32, s.shape, 0)
        col = lax.broadcasted_iota(jnp.int32, s.shape, 1)
        delta = r * t + row - col
        mult = jnp.zeros(s.shape, F32)
        for window, dilation in SWA_PATTERNS:
            hit = (delta >= 0) & (delta <= window) & ((delta & (dilation - 1)) == 0)
            mult = mult + jnp.where(hit, 1.0, 0.0)
        s = jnp.where(mult > 0.0, s, -jnp.inf)
        _flash_update(s, v_ref[...], m_sc, l_sc, acc_sc, weight=mult)

    @pl.when(r == nrel - 1)
    def _():
        o_ref[...] = (acc_sc[...] / l_sc[...]).astype(o_ref.dtype)


def _dilated_attention(qkv, *, t=512):
    S = qkv.shape[0]
    t = min(t, S)
    max_window = max(w for w, _ in SWA_PATTERNS)
    nrel = min(-(-max_window // t) + 1, S // t)
    H = SWA_HEADS
    return pl.pallas_call(
        functools.partial(_dilated_kernel, t=t, nrel=nrel),
        out_shape=jax.ShapeDtypeStruct((S, H * SWA_DIM), BF16),
        grid=(H, S // t, nrel),
        in_specs=[pl.BlockSpec((t, SWA_DIM), lambda h, i, r: (i, h)),
                  pl.BlockSpec((t, SWA_DIM), lambda h, i, r: (jnp.maximum(i - r, 0), H + h)),
                  pl.BlockSpec((t, SWA_DIM), lambda h, i, r: (jnp.maximum(i - r, 0), 2 * H + h))],
        out_specs=pl.BlockSpec((t, SWA_DIM), lambda h, i, r: (i, h)),
        scratch_shapes=[pltpu.VMEM((t, 1), F32), pltpu.VMEM((t, 1), F32), pltpu.VMEM((t, SWA_DIM), F32)],
        compiler_params=_cparams(("parallel", "parallel", "arbitrary")),
        name="dilated_attention",
    )(qkv, qkv, qkv)


def _gla_kernel(q_ref, k_ref, v_ref, glr_ref, wgu_ref, bg_ref, rd_ref, gn_ref, tri_ref, o_ref,
                st_sc, qt_sc, gkv_sc, eb_sc, oi_sc, *, G):
    C = GLA_CHUNK
    SB = GLA_SUB
    DK = GLA_DK
    DV = GLA_DV
    c_idx = pl.program_id(1)

    @pl.when(c_idx == 0)
    def _():
        st_sc[...] = jnp.zeros(st_sc.shape, F32)

    z = _dot(glr_ref[...], wgu_ref[...]) + bg_ref[...]
    lg = -_softplus(-z) * (1.0 / GLA_NORMALIZER)
    bc = _dot_split(tri_ref[...], lg).reshape(G, C, DK)
    q = (q_ref[...] * (DK ** -0.5)).reshape(G, C, DK)
    k = k_ref[...].reshape(G, C, DK)
    v = v_ref[...].reshape(G, C, DV)
    blast = bc[:, C - 1:C, :]
    qt_sc[...] = q * jnp.exp(bc)
    khat = k * jnp.exp(blast - bc)
    gkv_sc[...] = _bmm(jnp.swapaxes(v, 1, 2), khat)
    eb_sc[...] = jnp.exp(blast)

    ti = lax.broadcasted_iota(jnp.int32, (1, SB, SB, 1), 1)
    si = lax.broadcasted_iota(jnp.int32, (1, SB, SB, 1), 2)
    causal = si <= ti
    for sb in range(C // SB):
        lo = sb * SB
        qs = q[:, lo:lo + SB]
        ks = k[:, lo:lo + SB]
        bs = bc[:, lo:lo + SB]
        rel = bs[:, :, None, :] - bs[:, None, :, :]
        dec = jnp.exp(jnp.where(causal, rel, -jnp.inf))
        att = jnp.sum(qs[:, :, None, :] * ks[:, None, :, :] * dec, axis=-1)
        o_sb = _bmm(att, v[:, lo:lo + SB])
        if sb > 0:
            bm = bc[:, lo - 1:lo]
            qsc = qs * jnp.exp(bs - bm)
            ksc = k[:, :lo] * jnp.exp(bm - bc[:, :lo])
            o_sb = o_sb + _bmm(_bmm_nt(qsc, ksc), v[:, :lo])
        oi_sc[:, lo:lo + SB, :] = o_sb

    def chunk(c, carry):
        st = st_sc[...]
        oi_sc[c] = oi_sc[c] + _dot_nt(qt_sc[c], st)
        st_sc[...] = st * eb_sc[c] + gkv_sc[c]
        return carry

    lax.fori_loop(0, G, chunk, 0)

    o = oi_sc[...].reshape(G * C, DV)
    rd = rd_ref[...]
    o_ref[...] = (_rms(o, gn_ref[...]) * (rd * _sigmoid(rd))).astype(o_ref.dtype)


def _gla(pb, wgu, bg, gn, *, G=8):
    S = pb.shape[0]
    C = GLA_CHUNK
    G = min(G, S // C)
    TB = G * C
    H, DK, DV = GLA_HEADS, GLA_DK, GLA_DV
    t = np.arange(TB)
    tri = jnp.asarray(((t[:, None] // C == t[None, :] // C) & (t[None, :] <= t[:, None])).astype(np.float32), BF16)
    glr_block = (2 * H * DK + 2 * H * DV) // LANES
    return pl.pallas_call(
        functools.partial(_gla_kernel, G=G),
        out_shape=jax.ShapeDtypeStruct((S, H * DV), BF16),
        grid=(H, S // TB),
        in_specs=[pl.BlockSpec((TB, DK), lambda h, c: (c, h)),
                  pl.BlockSpec((TB, DK), lambda h, c: (c, H + h)),
                  pl.BlockSpec((TB, DV), lambda h, c: (c, (2 * H * DK) // DV + h)),
                  pl.BlockSpec((TB, LANES), lambda h, c: (c, glr_block)),
                  pl.BlockSpec((LANES, DK), lambda h, c: (0, h)),
                  pl.BlockSpec((1, DK), lambda h, c: (0, h)),
                  pl.BlockSpec((TB, DV), lambda h, c: (c, (2 * H * DK + H * DV) // DV + h)),
                  pl.BlockSpec((1, DV), lambda h, c: (0, h)),
                  pl.BlockSpec((TB, TB), lambda h, c: (0, 0))],
        out_specs=pl.BlockSpec((TB, DV), lambda h, c: (c, h)),
        scratch_shapes=[pltpu.VMEM((DV, DK), F32), pltpu.VMEM((G, C, DK), F32), pltpu.VMEM((G, DV, DK), F32),
                        pltpu.VMEM((G, 1, DK), F32), pltpu.VMEM((G, C, DV), F32)],
        compiler_params=_cparams(("parallel", "arbitrary")),
        name="gla",
    )(pb, pb, pb, pb, wgu, bg, pb, gn, tri)


def _odd_mixer(h, tabs, mix_norm, w_in, q_head_norm, k_head_norm, w_gate_up, b_gate, gla_norm, w_out):
    cos2, sin2 = tabs[2], tabs[3]
    nq = SWA_HEADS * SWA_DIM
    g_row = mix_norm[None, :]
    hg = jnp.concatenate([jnp.tile(q_head_norm * SWA_DIM ** -0.5, SWA_HEADS), jnp.tile(k_head_norm, SWA_HEADS)])[None, :]
    qkv = _swa_qkv(h, g_row, w_in[:, :3 * nq].astype(BF16), hg, cos2, sin2)
    o_c = _dilated_attention(qkv)
    o = 3 * nq
    dk, dv = GLA_HEADS * GLA_DK, GLA_HEADS * GLA_DV
    wb = jnp.concatenate([w_in[:, o:o + 2 * dk + dv], w_in[:, o + 2 * dk + dv + GLA_LORA:],
                          _pad_cols(w_in[:, o + 2 * dk + dv:o + 2 * dk + dv + GLA_LORA], LANES)], axis=1)
    wb = _pad_cols(wb, 3584).astype(BF16)
    pb = _norm_matmul(h, g_row, wb)
    wgu = jnp.pad(w_gate_up, ((0, LANES - GLA_LORA), (0, 0))).astype(BF16)
    o_d = _gla(pb, wgu, b_gate[None, :], gla_norm.reshape(1, dv))
    wo = w_out.astype(BF16)
    return _outproj(h, o_c, o_d, wo[:nq], wo[nq:])


def kernel(x, positions, ffn_norm, ffn_w_gate, ffn_w_up, ffn_w_down, mix_norm, mla_rwkv_w_in, mla_q_norm, mla_w_uq,
           mla_kv_norm, mla_w_ukv, mla_q_head_norm, mla_k_head_norm, rwkv_mu, rwkv_w0, rwkv_w2, rwkv_a0, rwkv_a2,
           rwkv_g2, rwkv_k_k, rwkv_k_a, rwkv_r_k, rwkv_ln_w, rwkv_ln_b, mla_rwkv_w_out, swa_gla_w_in,
           swa_q_head_norm, swa_k_head_norm, gla_w_gate_up, gla_b_gate, gla_norm, swa_gla_w_out):
    B, S, D = x.shape
    assert B == 1
    h = x.reshape(S, D)
    tabs = _rope_tables(positions.reshape(S))
    wg = ffn_w_gate.astype(BF16)
    wu = ffn_w_up.astype(BF16)
    wd = ffn_w_down.astype(BF16)
    depth = ffn_norm.shape[0]
    for layer in range(depth):
        i = layer // 2
        h = _ffn(h, ffn_norm, wg, wu, wd, layer, 0)
        if layer % 2 == 0:
            h = _even_mixer(h, tabs, mix_norm[layer], mla_rwkv_w_in[i], mla_q_norm[i], mla_w_uq[i], mla_kv_norm[i],
                            mla_w_ukv[i], mla_q_head_norm[i], mla_k_head_norm[i], rwkv_mu[i], rwkv_w0[i],
                            rwkv_w2[i], rwkv_a0[i], rwkv_a2[i], rwkv_g2[i], rwkv_k_k[i], rwkv_k_a[i], rwkv_r_k[i],
                            rwkv_ln_w[i], rwkv_ln_b[i], mla_rwkv_w_out[i])
        else:
            h = _odd_mixer(h, tabs, mix_norm[layer], swa_gla_w_in[i], swa_q_head_norm[i], swa_k_head_norm[i],
                           gla_w_gate_up[i], gla_b_gate[i], gla_norm[i], swa_gla_w_out[i])
        h = _ffn(h, ffn_norm, wg, wu, wd, layer, 1)
    return h.reshape(B, S, D)
```

```python
import functools

import numpy as np
import jax
import jax.numpy as jnp
from jax import lax
from jax.experimental import pallas as pl
from jax.experimental.pallas import tpu as pltpu

F32 = jnp.float32
BF16 = jnp.bfloat16

LANES = 128
VMEM_LIMIT = 56 * 1024 * 1024

D_MODEL = 2048
D_FF = 5632
MACARON_WEIGHT = 0.5
NORM_EPS = 1e-6
ROPE_THETA = 10000.0

MLA_HEADS = 8
MLA_NOPE = 128
MLA_ROPE = 64
MLA_QK = MLA_NOPE + MLA_ROPE
MLA_QK_PAD = 256
MLA_V = 128
MLA_Q_LORA = 512
MLA_KV_LORA = 256

RWKV_HEAD = 64
RWKV_WIDTH = 1024
RWKV_HEADS = 16
RWKV_W_LORA = 64
RWKV_A_LORA = 64
RWKV_G_LORA = 160
RWKV_GN_EPS = 64e-5
RWKV_SEC = 3584
RWKV_CHUNK = 64

SWA_HEADS = 8
SWA_DIM = 128
SWA_PATTERNS = ((128, 1), (512, 4), (2048, 16))

GLA_HEADS = 4
GLA_DK = 128
GLA_DV = 256
GLA_LORA = 16
GLA_NORMALIZER = 16.0
GLA_CHUNK = 64
GLA_SUB = 16


def _cparams(sem):
    return pltpu.CompilerParams(dimension_semantics=sem, vmem_limit_bytes=VMEM_LIMIT)


def _rms(x, g):
    return x * lax.rsqrt(jnp.mean(x * x, axis=-1, keepdims=True) + NORM_EPS) * g


def _dot(a, b):
    return jnp.dot(a.astype(BF16), b.astype(BF16), preferred_element_type=F32)


def _dot_nt(a, b):
    return lax.dot_general(a.astype(BF16), b.astype(BF16), (((1,), (1,)), ((), ())),
                           preferred_element_type=F32)


def _dot_split(a, b):
    b0 = b.astype(BF16)
    r1 = b - b0.astype(F32)
    b1 = r1.astype(BF16)
    b2 = (r1 - b1.astype(F32)).astype(BF16)
    a = a.astype(BF16)
    return (jnp.dot(a, b0, preferred_element_type=F32) + jnp.dot(a, b1, preferred_element_type=F32)
            + jnp.dot(a, b2, preferred_element_type=F32))


def _split_dot(a, b):
    a0 = a.astype(BF16)
    r1 = a - a0.astype(F32)
    a1 = r1.astype(BF16)
    a2 = (r1 - a1.astype(F32)).astype(BF16)
    b = b.astype(BF16)
    return (jnp.dot(a0, b, preferred_element_type=F32) + jnp.dot(a1, b, preferred_element_type=F32)
            + jnp.dot(a2, b, preferred_element_type=F32))


def _sigmoid(x):
    return 1.0 / (1.0 + jnp.exp(-x))


def _softplus(x):
    return jnp.maximum(x, 0.0) + jnp.log(1.0 + jnp.exp(-jnp.abs(x)))


def _rope_apply(x, cos, sin_signed):
    return x * cos + pltpu.roll(x, 64, 1) * sin_signed


def _ffn_kernel(h_ref, g_ref, wg_ref, wu_ref, wd_ref, o_ref, xn_ref):
    j = pl.program_id(1)

    @pl.when(j == 0)
    def _():
        x = h_ref[...]
        xn_ref[...] = _rms(x, g_ref[...]).astype(BF16)
        o_ref[...] = x

    xn = xn_ref[...]
    gate = jnp.dot(xn, wg_ref[...], preferred_element_type=F32)
    up = jnp.dot(xn, wu_ref[...], preferred_element_type=F32)
    act = (gate * _sigmoid(gate) * up * MACARON_WEIGHT).astype(BF16)
    o_ref[...] += jnp.dot(act, wd_ref[...], preferred_element_type=F32)


def _ffn(h, g, wg, wu, wd, layer, k, *, tm=512, tf=512):
    S, D = h.shape
    F = wg.shape[-1]
    tm = min(tm, S)
    wspec = pl.BlockSpec((None, None, D, tf), lambda i, j: (layer, k, 0, j))
    return pl.pallas_call(
        _ffn_kernel,
        out_shape=jax.ShapeDtypeStruct((S, D), F32),
        grid=(S // tm, F // tf),
        in_specs=[pl.BlockSpec((tm, D), lambda i, j: (i, 0)),
                  pl.BlockSpec((1, D), lambda i, j: (0, 0)),
                  wspec, wspec,
                  pl.BlockSpec((None, None, tf, D), lambda i, j: (layer, k, j, 0))],
        out_specs=pl.BlockSpec((tm, D), lambda i, j: (i, 0)),
        scratch_shapes=[pltpu.VMEM((tm, D), BF16)],
        compiler_params=_cparams(("parallel", "arbitrary")),
        name="ffn",
    )(h, g[layer, k][None, :], wg, wu, wd)


def _rope_tab_kernel(pos_ref, f1_ref, s1_ref, f2_ref, s2_ref, c1_ref, n1_ref, c2_ref, n2_ref):
    pos = pos_ref[...].astype(F32)
    a1 = pos * f1_ref[...]
    c1_ref[...] = jnp.cos(a1)
    n1_ref[...] = jnp.sin(a1) * s1_ref[...]
    a2 = pos * f2_ref[...]
    c2_ref[...] = jnp.cos(a2)
    n2_ref[...] = jnp.sin(a2) * s2_ref[...]


def _rope_tables(positions, *, tm=1024):
    S = positions.shape[0]
    tm = min(tm, S)
    f32half = ROPE_THETA ** (-jnp.arange(MLA_ROPE // 2, dtype=F32) / (MLA_ROPE // 2))
    z = jnp.zeros((32,), F32)
    f1 = jnp.concatenate([f32half, z, f32half, z])[None]
    f64half = ROPE_THETA ** (-jnp.arange(SWA_DIM // 2, dtype=F32) / (SWA_DIM // 2))
    f2 = jnp.concatenate([f64half, f64half])[None]
    sign = jnp.concatenate([-jnp.ones((64,), F32), jnp.ones((64,), F32)])[None]
    row = pl.BlockSpec((1, LANES), lambda i: (0, 0))
    tab = pl.BlockSpec((tm, LANES), lambda i: (i, 0))
    shp = jax.ShapeDtypeStruct((S, LANES), F32)
    return pl.pallas_call(
        _rope_tab_kernel,
        out_shape=(shp, shp, shp, shp),
        grid=(S // tm,),
        in_specs=[pl.BlockSpec((tm, 1), lambda i: (i, 0)), row, row, row, row],
        out_specs=(tab, tab, tab, tab),
        compiler_params=_cparams(("parallel",)),
        name="rope_tables",
    )(positions.reshape(S, 1), f1, sign, f2, sign)


def _nm_kernel(x_ref, g_ref, w_ref, o_ref, xn_ref):
    @pl.when(pl.program_id(1) == 0)
    def _():
        xn_ref[...] = _rms(x_ref[...], g_ref[...]).astype(BF16)

    o_ref[...] = jnp.dot(xn_ref[...], w_ref[...], preferred_element_type=F32).astype(o_ref.dtype)


def _norm_matmul(x, g, w, *, tm=512, tn=512, out_dtype=F32):
    S, K = x.shape
    N = w.shape[1]
    tm = min(tm, S)
    return pl.pallas_call(
        _nm_kernel,
        out_shape=jax.ShapeDtypeStruct((S, N), out_dtype),
        grid=(S // tm, N // tn),
        in_specs=[pl.BlockSpec((tm, K), lambda i, j: (i, 0)),
                  pl.BlockSpec((1, K), lambda i, j: (0, 0)),
                  pl.BlockSpec((K, tn), lambda i, j: (0, j))],
        out_specs=pl.BlockSpec((tm, tn), lambda i, j: (i, j)),
        scratch_shapes=[pltpu.VMEM((tm, K), BF16)],
        compiler_params=_cparams(("parallel", "arbitrary")),
        name="norm_matmul",
    )(x, g, w)


def _mla_q_kernel(x_ref, g_ref, w_ref, hg_ref, cos_ref, sin_ref, o_ref, xn_ref):
    @pl.when(pl.program_id(1) == 0)
    def _():
        xn_ref[...] = _rms(x_ref[...], g_ref[...]).astype(BF16)

    acc = jnp.dot(xn_ref[...], w_ref[...], preferred_element_type=F32)
    ss = jnp.sum(acc * acc, axis=-1, keepdims=True) * (1.0 / MLA_QK)
    y = acc * lax.rsqrt(ss + NORM_EPS) * hg_ref[...]
    o_ref[:, :LANES] = y[:, :LANES].astype(o_ref.dtype)
    o_ref[:, LANES:] = _rope_apply(y[:, LANES:], cos_ref[...], sin_ref[...]).astype(o_ref.dtype)


def _mla_q(p, g, w, hg, cos, sin, *, tm=512):
    S = p.shape[0]
    tm = min(tm, S)
    K = MLA_Q_LORA
    cq_block = RWKV_SEC // K
    tab = pl.BlockSpec((tm, LANES), lambda i, j: (i, 0))
    return pl.pallas_call(
        _mla_q_kernel,
        out_shape=jax.ShapeDtypeStruct((S, MLA_HEADS * MLA_QK_PAD), BF16),
        grid=(S // tm, MLA_HEADS),
        in_specs=[pl.BlockSpec((tm, K), lambda i, j: (i, cq_block)),
                  pl.BlockSpec((1, K), lambda i, j: (0, 0)),
                  pl.BlockSpec((K, MLA_QK_PAD), lambda i, j: (0, j)),
                  pl.BlockSpec((1, MLA_QK_PAD), lambda i, j: (0, 0)),
                  tab, tab],
        out_specs=pl.BlockSpec((tm, MLA_QK_PAD), lambda i, j: (i, j)),
        scratch_shapes=[pltpu.VMEM((tm, K), BF16)],
        compiler_params=_cparams(("parallel", "arbitrary")),
        name="mla_q",
    )(p, g, w, hg, cos, sin)


def _mla_kv_kernel(x_ref, g_ref, w_ref, kr_ref, gn_ref, gr_ref, cos_ref, sin_ref, k_ref, v_ref, xn_ref):
    @pl.when(pl.program_id(1) == 0)
    def _():
        xn_ref[...] = _rms(x_ref[...], g_ref[...]).astype(BF16)

    acc = jnp.dot(xn_ref[...], w_ref[...], preferred_element_type=F32)
    kn = acc[:, :LANES]
    kr = kr_ref[...]
    ss = (jnp.sum(kn * kn, axis=-1, keepdims=True) + jnp.sum(kr * kr, axis=-1, keepdims=True)) * (1.0 / MLA_QK)
    rs = lax.rsqrt(ss + NORM_EPS)
    k_ref[:, :LANES] = (kn * rs * gn_ref[...]).astype(k_ref.dtype)
    k_ref[:, LANES:] = _rope_apply(kr * rs * gr_ref[...], cos_ref[...], sin_ref[...]).astype(k_ref.dtype)
    v_ref[...] = acc[:, LANES:].astype(v_ref.dtype)


def _mla_kv(p, g, w, gn, gr, cos, sin, *, tm=512):
    S = p.shape[0]
    tm = min(tm, S)
    K = MLA_KV_LORA
    ckv_block = (RWKV_SEC + MLA_Q_LORA) // K
    kr_block = (RWKV_SEC + MLA_Q_LORA + MLA_KV_LORA) // LANES
    tab = pl.BlockSpec((tm, LANES), lambda i, j: (i, 0))
    row = pl.BlockSpec((1, LANES), lambda i, j: (0, 0))
    return pl.pallas_call(
        _mla_kv_kernel,
        out_shape=(jax.ShapeDtypeStruct((S, MLA_HEADS * MLA_QK_PAD), BF16),
                   jax.ShapeDtypeStruct((S, MLA_HEADS * MLA_V), BF16)),
        grid=(S // tm, MLA_HEADS),
        in_specs=[pl.BlockSpec((tm, K), lambda i, j: (i, ckv_block)),
                  pl.BlockSpec((1, K), lambda i, j: (0, 0)),
                  pl.BlockSpec((K, MLA_QK_PAD), lambda i, j: (0, j)),
                  pl.BlockSpec((tm, LANES), lambda i, j: (i, kr_block)),
                  row, row, tab, tab],
        out_specs=(pl.BlockSpec((tm, MLA_QK_PAD), lambda i, j: (i, j)),
                   pl.BlockSpec((tm, MLA_V), lambda i, j: (i, j))),
        scratch_shapes=[pltpu.VMEM((tm, K), BF16)],
        compiler_params=_cparams(("parallel", "arbitrary")),
        name="mla_kv",
    )(p, g, w, p, gn, gr, cos, sin)


def _flash_rows(s, v, rows, m_sc, l_sc, acc_sc, weight=None):
    m_prev = m_sc[rows, :]
    m_new = jnp.maximum(m_prev, jnp.max(s, axis=-1, keepdims=True))
    alpha = jnp.exp(m_prev - m_new)
    pr = jnp.exp(s - pltpu.repeat(m_new, s.shape[1] // LANES, axis=1))
    if weight is not None:
        pr = pr * weight
    l_sc[rows, :] = alpha * l_sc[rows, :] + jnp.sum(pr, axis=-1, keepdims=True)
    acc_sc[rows, :] = alpha * acc_sc[rows, :] + jnp.dot(pr.astype(BF16), v, preferred_element_type=F32)
    m_sc[rows, :] = m_new


def _flash_init(m_sc, l_sc, acc_sc):
    m_sc[...] = jnp.full(m_sc.shape, -jnp.inf, F32)
    l_sc[...] = jnp.zeros(l_sc.shape, F32)
    acc_sc[...] = jnp.zeros(acc_sc.shape, F32)


def _mla_attn_kernel(qi_ref, ki_ref, q_ref, k_ref, v_ref, o_ref, m_sc, l_sc, acc_sc, *, sb):
    p = pl.program_id(1)
    qi = qi_ref[p]
    ki = ki_ref[p]
    t = q_ref.shape[0]

    @pl.when(ki == 0)
    def _():
        _flash_init(m_sc, l_sc, acc_sc)

    @pl.when(ki < qi)
    def _():
        for rb in range(t // sb):
            rows = slice(rb * sb, (rb + 1) * sb)
            s = _dot_nt(q_ref[rows, :], k_ref[...])
            _flash_rows(s, v_ref[...], rows, m_sc, l_sc, acc_sc)

    @pl.when(ki == qi)
    def _():
        for rb in range(t // sb):
            rows = slice(rb * sb, (rb + 1) * sb)
            nc = (rb + 1) * sb
            s = _dot_nt(q_ref[rows, :], k_ref[0:nc, :])
            row = lax.broadcasted_iota(jnp.int32, s.shape, 0) + rb * sb
            col = lax.broadcasted_iota(jnp.int32, s.shape, 1)
            s = jnp.where(col <= row, s, -jnp.inf)
            _flash_rows(s, v_ref[0:nc, :], rows, m_sc, l_sc, acc_sc)
        o_ref[...] = (acc_sc[...] / l_sc[...]).astype(o_ref.dtype)


def _mla_attention(q, k, v, *, t=1024, sb=256):
    S = q.shape[0]
    t = min(t, S)
    sb = min(sb, t)
    nb = S // t
    pairs = [(a, b) for a in range(nb) for b in range(a + 1)]
    qi_tab = jnp.asarray(np.array([a for a, _ in pairs], np.int32))
    ki_tab = jnp.asarray(np.array([b for _, b in pairs], np.int32))
    stat = pltpu.VMEM((t, LANES), F32)
    grid_spec = pltpu.PrefetchScalarGridSpec(
        num_scalar_prefetch=2,
        grid=(MLA_HEADS, len(pairs)),
        in_specs=[pl.BlockSpec((t, MLA_QK_PAD), lambda h, p, qt, kt: (qt[p], h)),
                  pl.BlockSpec((t, MLA_QK_PAD), lambda h, p, qt, kt: (kt[p], h)),
                  pl.BlockSpec((t, MLA_V), lambda h, p, qt, kt: (kt[p], h))],
        out_specs=pl.BlockSpec((t, MLA_V), lambda h, p, qt, kt: (qt[p], h)),
        scratch_shapes=[stat, stat, pltpu.VMEM((t, MLA_V), F32)],
    )
    return pl.pallas_call(
        functools.partial(_mla_attn_kernel, sb=sb),
        out_shape=jax.ShapeDtypeStruct((S, MLA_HEADS * MLA_V), BF16),
        grid_spec=grid_spec,
        compiler_params=_cparams(("parallel", "arbitrary")),
        name="mla_attention",
    )(qi_tab, ki_tab, q, k, v)


def _seg_sum(x, e_ref):
    return _split_dot(x, e_ref[...])


def _rwkv_prep_kernel(p_ref, mu_ref, w0_ref, w2_ref, a0_ref, a2_ref, g2_ref, kk_ref, ka_ref, e_ref,
                      r_o, lw_o, k_o, v_o, kkn_o, b_o, g_o, carry):
    i = pl.program_id(0)

    @pl.when(i == 0)
    def _():
        carry[...] = jnp.zeros(carry.shape, F32)

    x = p_ref[...]
    tm = x.shape[0]
    row = lax.broadcasted_iota(jnp.int32, (tm, 1), 0)
    prev = jnp.where(row == 0, carry[...], pltpu.roll(x, 1, 0))
    carry[...] = x[tm - 1:tm, :]
    xs = x + (prev - x) * mu_ref[...]
    W = RWKV_WIDTH
    r = xs[:, :W]
    kb = xs[:, W:2 * W]
    vb = xs[:, 2 * W:3 * W]
    xw = xs[:, 3 * W:3 * W + LANES]
    xa = xs[:, 3 * W + LANES:3 * W + 2 * LANES]
    xg = xs[:, 3 * W + 2 * LANES:]
    w_raw = -_softplus(-(w0_ref[...] + _dot(jnp.tanh(xw), w2_ref[...]))) - 0.5
    a = _sigmoid(a0_ref[...] + _dot(xa, a2_ref[...]))
    kk = kb * kk_ref[...]
    nrm = jnp.sqrt(_seg_sum(kk * kk, e_ref))
    kkn = kk / jnp.maximum(nrm, 1e-12)
    r_o[...] = r
    lw_o[...] = -jnp.exp(w_raw)
    k_o[...] = kb * (1.0 + (a - 1.0) * ka_ref[...])
    v_o[...] = vb
    kkn_o[...] = kkn
    b_o[...] = kkn * a
    g_o[...] = _dot(_sigmoid(xg), g2_ref[...])


def _rwkv_prep(p, mu, w0, w2, a0, a2, g2, k_k, k_a, e, *, tm=256):
    S = p.shape[0]
    tm = min(tm, S)
    W = RWKV_WIDTH
    row = pl.BlockSpec((1, W), lambda i: (0, 0))
    out = pl.BlockSpec((tm, W), lambda i: (i, 0))
    shp = jax.ShapeDtypeStruct((S, W), F32)
    return pl.pallas_call(
        _rwkv_prep_kernel,
        out_shape=(shp,) * 7,
        grid=(S // tm,),
        in_specs=[pl.BlockSpec((tm, RWKV_SEC), lambda i: (i, 0)),
                  pl.BlockSpec((1, RWKV_SEC), lambda i: (0, 0)),
                  row, pl.BlockSpec((LANES, W), lambda i: (0, 0)),
                  row, pl.BlockSpec((LANES, W), lambda i: (0, 0)),
                  pl.BlockSpec((2 * LANES, W), lambda i: (0, 0)),
                  row, row, pl.BlockSpec((W, W), lambda i: (0, 0))],
        out_specs=(out,) * 7,
        scratch_shapes=[pltpu.VMEM((1, RWKV_SEC), F32)],
        compiler_params=_cparams(("arbitrary",)),
        name="rwkv_prep",
    )(p, mu, w0, w2, a0, a2, g2, k_k, k_a, e)


def _bmm(a, b):
    return jnp.einsum('gik,gkj->gij', a.astype(BF16), b.astype(BF16), preferred_element_type=F32)


def _bmm_nt(a, b):
    return jnp.einsum('gik,gjk->gij', a.astype(BF16), b.astype(BF16), preferred_element_type=F32)


def _rwkv_kernel(r_ref, lw_ref, k_ref, v_ref, kk_ref, b_ref, g_ref, lnw_ref, lnb_ref, rk_ref, tri_ref, e_ref,
                 o_ref, st_sc, wm_sc, z_sc, y0_sc, r2_sc, mrb_sc, bh_sc, gkv_sc, pc_sc, y_sc, *, hg, G):
    C = RWKV_CHUNK
    LW = hg * RWKV_HEAD
    RW = hg * C
    c_idx = pl.program_id(1)

    @pl.when(c_idx == 0)
    def _():
        st_sc[...] = jnp.zeros(st_sc.shape, F32)

    lw = lw_ref[...]
    cum = _dot_split(tri_ref[...], lw)
    cum3 = cum.reshape(G, C, LW)
    lw3 = lw.reshape(G, C, LW)
    clast = cum3[:, C - 1:C, :]
    r3 = r_ref[...].reshape(G, C, LW)
    k3 = k_ref[...].reshape(G, C, LW)
    v3 = v_ref[...].reshape(G, C, LW)
    kk3 = kk_ref[...].reshape(G, C, LW)
    b3 = b_ref[...].reshape(G, C, LW)
    einv = jnp.exp(-cum3)
    etail = jnp.exp(clast - cum3)
    rt = r3 * jnp.exp(cum3)
    at = -kk3 * jnp.exp(cum3 - lw3)
    bt = b3 * einv
    kt = k3 * einv
    bh = b3 * etail
    kh = k3 * etail

    lane_head = lax.broadcasted_iota(jnp.int32, (1, 1, LW), 2) // RWKV_HEAD

    def stack(x):
        return jnp.concatenate([jnp.where(lane_head == h, x, 0.0) for h in range(hg)], axis=1)

    a2, b2, k2, r2, v2, bh2, kh2 = (stack(t) for t in (at, bt, kt, rt, v3, bh, kh))
    ri = lax.broadcasted_iota(jnp.int32, (1, RW, RW), 1)
    ci = lax.broadcasted_iota(jnp.int32, (1, RW, RW), 2)
    same = (ri // C) == (ci // C)
    strict = same & (ci < ri)
    incl = same & (ci <= ri)
    eye = (ri == ci).astype(F32)

    lmat = jnp.where(strict, _bmm_nt(a2, b2), 0.0)
    akm = jnp.where(strict, _bmm_nt(a2, k2), 0.0)
    rbm = jnp.where(incl, _bmm_nt(r2, b2), 0.0)
    rkm = jnp.where(incl, _bmm_nt(r2, k2), 0.0)
    tinv = eye + lmat
    pw = lmat
    for _ in range(int(np.log2(C)) - 1):
        pw = _bmm(pw, pw)
        tinv = tinv + _bmm(tinv, pw)
    wm_sc[...] = _bmm(tinv, a2)
    z_sc[...] = _bmm(tinv, _bmm(akm, v2))
    y0_sc[...] = _bmm(rkm, v2)
    r2_sc[...] = r2
    mrb_sc[...] = rbm
    bh_sc[...] = bh2
    gkv_sc[...] = _bmm(jnp.swapaxes(v2, 1, 2), kh2)
    pc_sc[...] = jnp.exp(clast)

    def chunk(c, carry):
        st = st_sc[...]
        u2 = _dot_nt(wm_sc[c], st) + z_sc[c]
        y2 = _dot_nt(r2_sc[c], st) + _dot(mrb_sc[c], u2) + y0_sc[c]
        st_sc[...] = st * pc_sc[c] + _dot(u2.T, bh_sc[c]) + gkv_sc[c]
        y = y2[0:C]
        for h in range(1, hg):
            y = y + y2[h * C:(h + 1) * C]
        y_sc[c] = y
        return carry

    lax.fori_loop(0, G, chunk, 0)

    y = y_sc[...].reshape(G * C, LW)
    inv_n = 1.0 / RWKV_HEAD
    mu = _seg_sum(y, e_ref) * inv_n
    d = y - mu
    var = _seg_sum(d * d, e_ref) * inv_n
    yn = d * lax.rsqrt(var + RWKV_GN_EPS) * lnw_ref[...] + lnb_ref[...]
    r = r_ref[...]
    bonus = _seg_sum(r * k_ref[...] * rk_ref[...], e_ref) * v_ref[...]
    o_ref[...] = ((yn + bonus) * g_ref[...]).astype(o_ref.dtype)


def _rwkv_mix(r, lw, k, v, kkn, b, g, ln_w, ln_b, r_k, *, hg=2, G=8):
    S = r.shape[0]
    C = RWKV_CHUNK
    G = min(G, S // C)
    TB = G * C
    LW = hg * RWKV_HEAD
    RW = hg * C
    ngrp = RWKV_WIDTH // LW
    t = np.arange(TB)
    tri = jnp.asarray(((t[:, None] // C == t[None, :] // C) & (t[None, :] <= t[:, None])).astype(np.float32), BF16)
    l = np.arange(LW)
    e = jnp.asarray((l[:, None] // RWKV_HEAD == l[None, :] // RWKV_HEAD).astype(np.float32), BF16)
    blk = pl.BlockSpec((TB, LW), lambda gi, c: (c, gi))
    row = pl.BlockSpec((1, LW), lambda gi, c: (0, gi))
    mat = lambda: pltpu.VMEM((G, RW, LW), F32)
    return pl.pallas_call(
        functools.partial(_rwkv_kernel, hg=hg, G=G),
        out_shape=jax.ShapeDtypeStruct((S, RWKV_WIDTH), BF16),
        grid=(ngrp, S // TB),
        in_specs=[blk] * 7 + [row] * 3 + [pl.BlockSpec((TB, TB), lambda gi, c: (0, 0)),
                                          pl.BlockSpec((LW, LW), lambda gi, c: (0, 0))],
        out_specs=blk,
        scratch_shapes=[pltpu.VMEM((LW, LW), F32), mat(), mat(), mat(), mat(),
                        pltpu.VMEM((G, RW, RW), F32), mat(), pltpu.VMEM((G, LW, LW), F32),
                        pltpu.VMEM((G, 1, LW), F32), pltpu.VMEM((G, C, LW), F32)],
        compiler_params=_cparams(("parallel", "arbitrary")),
        name="rwkv_mix",
    )(r, lw, k, v, kkn, b, g, ln_w, ln_b, r_k, tri, e)


def _outproj_kernel(h_ref, a_ref, b_ref, wa_ref, wb_ref, o_ref):
    o_ref[...] = (h_ref[...] + jnp.dot(a_ref[...], wa_ref[...], preferred_element_type=F32)
                  + jnp.dot(b_ref[...], wb_ref[...], preferred_element_type=F32))


def _outproj(h, a, b, wa, wb, *, tm=512, tn=512):
    S, D = h.shape
    tm = min(tm, S)
    Ka, Kb = a.shape[1], b.shape[1]
    return pl.pallas_call(
        _outproj_kernel,
        out_shape=jax.ShapeDtypeStruct((S, D), F32),
        grid=(S // tm, D // tn),
        in_specs=[pl.BlockSpec((tm, tn), lambda i, j: (i, j)),
                  pl.BlockSpec((tm, Ka), lambda i, j: (i, 0)),
                  pl.BlockSpec((tm, Kb), lambda i, j: (i, 0)),
                  pl.BlockSpec((Ka, tn), lambda i, j: (0, j)),
                  pl.BlockSpec((Kb, tn), lambda i, j: (0, j))],
        out_specs=pl.BlockSpec((tm, tn), lambda i, j: (i, j)),
        compiler_params=_cparams(("parallel", "arbitrary")),
        name="outproj",
    )(h, a, b, wa, wb)


def _pad_cols(w, width):
    return jnp.pad(w, ((0, 0), (0, width - w.shape[1])))


def _rope_gap_layout(w):
    z = jnp.zeros(w.shape[:-1] + (32,), w.dtype)
    return jnp.concatenate([w[..., :32], z, w[..., 32:], z], axis=-1)


def _even_mixer(h, tabs, mix_norm, w_in, q_norm, w_uq, kv_norm, w_ukv, q_head_norm, k_head_norm,
                mu, w0, w2, a0, a2, g2, k_k, k_a, r_k, ln_w, ln_b, w_out):
    cos1, sin1 = tabs[0], tabs[1]
    W = RWKV_WIDTH
    o_cq = 0
    o_ckv = MLA_Q_LORA
    o_kr = o_ckv + MLA_KV_LORA
    o_rw = o_kr + MLA_ROPE
    rw = w_in[:, o_rw:]
    mu_r = mu[None, :]

    def rwkv_layout(t):
        return jnp.concatenate([t[:, :3 * W], _pad_cols(t[:, 3 * W:3 * W + RWKV_W_LORA], LANES),
                                _pad_cols(t[:, 3 * W + RWKV_W_LORA:3 * W + RWKV_W_LORA + RWKV_A_LORA], LANES),
                                _pad_cols(t[:, 3 * W + RWKV_W_LORA + RWKV_A_LORA:], 2 * LANES)], axis=1)

    w_in_p = jnp.concatenate([rwkv_layout(rw), w_in[:, o_cq:o_ckv], w_in[:, o_ckv:o_kr],
                              _rope_gap_layout(w_in[:, o_kr:o_rw])], axis=1)
    w_in_p = _pad_cols(w_in_p, 4608).astype(BF16)
    p = _norm_matmul(h, mix_norm[None, :], w_in_p)

    wq = w_uq.reshape(MLA_Q_LORA, MLA_HEADS, MLA_QK)
    wq = jnp.concatenate([wq[..., :MLA_NOPE], _rope_gap_layout(wq[..., MLA_NOPE:])], axis=-1)
    wq = wq.reshape(MLA_Q_LORA, MLA_HEADS * MLA_QK_PAD).astype(BF16)
    scale = MLA_QK ** -0.5
    hg_q = jnp.concatenate([q_head_norm[:MLA_NOPE], _rope_gap_layout(q_head_norm[MLA_NOPE:])])[None, :] * scale
    q = _mla_q(p, q_norm[None, :], wq, hg_q, cos1, sin1)
    gn = k_head_norm[None, :MLA_NOPE]
    gr = _rope_gap_layout(k_head_norm[MLA_NOPE:])[None, :]
    kmat, vmat = _mla_kv(p, kv_norm[None, :], w_ukv.astype(BF16), gn, gr, cos1, sin1)
    o_a = _mla_attention(q, kmat, vmat)

    l = np.arange(W)
    e = jnp.asarray((l[:, None] // RWKV_HEAD == l[None, :] // RWKV_HEAD).astype(np.float32), BF16)
    w2p = jnp.pad(w2, ((0, LANES - RWKV_W_LORA), (0, 0))).astype(BF16)
    a2p = jnp.pad(a2, ((0, LANES - RWKV_A_LORA), (0, 0))).astype(BF16)
    g2p = jnp.pad(g2, ((0, 2 * LANES - RWKV_G_LORA), (0, 0))).astype(BF16)
    r, lw, k, v, kkn, b, g = _rwkv_prep(p, rwkv_layout(mu_r), w0[None, :], w2p, a0[None, :], a2p, g2p,
                                         k_k[None, :], k_a[None, :], e)
    o_b = _rwkv_mix(r, lw, k, v, kkn, b, g, ln_w[None, :], ln_b[None, :], r_k.reshape(1, W))
    wo = w_out.astype(BF16)
    return _outproj(h, o_a, o_b, wo[:MLA_HEADS * MLA_V], wo[MLA_HEADS * MLA_V:])


def _swa_qkv_kernel(x_ref, g_ref, w_ref, hg_ref, cos_ref, sin_ref, o_ref, xn_ref, *, n_normed):
    j = pl.program_id(1)

    @pl.when(j == 0)
    def _():
        xn_ref[...] = _rms(x_ref[...], g_ref[...]).astype(BF16)

    acc = jnp.dot(xn_ref[...], w_ref[...], preferred_element_type=F32)

    @pl.when(j < n_normed)
    def _():
        for c in range(acc.shape[1] // LANES):
            sl = slice(c * LANES, (c + 1) * LANES)
            y = _rms(acc[:, sl], hg_ref[:, sl])
            o_ref[:, sl] = _rope_apply(y, cos_ref[...], sin_ref[...]).astype(o_ref.dtype)

    @pl.when(j >= n_normed)
    def _():
        o_ref[...] = acc.astype(o_ref.dtype)


def _swa_qkv(h, g, w, hg, cos, sin, *, tm=512, tn=512):
    S, K = h.shape
    N = w.shape[1]
    tm = min(tm, S)
    n_normed = (2 * SWA_HEADS * SWA_DIM) // tn
    tab = pl.BlockSpec((tm, LANES), lambda i, j: (i, 0))
    return pl.pallas_call(
        functools.partial(_swa_qkv_kernel, n_normed=n_normed),
        out_shape=jax.ShapeDtypeStruct((S, N), BF16),
        grid=(S // tm, N // tn),
        in_specs=[pl.BlockSpec((tm, K), lambda i, j: (i, 0)),
                  pl.BlockSpec((1, K), lambda i, j: (0, 0)),
                  pl.BlockSpec((K, tn), lambda i, j: (0, j)),
                  pl.BlockSpec((1, tn), lambda i, j: (0, jnp.minimum(j, n_normed - 1))),
                  tab, tab],
        out_specs=pl.BlockSpec((tm, tn), lambda i, j: (i, j)),
        scratch_shapes=[pltpu.VMEM((tm, K), BF16)],
        compiler_params=_cparams(("parallel", "arbitrary")),
        name="swa_qkv",
    )(h, g, w, hg, cos, sin)


def _dilated_bias(t, nrel):
    row = np.arange(t)[:, None]
    col = np.arange(t)[None, :]
    out = np.empty((nrel, t, t), np.float32)
    for r in range(nrel):
        delta = r * t + row - col
        cnt = np.zeros((t, t), np.float64)
        for window, dilation in SWA_PATTERNS:
            cnt += (delta >= 0) & (delta <= window) & (delta % dilation == 0)
        with np.errstate(divide="ignore"):
            out[r] = np.log(cnt)
    return out


def _dilated_kernel(q_ref, k_ref, v_ref, bias_ref, o_ref, m_sc, l_sc, acc_sc, *, sb, nrel, live):
    qi = pl.program_id(1)
    r = pl.program_id(2)
    t = q_ref.shape[0]

    @pl.when(r == 0)
    def _():
        _flash_init(m_sc, l_sc, acc_sc)

    for rr in range(nrel):
        @pl.when((r == rr) & (qi - r >= 0))
        def _(rr=rr):
            for rb in range(t // sb):
                rows = slice(rb * sb, (rb + 1) * sb)
                lo, hi = live[rr][rb]
                s = _dot_nt(q_ref[rows, :], k_ref[lo:hi, :]) + bias_ref[rr, rows, lo:hi]
                _flash_rows(s, v_ref[lo:hi, :], rows, m_sc, l_sc, acc_sc)

    @pl.when(r == nrel - 1)
    def _():
        o_ref[...] = (acc_sc[...] / l_sc[...]).astype(o_ref.dtype)


def _dilated_attention(qkv, *, t=512, sb=256):
    S = qkv.shape[0]
    t = min(t, S)
    sb = min(sb, t)
    max_window = max(w for w, _ in SWA_PATTERNS)
    nrel = min(-(-max_window // t) + 1, S // t)
    H = SWA_HEADS
    bias = _dilated_bias(t, nrel)
    live = []
    for r in range(nrel):
        per_rb = []
        for rb in range(t // sb):
            cols = np.nonzero(np.isfinite(bias[r, rb * sb:(rb + 1) * sb]).any(axis=0))[0]
            lo = int(cols.min()) // LANES * LANES
            hi = -(-(int(cols.max()) + 1) // LANES) * LANES
            per_rb.append((lo, hi))
        live.append(per_rb)
    stat = pltpu.VMEM((t, LANES), F32)
    return pl.pallas_call(
        functools.partial(_dilated_kernel, sb=sb, nrel=nrel, live=live),
        out_shape=jax.ShapeDtypeStruct((S, H * SWA_DIM), BF16),
        grid=(H, S // t, nrel),
        in_specs=[pl.BlockSpec((t, SWA_DIM), lambda h, i, r: (i, h)),
                  pl.BlockSpec((t, SWA_DIM), lambda h, i, r: (jnp.maximum(i - r, 0), H + h)),
                  pl.BlockSpec((t, SWA_DIM), lambda h, i, r: (jnp.maximum(i - r, 0), 2 * H + h)),
                  pl.BlockSpec((nrel, t, t), lambda h, i, r: (0, 0, 0))],
        out_specs=pl.BlockSpec((t, SWA_DIM), lambda h, i, r: (i, h)),
        scratch_shapes=[stat, stat, pltpu.VMEM((t, SWA_DIM), F32)],
        compiler_params=_cparams(("parallel", "parallel", "arbitrary")),
        name="dilated_attention",
    )(qkv, qkv, qkv, jnp.asarray(bias))


def _gla_kernel(q_ref, k_ref, v_ref, glr_ref, wgu_ref, bg_ref, rd_ref, gn_ref, tri_ref, o_ref,
                st_sc, qt_sc, gkv_sc, eb_sc, oi_sc, *, G):
    C = GLA_CHUNK
    SB = GLA_SUB
    DK = GLA_DK
    DV = GLA_DV
    c_idx = pl.program_id(1)

    @pl.when(c_idx == 0)
    def _():
        st_sc[...] = jnp.zeros(st_sc.shape, F32)

    z = _dot(glr_ref[...], wgu_ref[...]) + bg_ref[...]
    lg = -_softplus(-z) * (1.0 / GLA_NORMALIZER)
    bc = _dot_split(tri_ref[...], lg).reshape(G, C, DK)
    q = (q_ref[...] * (DK ** -0.5)).reshape(G, C, DK)
    k = k_ref[...].reshape(G, C, DK)
    v = v_ref[...].reshape(G, C, DV)
    blast = bc[:, C - 1:C, :]
    qt_sc[...] = q * jnp.exp(bc)
    khat = k * jnp.exp(blast - bc)
    gkv_sc[...] = _bmm(jnp.swapaxes(v, 1, 2), khat)
    eb_sc[...] = jnp.exp(blast)

    ti = lax.broadcasted_iota(jnp.int32, (1, SB, SB, 1), 1)
    si = lax.broadcasted_iota(jnp.int32, (1, SB, SB, 1), 2)
    causal = si <= ti
    for sb in range(C // SB):
        lo = sb * SB
        qs = q[:, lo:lo + SB]
        ks = k[:, lo:lo + SB]
        bs = bc[:, lo:lo + SB]
        rel = bs[:, :, None, :] - bs[:, None, :, :]
        dec = jnp.exp(jnp.where(causal, rel, -jnp.inf))
        att = jnp.sum(qs[:, :, None, :] * ks[:, None, :, :] * dec, axis=-1)
        o_sb = _bmm(att, v[:, lo:lo + SB])
        if sb > 0:
            bm = bc[:, lo - 1:lo]
            qsc = qs * jnp.exp(bs - bm)
            ksc = k[:, :lo] * jnp.exp(bm - bc[:, :lo])
            o_sb = o_sb + _bmm(_bmm_nt(qsc, ksc), v[:, :lo])
        oi_sc[:, lo:lo + SB, :] = o_sb

    def chunk(c, carry):
        st = st_sc[...]
        oi_sc[c] = oi_sc[c] + _dot_nt(qt_sc[c], st)
        st_sc[...] = st * eb_sc[c] + gkv_sc[c]
        return carry

    lax.fori_loop(0, G, chunk, 0)

    o = oi_sc[...].reshape(G * C, DV)
    rd = rd_ref[...]
    o_ref[...] = (_rms(o, gn_ref[...]) * (rd * _sigmoid(rd))).astype(o_ref.dtype)


def _gla(pb, wgu, bg, gn, *, G=8):
    S = pb.shape[0]
    C = GLA_CHUNK
    G = min(G, S // C)
    TB = G * C
    H, DK, DV = GLA_HEADS, GLA_DK, GLA_DV
    t = np.arange(TB)
    tri = jnp.asarray(((t[:, None] // C == t[None, :] // C) & (t[None, :] <= t[:, None])).astype(np.float32), BF16)
    glr_block = (2 * H * DK + 2 * H * DV) // LANES
    return pl.pallas_call(
        functools.partial(_gla_kernel, G=G),
        out_shape=jax.ShapeDtypeStruct((S, H * DV), BF16),
        grid=(H, S // TB),
        in_specs=[pl.BlockSpec((TB, DK), lambda h, c: (c, h)),
                  pl.BlockSpec((TB, DK), lambda h, c: (c, H + h)),
                  pl.BlockSpec((TB, DV), lambda h, c: (c, (2 * H * DK) // DV + h)),
                  pl.BlockSpec((TB, LANES), lambda h, c: (c, glr_block)),
                  pl.BlockSpec((LANES, DK), lambda h, c: (0, h)),
                  pl.BlockSpec((1, DK), lambda h, c: (0, h)),
                  pl.BlockSpec((TB, DV), lambda h, c: (c, (2 * H * DK + H * DV) // DV + h)),
                  pl.BlockSpec((1, DV), lambda h, c: (0, h)),
                  pl.BlockSpec((TB, TB), lambda h, c: (0, 0))],
        out_specs=pl.BlockSpec((TB, DV), lambda h, c: (c, h)),
        scratch_shapes=[pltpu.VMEM((DV, DK), F32), pltpu.VMEM((G, C, DK), F32), pltpu.VMEM((G, DV, DK), F32),
                        pltpu.VMEM((G, 1, DK), F32), pltpu.VMEM((G, C, DV), F32)],
        compiler_params=_cparams(("parallel", "arbitrary")),
        name="gla",
    )(pb, pb, pb, pb, wgu, bg, pb, gn, tri)


def _odd_mixer(h, tabs, mix_norm, w_in, q_head_norm, k_head_norm, w_gate_up, b_gate, gla_norm, w_out):
    cos2, sin2 = tabs[2], tabs[3]
    nq = SWA_HEADS * SWA_DIM
    g_row = mix_norm[None, :]
    hg = jnp.concatenate([jnp.tile(q_head_norm * SWA_DIM ** -0.5, SWA_HEADS), jnp.tile(k_head_norm, SWA_HEADS)])[None, :]
    qkv = _swa_qkv(h, g_row, w_in[:, :3 * nq].astype(BF16), hg, cos2, sin2)
    o_c = _dilated_attention(qkv)
    o = 3 * nq
    dk, dv = GLA_HEADS * GLA_DK, GLA_HEADS * GLA_DV
    wb = jnp.concatenate([w_in[:, o:o + 2 * dk + dv], w_in[:, o + 2 * dk + dv + GLA_LORA:],
                          _pad_cols(w_in[:, o + 2 * dk + dv:o + 2 * dk + dv + GLA_LORA], LANES)], axis=1)
    wb = _pad_cols(wb, 3584).astype(BF16)
    pb = _norm_matmul(h, g_row, wb)
    wgu = jnp.pad(w_gate_up, ((0, LANES - GLA_LORA), (0, 0))).astype(BF16)
    o_d = _gla(pb, wgu, b_gate[None, :], gla_norm.reshape(1, dv))
    wo = w_out.astype(BF16)
    return _outproj(h, o_c, o_d, wo[:nq], wo[nq:])


def kernel(x, positions, ffn_norm, ffn_w_gate, ffn_w_up, ffn_w_down, mix_norm, mla_rwkv_w_in, mla_q_norm, mla_w_uq,
           mla_kv_norm, mla_w_ukv, mla_q_head_norm, mla_k_head_norm, rwkv_mu, rwkv_w0, rwkv_w2, rwkv_a0, rwkv_a2,
           rwkv_g2, rwkv_k_k, rwkv_k_a, rwkv_r_k, rwkv_ln_w, rwkv_ln_b, mla_rwkv_w_out, swa_gla_w_in,
           swa_q_head_norm, swa_k_head_norm, gla_w_gate_up, gla_b_gate, gla_norm, swa_gla_w_out):
    B, S, D = x.shape
    assert B == 1
    h = x.reshape(S, D)
    tabs = _rope_tables(positions.reshape(S))
    wg = ffn_w_gate.astype(BF16)
    wu = ffn_w_up.astype(BF16)
    wd = ffn_w_down.astype(BF16)
    depth = ffn_norm.shape[0]
    for layer in range(depth):
        i = layer // 2
        h = _ffn(h, ffn_norm, wg, wu, wd, layer, 0)
        if layer % 2 == 0:
            h = _even_mixer(h, tabs, mix_norm[layer], mla_rwkv_w_in[i], mla_q_norm[i], mla_w_uq[i], mla_kv_norm[i],
                            mla_w_ukv[i], mla_q_head_norm[i], mla_k_head_norm[i], rwkv_mu[i], rwkv_w0[i],
                            rwkv_w2[i], rwkv_a0[i], rwkv_a2[i], rwkv_g2[i], rwkv_k_k[i], rwkv_k_a[i], rwkv_r_k[i],
                            rwkv_ln_w[i], rwkv_ln_b[i], mla_rwkv_w_out[i])
        else:
            h = _odd_mixer(h, tabs, mix_norm[layer], swa_gla_w_in[i], swa_q_head_norm[i], swa_k_head_norm[i],
                           gla_w_gate_up[i], gla_b_gate[i], gla_norm[i], swa_gla_w_out[i])
        h = _ffn(h, ffn_norm, wg, wu, wd, layer, 1)
    return h.reshape(B, S, D)
```

```python
import functools

import numpy as np
import jax
import jax.numpy as jnp
from jax import lax
from jax.experimental import pallas as pl
from jax.experimental.pallas import tpu as pltpu

F32 = jnp.float32
BF16 = jnp.bfloat16

LANES = 128
VMEM_LIMIT = 56 * 1024 * 1024

D_MODEL = 2048
D_FF = 5632
MACARON_WEIGHT = 0.5
NORM_EPS = 1e-6
ROPE_THETA = 10000.0
LOG2E = float(np.log2(np.e))

MLA_HEADS = 8
MLA_NOPE = 128
MLA_ROPE = 64
MLA_QK = MLA_NOPE + MLA_ROPE
MLA_QK_PAD = 256
MLA_V = 128
MLA_Q_LORA = 512
MLA_KV_LORA = 256

RWKV_HEAD = 64
RWKV_WIDTH = 1024
RWKV_HEADS = 16
RWKV_W_LORA = 64
RWKV_A_LORA = 64
RWKV_G_LORA = 160
RWKV_GN_EPS = 64e-5
RWKV_SEC = 3584
RWKV_CHUNK = 64

SWA_HEADS = 8
SWA_DIM = 128
SWA_PATTERNS = ((128, 1), (512, 4), (2048, 16))

GLA_HEADS = 4
GLA_DK = 128
GLA_DV = 256
GLA_LORA = 16
GLA_NORMALIZER = 16.0
GLA_CHUNK = 64
GLA_SUB = 16


def _cparams(sem):
    return pltpu.CompilerParams(dimension_semantics=sem, vmem_limit_bytes=VMEM_LIMIT)


def _rms(x, g):
    return x * lax.rsqrt(jnp.mean(x * x, axis=-1, keepdims=True) + NORM_EPS) * g


def _dot(a, b):
    return jnp.dot(a.astype(BF16), b.astype(BF16), preferred_element_type=F32)


def _dot_nt(a, b):
    return lax.dot_general(a.astype(BF16), b.astype(BF16), (((1,), (1,)), ((), ())),
                           preferred_element_type=F32)


def _dot_split(a, b):
    b0 = b.astype(BF16)
    r1 = b - b0.astype(F32)
    b1 = r1.astype(BF16)
    b2 = (r1 - b1.astype(F32)).astype(BF16)
    a = a.astype(BF16)
    return (jnp.dot(a, b0, preferred_element_type=F32) + jnp.dot(a, b1, preferred_element_type=F32)
            + jnp.dot(a, b2, preferred_element_type=F32))


def _split_dot(a, b):
    a0 = a.astype(BF16)
    r1 = a - a0.astype(F32)
    a1 = r1.astype(BF16)
    a2 = (r1 - a1.astype(F32)).astype(BF16)
    b = b.astype(BF16)
    return (jnp.dot(a0, b, preferred_element_type=F32) + jnp.dot(a1, b, preferred_element_type=F32)
            + jnp.dot(a2, b, preferred_element_type=F32))


def _sigmoid(x):
    return 1.0 / (1.0 + jnp.exp(-x))


def _softplus(x):
    return jnp.maximum(x, 0.0) + jnp.log(1.0 + jnp.exp(-jnp.abs(x)))


def _rope_apply(x, cos, sin_signed):
    return x * cos + pltpu.roll(x, 64, 1) * sin_signed


def _ffn_kernel(h_ref, g_ref, wg_ref, wu_ref, wd_ref, *rest, emit_norm):
    if emit_norm:
        g2_ref, o_ref, hn_ref, xn_ref = rest
    else:
        o_ref, xn_ref = rest
    j = pl.program_id(1)

    @pl.when(j == 0)
    def _():
        x = h_ref[...]
        xn_ref[...] = _rms(x, g_ref[...]).astype(BF16)
        o_ref[...] = x

    xn = xn_ref[...]
    gate = jnp.dot(xn, wg_ref[...], preferred_element_type=F32)
    up = jnp.dot(xn, wu_ref[...], preferred_element_type=F32)
    act = (gate * _sigmoid(gate) * up * MACARON_WEIGHT).astype(BF16)
    o_ref[...] += jnp.dot(act, wd_ref[...], preferred_element_type=F32)

    if emit_norm:
        @pl.when(j == pl.num_programs(1) - 1)
        def _():
            hn_ref[...] = _rms(o_ref[...], g2_ref[...]).astype(hn_ref.dtype)


def _ffn(h, g, wg, wu, wd, layer, k, next_norm=None, *, tm=512, tf=512):
    S, D = h.shape
    F = wg.shape[-1]
    tm = min(tm, S)
    emit_norm = next_norm is not None
    wspec = pl.BlockSpec((None, None, D, tf), lambda i, j: (layer, k, 0, j))
    row = pl.BlockSpec((1, D), lambda i, j: (0, 0))
    tile = pl.BlockSpec((tm, D), lambda i, j: (i, 0))
    in_specs = [tile, row, wspec, wspec, pl.BlockSpec((None, None, tf, D), lambda i, j: (layer, k, j, 0))]
    args = [h, g[layer, k][None, :], wg, wu, wd]
    out_shape = jax.ShapeDtypeStruct((S, D), F32)
    out_specs = tile
    if emit_norm:
        in_specs.append(row)
        args.append(next_norm[None, :])
        out_shape = (out_shape, jax.ShapeDtypeStruct((S, D), BF16))
        out_specs = (tile, tile)
    return pl.pallas_call(
        functools.partial(_ffn_kernel, emit_norm=emit_norm),
        out_shape=out_shape,
        grid=(S // tm, F // tf),
        in_specs=in_specs,
        out_specs=out_specs,
        scratch_shapes=[pltpu.VMEM((tm, D), BF16)],
        compiler_params=_cparams(("parallel", "arbitrary")),
        name="ffn",
    )(*args)


def _rope_tab_kernel(pos_ref, f1_ref, s1_ref, f2_ref, s2_ref, c1_ref, n1_ref, c2_ref, n2_ref):
    pos = pos_ref[...].astype(F32)
    a1 = pos * f1_ref[...]
    c1_ref[...] = jnp.cos(a1)
    n1_ref[...] = jnp.sin(a1) * s1_ref[...]
    a2 = pos * f2_ref[...]
    c2_ref[...] = jnp.cos(a2)
    n2_ref[...] = jnp.sin(a2) * s2_ref[...]


def _rope_tables(positions, *, tm=1024):
    S = positions.shape[0]
    tm = min(tm, S)
    f32half = ROPE_THETA ** (-jnp.arange(MLA_ROPE // 2, dtype=F32) / (MLA_ROPE // 2))
    z = jnp.zeros((32,), F32)
    f1 = jnp.concatenate([f32half, z, f32half, z])[None]
    f64half = ROPE_THETA ** (-jnp.arange(SWA_DIM // 2, dtype=F32) / (SWA_DIM // 2))
    f2 = jnp.concatenate([f64half, f64half])[None]
    sign = jnp.concatenate([-jnp.ones((64,), F32), jnp.ones((64,), F32)])[None]
    row = pl.BlockSpec((1, LANES), lambda i: (0, 0))
    tab = pl.BlockSpec((tm, LANES), lambda i: (i, 0))
    shp = jax.ShapeDtypeStruct((S, LANES), F32)
    return pl.pallas_call(
        _rope_tab_kernel,
        out_shape=(shp, shp, shp, shp),
        grid=(S // tm,),
        in_specs=[pl.BlockSpec((tm, 1), lambda i: (i, 0)), row, row, row, row],
        out_specs=(tab, tab, tab, tab),
        compiler_params=_cparams(("parallel",)),
        name="rope_tables",
    )(positions.reshape(S, 1), f1, sign, f2, sign)


def _proj_kernel(x_ref, w_ref, o_ref):
    o_ref[...] = jnp.dot(x_ref[...], w_ref[...], preferred_element_type=F32).astype(o_ref.dtype)


def _proj(x, w, *, tm=512, tn, out_dtype=F32):
    S, K = x.shape
    N = w.shape[1]
    tm = min(tm, S)
    return pl.pallas_call(
        _proj_kernel,
        out_shape=jax.ShapeDtypeStruct((S, N), out_dtype),
        grid=(N // tn, S // tm),
        in_specs=[pl.BlockSpec((tm, K), lambda n, i: (i, 0)),
                  pl.BlockSpec((K, tn), lambda n, i: (0, n))],
        out_specs=pl.BlockSpec((tm, tn), lambda n, i: (i, n)),
        compiler_params=_cparams(("parallel", "parallel")),
        name="proj",
    )(x, w)


def _mla_qkv_kernel(cq_ref, ckv_ref, kr_ref, qn_ref, kvn_ref, wq_ref, wkv_ref, hgq_ref, gn_ref, gr_ref,
                    cos_ref, sin_ref, q_ref, k_ref, v_ref):
    cos = cos_ref[...]
    sin = sin_ref[...]
    accq = _dot(_rms(cq_ref[...], qn_ref[...]), wq_ref[...])
    acckv = _dot(_rms(ckv_ref[...], kvn_ref[...]), wkv_ref[...])
    kr = kr_ref[...]
    kr_ss = jnp.sum(kr * kr, axis=-1, keepdims=True)
    inv_d = 1.0 / MLA_QK
    for h in range(MLA_HEADS):
        lo = h * MLA_QK_PAD
        a = accq[:, lo:lo + MLA_QK_PAD]
        y = a * lax.rsqrt(jnp.sum(a * a, axis=-1, keepdims=True) * inv_d + NORM_EPS) * hgq_ref[...]
        q_ref[:, lo:lo + LANES] = y[:, :LANES].astype(q_ref.dtype)
        q_ref[:, lo + LANES:lo + MLA_QK_PAD] = _rope_apply(y[:, LANES:], cos, sin).astype(q_ref.dtype)
        kn = acckv[:, lo:lo + LANES]
        rs = lax.rsqrt((jnp.sum(kn * kn, axis=-1, keepdims=True) + kr_ss) * inv_d + NORM_EPS)
        k_ref[:, lo:lo + LANES] = (kn * rs * gn_ref[...]).astype(k_ref.dtype)
        k_ref[:, lo + LANES:lo + MLA_QK_PAD] = _rope_apply(kr * rs * gr_ref[...], cos, sin).astype(k_ref.dtype)
        v_ref[:, h * MLA_V:(h + 1) * MLA_V] = acckv[:, lo + LANES:lo + MLA_QK_PAD].astype(v_ref.dtype)


def _mla_qkv(p, qn, kvn, wq, wkv, hgq, gn, gr, cos, sin, *, tm=256):
    S = p.shape[0]
    tm = min(tm, S)
    cq_block = RWKV_SEC // MLA_Q_LORA
    ckv_block = (RWKV_SEC + MLA_Q_LORA) // MLA_KV_LORA
    kr_block = (RWKV_SEC + MLA_Q_LORA + MLA_KV_LORA) // LANES
    NQ = MLA_HEADS * MLA_QK_PAD
    tab = pl.BlockSpec((tm, LANES), lambda i: (i, 0))
    full = lambda a: pl.BlockSpec(a.shape, lambda i: (0, 0))
    return pl.pallas_call(
        _mla_qkv_kernel,
        out_shape=(jax.ShapeDtypeStruct((S, NQ), BF16), jax.ShapeDtypeStruct((S, NQ), BF16),
                   jax.ShapeDtypeStruct((S, MLA_HEADS * MLA_V), BF16)),
        grid=(S // tm,),
        in_specs=[pl.BlockSpec((tm, MLA_Q_LORA), lambda i: (i, cq_block)),
                  pl.BlockSpec((tm, MLA_KV_LORA), lambda i: (i, ckv_block)),
                  pl.BlockSpec((tm, LANES), lambda i: (i, kr_block)),
                  full(qn), full(kvn), full(wq), full(wkv), full(hgq), full(gn), full(gr), tab, tab],
        out_specs=(pl.BlockSpec((tm, NQ), lambda i: (i, 0)), pl.BlockSpec((tm, NQ), lambda i: (i, 0)),
                   pl.BlockSpec((tm, MLA_HEADS * MLA_V), lambda i: (i, 0))),
        compiler_params=_cparams(("parallel",)),
        name="mla_qkv",
    )(p, p, p, qn, kvn, wq, wkv, hgq, gn, gr, cos, sin)


def _flash_rows(s, v, rows, m_sc, l_sc, acc_sc, weight=None):
    m_prev = m_sc[rows, :]
    m_new = jnp.maximum(m_prev, jnp.max(s, axis=-1, keepdims=True))
    alpha = jnp.exp2(m_prev - m_new)
    pr = jnp.exp2(s - jnp.concatenate([m_new] * (s.shape[1] // LANES), axis=1))
    if weight is not None:
        pr = pr * weight
    l_sc[rows, :] = alpha * l_sc[rows, :] + jnp.sum(pr, axis=-1, keepdims=True)
    acc_sc[rows, :] = alpha * acc_sc[rows, :] + jnp.dot(pr.astype(BF16), v, preferred_element_type=F32)
    m_sc[rows, :] = m_new


def _flash_init(m_sc, l_sc, acc_sc):
    m_sc[...] = jnp.full(m_sc.shape, -jnp.inf, F32)
    l_sc[...] = jnp.zeros(l_sc.shape, F32)
    acc_sc[...] = jnp.zeros(acc_sc.shape, F32)


def _mla_attn_kernel(qi_ref, ki_ref, q_ref, k_ref, v_ref, o_ref, m_sc, l_sc, acc_sc, *, sb):
    p = pl.program_id(1)
    qi = qi_ref[p]
    ki = ki_ref[p]
    t = q_ref.shape[0]

    @pl.when(ki == 0)
    def _():
        _flash_init(m_sc, l_sc, acc_sc)

    @pl.when(ki < qi)
    def _():
        for rb in range(t // sb):
            rows = slice(rb * sb, (rb + 1) * sb)
            s = _dot_nt(q_ref[rows, :], k_ref[...])
            _flash_rows(s, v_ref[...], rows, m_sc, l_sc, acc_sc)

    @pl.when(ki == qi)
    def _():
        for rb in range(t // sb):
            rows = slice(rb * sb, (rb + 1) * sb)
            nc = (rb + 1) * sb
            s = _dot_nt(q_ref[rows, :], k_ref[0:nc, :])
            row = lax.broadcasted_iota(jnp.int32, s.shape, 0) + rb * sb
            col = lax.broadcasted_iota(jnp.int32, s.shape, 1)
            s = jnp.where(col <= row, s, -jnp.inf)
            _flash_rows(s, v_ref[0:nc, :], rows, m_sc, l_sc, acc_sc)
        o_ref[...] = (acc_sc[...] / l_sc[...]).astype(o_ref.dtype)


def _mla_attention(q, k, v, *, t=1024, sb=256):
    S = q.shape[0]
    t = min(t, S)
    sb = min(sb, t)
    nb = S // t
    pairs = [(a, b) for a in range(nb) for b in range(a + 1)]
    qi_tab = jnp.asarray(np.array([a for a, _ in pairs], np.int32))
    ki_tab = jnp.asarray(np.array([b for _, b in pairs], np.int32))
    stat = pltpu.VMEM((t, LANES), F32)
    grid_spec = pltpu.PrefetchScalarGridSpec(
        num_scalar_prefetch=2,
        grid=(MLA_HEADS, len(pairs)),
        in_specs=[pl.BlockSpec((t, MLA_QK_PAD), lambda h, p, qt, kt: (qt[p], h)),
                  pl.BlockSpec((t, MLA_QK_PAD), lambda h, p, qt, kt: (kt[p], h)),
                  pl.BlockSpec((t, MLA_V), lambda h, p, qt, kt: (kt[p], h))],
        out_specs=pl.BlockSpec((t, MLA_V), lambda h, p, qt, kt: (qt[p], h)),
        scratch_shapes=[stat, stat, pltpu.VMEM((t, MLA_V), F32)],
    )
    return pl.pallas_call(
        functools.partial(_mla_attn_kernel, sb=sb),
        out_shape=jax.ShapeDtypeStruct((S, MLA_HEADS * MLA_V), BF16),
        grid_spec=grid_spec,
        compiler_params=_cparams(("parallel", "arbitrary")),
        name="mla_attention",
    )(qi_tab, ki_tab, q, k, v)


def _seg_sum(x, e_ref):
    return _split_dot(x, e_ref[...])


def _rwkv_prep_kernel(p_ref, mu_ref, w0_ref, w2_ref, a0_ref, a2_ref, g2_ref, kk_ref, ka_ref, e_ref,
                      r_o, lw_o, k_o, v_o, kkn_o, b_o, g_o, carry):
    i = pl.program_id(0)

    @pl.when(i == 0)
    def _():
        carry[...] = jnp.zeros(carry.shape, F32)

    x = p_ref[...]
    tm = x.shape[0]
    row = lax.broadcasted_iota(jnp.int32, (tm, 1), 0)
    prev = jnp.where(row == 0, carry[...], pltpu.roll(x, 1, 0))
    carry[...] = x[tm - 1:tm, :]
    xs = x + (prev - x) * mu_ref[...]
    W = RWKV_WIDTH
    r = xs[:, :W]
    kb = xs[:, W:2 * W]
    vb = xs[:, 2 * W:3 * W]
    xw = xs[:, 3 * W:3 * W + LANES]
    xa = xs[:, 3 * W + LANES:3 * W + 2 * LANES]
    xg = xs[:, 3 * W + 2 * LANES:]
    w_raw = -_softplus(-(w0_ref[...] + _dot(jnp.tanh(xw), w2_ref[...]))) - 0.5
    a = _sigmoid(a0_ref[...] + _dot(xa, a2_ref[...]))
    kk = kb * kk_ref[...]
    nrm = jnp.sqrt(_seg_sum(kk * kk, e_ref))
    kkn = kk / jnp.maximum(nrm, 1e-12)
    r_o[...] = r
    lw_o[...] = -jnp.exp(w_raw)
    k_o[...] = kb * (1.0 + (a - 1.0) * ka_ref[...])
    v_o[...] = vb
    kkn_o[...] = kkn
    b_o[...] = kkn * a
    g_o[...] = _dot(_sigmoid(xg), g2_ref[...])


def _rwkv_prep(p, mu, w0, w2, a0, a2, g2, k_k, k_a, e, *, tm=256):
    S = p.shape[0]
    tm = min(tm, S)
    W = RWKV_WIDTH
    row = pl.BlockSpec((1, W), lambda i: (0, 0))
    out = pl.BlockSpec((tm, W), lambda i: (i, 0))
    shp = jax.ShapeDtypeStruct((S, W), F32)
    return pl.pallas_call(
        _rwkv_prep_kernel,
        out_shape=(shp,) * 7,
        grid=(S // tm,),
        in_specs=[pl.BlockSpec((tm, RWKV_SEC), lambda i: (i, 0)),
                  pl.BlockSpec((1, RWKV_SEC), lambda i: (0, 0)),
                  row, pl.BlockSpec((LANES, W), lambda i: (0, 0)),
                  row, pl.BlockSpec((LANES, W), lambda i: (0, 0)),
                  pl.BlockSpec((2 * LANES, W), lambda i: (0, 0)),
                  row, row, pl.BlockSpec((W, W), lambda i: (0, 0))],
        out_specs=(out,) * 7,
        scratch_shapes=[pltpu.VMEM((1, RWKV_SEC), F32)],
        compiler_params=_cparams(("arbitrary",)),
        name="rwkv_prep",
    )(p, mu, w0, w2, a0, a2, g2, k_k, k_a, e)


def _bmm(a, b):
    return jnp.einsum('gik,gkj->gij', a.astype(BF16), b.astype(BF16), preferred_element_type=F32)


def _bmm_nt(a, b):
    return jnp.einsum('gik,gjk->gij', a.astype(BF16), b.astype(BF16), preferred_element_type=F32)


def _bsplit_cumsum(tri, x):
    G = x.shape[0]
    tb = jnp.broadcast_to(tri.astype(BF16)[None], (G,) + tri.shape)
    x0 = x.astype(BF16)
    r1 = x - x0.astype(F32)
    x1 = r1.astype(BF16)
    x2 = (r1 - x1.astype(F32)).astype(BF16)
    f = lambda t: jnp.einsum('gts,gsl->gtl', tb, t, preferred_element_type=F32)
    return f(x0) + f(x1) + f(x2)


def _rwkv_kernel(r_ref, lw_ref, k_ref, v_ref, kk_ref, b_ref, g_ref, lnw_ref, lnb_ref, rk_ref, tri_ref, e_ref,
                 o_ref, st_sc, sall_sc, kc_sc, nc_sc, pc_sc, *, hg, G):
    C = RWKV_CHUNK
    LW = hg * RWKV_HEAD
    RW = hg * C
    c_idx = pl.program_id(1)

    @pl.when(c_idx == 0)
    def _():
        st_sc[...] = jnp.zeros(st_sc.shape, F32)

    lw3 = lw_ref[...].reshape(G, C, LW)
    cum3 = _bsplit_cumsum(tri_ref[...], lw3)
    clast = cum3[:, C - 1:C, :]
    r3 = r_ref[...].reshape(G, C, LW)
    k3 = k_ref[...].reshape(G, C, LW)
    v3 = v_ref[...].reshape(G, C, LW)
    kk3 = kk_ref[...].reshape(G, C, LW)
    b3 = b_ref[...].reshape(G, C, LW)
    einv = jnp.exp(-cum3)
    etail = jnp.exp(clast - cum3)
    rt = r3 * jnp.exp(cum3)
    at = -kk3 * jnp.exp(cum3 - lw3)
    bt = b3 * einv
    kt = k3 * einv
    bh = b3 * etail
    kh = k3 * etail

    lane_head = lax.broadcasted_iota(jnp.int32, (1, 1, LW), 2) // RWKV_HEAD

    def stack(x):
        return jnp.concatenate([jnp.where(lane_head == h, x, 0.0) for h in range(hg)], axis=1)

    a2, b2, k2, r2, v2, bh2, kh2 = (stack(t).astype(BF16) for t in (at, bt, kt, rt, v3, bh, kh))
    ri = lax.broadcasted_iota(jnp.int32, (1, RW, RW), 1)
    ci = lax.broadcasted_iota(jnp.int32, (1, RW, RW), 2)
    same = (ri // C) == (ci // C)
    strict = same & (ci < ri)
    incl = same & (ci <= ri)
    eye = (ri == ci).astype(F32)

    bk2 = jnp.concatenate([b2, k2], axis=1)
    a_bk = _bmm_nt(a2, bk2)
    r_bk = _bmm_nt(r2, bk2)
    lmat = jnp.where(strict, a_bk[:, :, :RW], 0.0)
    akm = jnp.where(strict, a_bk[:, :, RW:], 0.0)
    rbm = jnp.where(incl, r_bk[:, :, :RW], 0.0)
    rkm = jnp.where(incl, r_bk[:, :, RW:], 0.0)
    tinv = eye + lmat
    pw = lmat
    for _ in range(int(np.log2(C)) - 1):
        pw = _bmm(pw, pw)
        tinv = tinv + _bmm(tinv, pw)
    wm = _bmm(tinv, a2)
    z = _bmm(tinv, _bmm(akm, v2))
    qm = r2.astype(F32) + _bmm(rbm, wm)
    y0 = _bmm(rkm, v2) + _bmm(rbm, z)
    kc_sc[...] = _bmm(jnp.swapaxes(wm, 1, 2), bh2)
    nc_sc[...] = _bmm(jnp.swapaxes(z, 1, 2), bh2) + _bmm(jnp.swapaxes(v2, 1, 2), kh2)
    pc_sc[...] = jnp.exp(clast)

    def chunk(c, carry):
        st = st_sc[...]
        sall_sc[c] = st
        st_sc[...] = st * pc_sc[c] + _dot(st, kc_sc[c]) + nc_sc[c]
        return carry

    lax.fori_loop(0, G, chunk, 0, unroll=True)

    y2 = _bmm_nt(qm, sall_sc[...]) + y0
    y = y2[:, 0:C]
    for h in range(1, hg):
        y = y + y2[:, h * C:(h + 1) * C]

    y = y.reshape(G * C, LW)
    inv_n = 1.0 / RWKV_HEAD
    mu = _seg_sum(y, e_ref) * inv_n
    d = y - mu
    var = _seg_sum(d * d, e_ref) * inv_n
    yn = d * lax.rsqrt(var + RWKV_GN_EPS) * lnw_ref[...] + lnb_ref[...]
    bonus = _seg_sum(r_ref[...] * k_ref[...] * rk_ref[...], e_ref) * v_ref[...]
    o_ref[...] = ((yn + bonus) * g_ref[...]).astype(o_ref.dtype)


def _rwkv_mix(r, lw, k, v, kkn, b, g, ln_w, ln_b, r_k, *, hg=2, G=8):
    S = r.shape[0]
    C = RWKV_CHUNK
    G = min(G, S // C)
    TB = G * C
    LW = hg * RWKV_HEAD
    ngrp = RWKV_WIDTH // LW
    t = np.arange(C)
    tri = jnp.asarray((t[None, :] <= t[:, None]).astype(np.float32), BF16)
    l = np.arange(LW)
    e = jnp.asarray((l[:, None] // RWKV_HEAD == l[None, :] // RWKV_HEAD).astype(np.float32), BF16)
    blk = pl.BlockSpec((TB, LW), lambda gi, c: (c, gi))
    row = pl.BlockSpec((1, LW), lambda gi, c: (0, gi))
    sq = lambda: pltpu.VMEM((G, LW, LW), F32)
    return pl.pallas_call(
        functools.partial(_rwkv_kernel, hg=hg, G=G),
        out_shape=jax.ShapeDtypeStruct((S, RWKV_WIDTH), BF16),
        grid=(ngrp, S // TB),
        in_specs=[blk] * 7 + [row] * 3 + [pl.BlockSpec((C, C), lambda gi, c: (0, 0)),
                                          pl.BlockSpec((LW, LW), lambda gi, c: (0, 0))],
        out_specs=blk,
        scratch_shapes=[pltpu.VMEM((LW, LW), F32), sq(), sq(), sq(), pltpu.VMEM((G, 1, LW), F32)],
        compiler_params=_cparams(("parallel", "arbitrary")),
        name="rwkv_mix",
    )(r, lw, k, v, kkn, b, g, ln_w, ln_b, r_k, tri, e)


def _outproj_kernel(h_ref, a_ref, b_ref, wa_ref, wb_ref, o_ref):
    o_ref[...] = (h_ref[...] + jnp.dot(a_ref[...], wa_ref[...], preferred_element_type=F32)
                  + jnp.dot(b_ref[...], wb_ref[...], preferred_element_type=F32))


def _outproj(h, a, b, wa, wb, *, tm=512):
    S, D = h.shape
    tm = min(tm, S)
    Ka, Kb = a.shape[1], b.shape[1]
    return pl.pallas_call(
        _outproj_kernel,
        out_shape=jax.ShapeDtypeStruct((S, D), F32),
        grid=(S // tm,),
        in_specs=[pl.BlockSpec((tm, D), lambda i: (i, 0)),
                  pl.BlockSpec((tm, Ka), lambda i: (i, 0)),
                  pl.BlockSpec((tm, Kb), lambda i: (i, 0)),
                  pl.BlockSpec((Ka, D), lambda i: (0, 0)),
                  pl.BlockSpec((Kb, D), lambda i: (0, 0))],
        out_specs=pl.BlockSpec((tm, D), lambda i: (i, 0)),
        compiler_params=_cparams(("parallel",)),
        name="outproj",
    )(h, a, b, wa, wb)


def _pad_cols(w, width):
    return jnp.pad(w, ((0, 0), (0, width - w.shape[1])))


def _rope_gap_layout(w):
    z = jnp.zeros(w.shape[:-1] + (32,), w.dtype)
    return jnp.concatenate([w[..., :32], z, w[..., 32:], z], axis=-1)


def _even_mixer(h, hn, tabs, w_in, q_norm, w_uq, kv_norm, w_ukv, q_head_norm, k_head_norm,
                mu, w0, w2, a0, a2, g2, k_k, k_a, r_k, ln_w, ln_b, w_out):
    cos1, sin1 = tabs[0], tabs[1]
    W = RWKV_WIDTH
    o_cq = 0
    o_ckv = MLA_Q_LORA
    o_kr = o_ckv + MLA_KV_LORA
    o_rw = o_kr + MLA_ROPE
    rw = w_in[:, o_rw:]
    mu_r = mu[None, :]

    def rwkv_layout(t):
        return jnp.concatenate([t[:, :3 * W], _pad_cols(t[:, 3 * W:3 * W + RWKV_W_LORA], LANES),
                                _pad_cols(t[:, 3 * W + RWKV_W_LORA:3 * W + RWKV_W_LORA + RWKV_A_LORA], LANES),
                                _pad_cols(t[:, 3 * W + RWKV_W_LORA + RWKV_A_LORA:], 2 * LANES)], axis=1)

    w_in_p = jnp.concatenate([rwkv_layout(rw), w_in[:, o_cq:o_ckv], w_in[:, o_ckv:o_kr],
                              _rope_gap_layout(w_in[:, o_kr:o_rw])], axis=1)
    w_in_p = _pad_cols(w_in_p, 4608).astype(BF16)
    p = _proj(hn, w_in_p, tn=1536)

    wq = w_uq.reshape(MLA_Q_LORA, MLA_HEADS, MLA_QK)
    wq = jnp.concatenate([wq[..., :MLA_NOPE], _rope_gap_layout(wq[..., MLA_NOPE:])], axis=-1)
    wq = wq.reshape(MLA_Q_LORA, MLA_HEADS * MLA_QK_PAD).astype(BF16)
    scale = MLA_QK ** -0.5 * LOG2E
    hg_q = jnp.concatenate([q_head_norm[:MLA_NOPE], _rope_gap_layout(q_head_norm[MLA_NOPE:])])[None, :] * scale
    gn = k_head_norm[None, :MLA_NOPE]
    gr = _rope_gap_layout(k_head_norm[MLA_NOPE:])[None, :]
    q, kmat, vmat = _mla_qkv(p, q_norm[None, :], kv_norm[None, :], wq, w_ukv.astype(BF16), hg_q, gn, gr, cos1, sin1)
    o_a = _mla_attention(q, kmat, vmat)

    l = np.arange(W)
    e = jnp.asarray((l[:, None] // RWKV_HEAD == l[None, :] // RWKV_HEAD).astype(np.float32), BF16)
    w2p = jnp.pad(w2, ((0, LANES - RWKV_W_LORA), (0, 0))).astype(BF16)
    a2p = jnp.pad(a2, ((0, LANES - RWKV_A_LORA), (0, 0))).astype(BF16)
    g2p = jnp.pad(g2, ((0, 2 * LANES - RWKV_G_LORA), (0, 0))).astype(BF16)
    r, lw, k, v, kkn, b, g = _rwkv_prep(p, rwkv_layout(mu_r), w0[None, :], w2p, a0[None, :], a2p, g2p,
                                         k_k[None, :], k_a[None, :], e)
    o_b = _rwkv_mix(r, lw, k, v, kkn, b, g, ln_w[None, :], ln_b[None, :], r_k.reshape(1, W))
    wo = w_out.astype(BF16)
    return _outproj(h, o_a, o_b, wo[:MLA_HEADS * MLA_V], wo[MLA_HEADS * MLA_V:])


def _swa_qkv_kernel(x_ref, w_ref, hg_ref, cos_ref, sin_ref, o_ref):
    n = pl.program_id(0)
    acc = jnp.dot(x_ref[...], w_ref[...], preferred_element_type=F32)

    @pl.when(n < 2)
    def _():
        cos = cos_ref[...]
        sin = sin_ref[...]
        for c in range(acc.shape[1] // LANES):
            sl = slice(c * LANES, (c + 1) * LANES)
            o_ref[:, sl] = _rope_apply(_rms(acc[:, sl], hg_ref[:, sl]), cos, sin).astype(o_ref.dtype)

    @pl.when(n == 2)
    def _():
        o_ref[...] = acc.astype(o_ref.dtype)


def _swa_qkv(x, w, hg, cos, sin, *, tm=512):
    S, K = x.shape
    tn = SWA_HEADS * SWA_DIM
    tm = min(tm, S)
    tab = pl.BlockSpec((tm, LANES), lambda n, i: (i, 0))
    return pl.pallas_call(
        _swa_qkv_kernel,
        out_shape=jax.ShapeDtypeStruct((S, 3 * tn), BF16),
        grid=(3, S // tm),
        in_specs=[pl.BlockSpec((tm, K), lambda n, i: (i, 0)),
                  pl.BlockSpec((K, tn), lambda n, i: (0, n)),
                  pl.BlockSpec((1, tn), lambda n, i: (0, jnp.minimum(n, 1))),
                  tab, tab],
        out_specs=pl.BlockSpec((tm, tn), lambda n, i: (i, n)),
        compiler_params=_cparams(("parallel", "parallel")),
        name="swa_qkv",
    )(x, w, hg, cos, sin)


def _dilated_bias(t, nrel):
    row = np.arange(t)[:, None]
    col = np.arange(t)[None, :]
    out = np.empty((nrel, t, t), np.float32)
    for r in range(nrel):
        delta = r * t + row - col
        cnt = np.zeros((t, t), np.float64)
        for window, dilation in SWA_PATTERNS:
            cnt += (delta >= 0) & (delta <= window) & (delta % dilation == 0)
        with np.errstate(divide="ignore"):
            out[r] = np.log2(cnt)
    return out


def _dilated_kernel(q_ref, k_ref, v_ref, bias_ref, o_ref, m_sc, l_sc, acc_sc, *, sb, nrel, live):
    qi = pl.program_id(1)
    r = pl.program_id(2)
    t = q_ref.shape[0]

    @pl.when(r == 0)
    def _():
        _flash_init(m_sc, l_sc, acc_sc)

    for rr in range(nrel):
        @pl.when((r == rr) & (qi - r >= 0))
        def _(rr=rr):
            for rb in range(t // sb):
                rows = slice(rb * sb, (rb + 1) * sb)
                lo, hi = live[rr][rb]
                s = _dot_nt(q_ref[rows, :], k_ref[lo:hi, :]) + bias_ref[rr, rows, lo:hi]
                _flash_rows(s, v_ref[lo:hi, :], rows, m_sc, l_sc, acc_sc)

    @pl.when(r == nrel - 1)
    def _():
        o_ref[...] = (acc_sc[...] / l_sc[...]).astype(o_ref.dtype)


def _dilated_attention(qkv, *, t=512, sb=256):
    S = qkv.shape[0]
    t = min(t, S)
    sb = min(sb, t)
    max_window = max(w for w, _ in SWA_PATTERNS)
    nrel = min(-(-max_window // t) + 1, S // t)
    H = SWA_HEADS
    bias = _dilated_bias(t, nrel)
    live = []
    for r in range(nrel):
        per_rb = []
        for rb in range(t // sb):
            cols = np.nonzero(np.isfinite(bias[r, rb * sb:(rb + 1) * sb]).any(axis=0))[0]
            lo = int(cols.min()) // LANES * LANES
            hi = -(-(int(cols.max()) + 1) // LANES) * LANES
            per_rb.append((lo, hi))
        live.append(per_rb)
    stat = pltpu.VMEM((t, LANES), F32)
    return pl.pallas_call(
        functools.partial(_dilated_kernel, sb=sb, nrel=nrel, live=live),
        out_shape=jax.ShapeDtypeStruct((S, H * SWA_DIM), BF16),
        grid=(H, S // t, nrel),
        in_specs=[pl.BlockSpec((t, SWA_DIM), lambda h, i, r: (i, h)),
                  pl.BlockSpec((t, SWA_DIM), lambda h, i, r: (jnp.maximum(i - r, 0), H + h)),
                  pl.BlockSpec((t, SWA_DIM), lambda h, i, r: (jnp.maximum(i - r, 0), 2 * H + h)),
                  pl.BlockSpec((nrel, t, t), lambda h, i, r: (0, 0, 0))],
        out_specs=pl.BlockSpec((t, SWA_DIM), lambda h, i, r: (i, h)),
        scratch_shapes=[stat, stat, pltpu.VMEM((t, SWA_DIM), F32)],
        compiler_params=_cparams(("parallel", "parallel", "arbitrary")),
        name="dilated_attention",
    )(qkv, qkv, qkv, jnp.asarray(bias))


def _gla_kernel(q_ref, k_ref, v_ref, glr_ref, wgu_ref, bg_ref, rd_ref, gn_ref, tri_ref, o_ref,
                st_sc, qt_sc, gkv_sc, eb_sc, oi_sc, *, G):
    C = GLA_CHUNK
    SB = GLA_SUB
    DK = GLA_DK
    DV = GLA_DV
    c_idx = pl.program_id(1)

    @pl.when(c_idx == 0)
    def _():
        st_sc[...] = jnp.zeros(st_sc.shape, F32)

    z = _dot(glr_ref[...], wgu_ref[...]) + bg_ref[...]
    lg = -_softplus(-z) * (1.0 / GLA_NORMALIZER)
    bc = _dot_split(tri_ref[...], lg).reshape(G, C, DK)
    q = (q_ref[...] * (DK ** -0.5)).reshape(G, C, DK)
    k = k_ref[...].reshape(G, C, DK)
    v = v_ref[...].reshape(G, C, DV)
    blast = bc[:, C - 1:C, :]
    qt_sc[...] = q * jnp.exp(bc)
    khat = k * jnp.exp(blast - bc)
    gkv_sc[...] = _bmm(jnp.swapaxes(v, 1, 2), khat)
    eb_sc[...] = jnp.exp(blast)

    ti = lax.broadcasted_iota(jnp.int32, (1, SB, SB, 1), 1)
    si = lax.broadcasted_iota(jnp.int32, (1, SB, SB, 1), 2)
    causal = si <= ti
    for sb in range(C // SB):
        lo = sb * SB
        qs = q[:, lo:lo + SB]
        ks = k[:, lo:lo + SB]
        bs = bc[:, lo:lo + SB]
        rel = bs[:, :, None, :] - bs[:, None, :, :]
        dec = jnp.exp(jnp.where(causal, rel, -jnp.inf))
        att = jnp.sum(qs[:, :, None, :] * ks[:, None, :, :] * dec, axis=-1)
        o_sb = _bmm(att, v[:, lo:lo + SB])
        if sb > 0:
            bm = bc[:, lo - 1:lo]
            qsc = qs * jnp.exp(bs - bm)
            ksc = k[:, :lo] * jnp.exp(bm - bc[:, :lo])
            o_sb = o_sb + _bmm(_bmm_nt(qsc, ksc), v[:, :lo])
        oi_sc[:, lo:lo + SB, :] = o_sb

    def chunk(c, carry):
        st = st_sc[...]
        oi_sc[c] = oi_sc[c] + _dot_nt(qt_sc[c], st)
        st_sc[...] = st * eb_sc[c] + gkv_sc[c]
        return carry

    lax.fori_loop(0, G, chunk, 0)

    o = oi_sc[...].reshape(G * C, DV)
    rd = rd_ref[...]
    o_ref[...] = (_rms(o, gn_ref[...]) * (rd * _sigmoid(rd))).astype(o_ref.dtype)


def _gla(pb, wgu, bg, gn, *, G=8):
    S = pb.shape[0]
    C = GLA_CHUNK
    G = min(G, S // C)
    TB = G * C
    H, DK, DV = GLA_HEADS, GLA_DK, GLA_DV
    t = np.arange(TB)
    tri = jnp.asarray(((t[:, None] // C == t[None, :] // C) & (t[None, :] <= t[:, None])).astype(np.float32), BF16)
    glr_block = (2 * H * DK + 2 * H * DV) // LANES
    return pl.pallas_call(
        functools.partial(_gla_kernel, G=G),
        out_shape=jax.ShapeDtypeStruct((S, H * DV), BF16),
        grid=(H, S // TB),
        in_specs=[pl.BlockSpec((TB, DK), lambda h, c: (c, h)),
                  pl.BlockSpec((TB, DK), lambda h, c: (c, H + h)),
                  pl.BlockSpec((TB, DV), lambda h, c: (c, (2 * H * DK) // DV + h)),
                  pl.BlockSpec((TB, LANES), lambda h, c: (c, glr_block)),
                  pl.BlockSpec((LANES, DK), lambda h, c: (0, h)),
                  pl.BlockSpec((1, DK), lambda h, c: (0, h)),
                  pl.BlockSpec((TB, DV), lambda h, c: (c, (2 * H * DK + H * DV) // DV + h)),
                  pl.BlockSpec((1, DV), lambda h, c: (0, h)),
                  pl.BlockSpec((TB, TB), lambda h, c: (0, 0))],
        out_specs=pl.BlockSpec((TB, DV), lambda h, c: (c, h)),
        scratch_shapes=[pltpu.VMEM((DV, DK), F32), pltpu.VMEM((G, C, DK), F32), pltpu.VMEM((G, DV, DK), F32),
                        pltpu.VMEM((G, 1, DK), F32), pltpu.VMEM((G, C, DV), F32)],
        compiler_params=_cparams(("parallel", "arbitrary")),
        name="gla",
    )(pb, pb, pb, pb, wgu, bg, pb, gn, tri)


def _odd_mixer(h, hn, tabs, w_in, q_head_norm, k_head_norm, w_gate_up, b_gate, gla_norm, w_out):
    cos2, sin2 = tabs[2], tabs[3]
    nq = SWA_HEADS * SWA_DIM
    hg = jnp.concatenate([jnp.tile(q_head_norm * (SWA_DIM ** -0.5 * LOG2E), SWA_HEADS), jnp.tile(k_head_norm, SWA_HEADS)])[None, :]
    qkv = _swa_qkv(hn, w_in[:, :3 * nq].astype(BF16), hg, cos2, sin2)
    o_c = _dilated_attention(qkv)
    o = 3 * nq
    dk, dv = GLA_HEADS * GLA_DK, GLA_HEADS * GLA_DV
    wb = jnp.concatenate([w_in[:, o:o + 2 * dk + dv], w_in[:, o + 2 * dk + dv + GLA_LORA:],
                          _pad_cols(w_in[:, o + 2 * dk + dv:o + 2 * dk + dv + GLA_LORA], LANES)], axis=1)
    wb = _pad_cols(wb, 3584).astype(BF16)
    pb = _proj(hn, wb, tn=1792)
    wgu = jnp.pad(w_gate_up, ((0, LANES - GLA_LORA), (0, 0))).astype(BF16)
    o_d = _gla(pb, wgu, b_gate[None, :], gla_norm.reshape(1, dv))
    wo = w_out.astype(BF16)
    return _outproj(h, o_c, o_d, wo[:nq], wo[nq:])


def kernel(x, positions, ffn_norm, ffn_w_gate, ffn_w_up, ffn_w_down, mix_norm, mla_rwkv_w_in, mla_q_norm, mla_w_uq,
           mla_kv_norm, mla_w_ukv, mla_q_head_norm, mla_k_head_norm, rwkv_mu, rwkv_w0, rwkv_w2, rwkv_a0, rwkv_a2,
           rwkv_g2, rwkv_k_k, rwkv_k_a, rwkv_r_k, rwkv_ln_w, rwkv_ln_b, mla_rwkv_w_out, swa_gla_w_in,
           swa_q_head_norm, swa_k_head_norm, gla_w_gate_up, gla_b_gate, gla_norm, swa_gla_w_out):
    B, S, D = x.shape
    assert B == 1
    h = x.reshape(S, D)
    tabs = _rope_tables(positions.reshape(S))
    wg = ffn_w_gate.astype(BF16)
    wu = ffn_w_up.astype(BF16)
    wd = ffn_w_down.astype(BF16)
    depth = ffn_norm.shape[0]
    for layer in range(depth):
        i = layer // 2
        h, hn = _ffn(h, ffn_norm, wg, wu, wd, layer, 0, next_norm=mix_norm[layer])
        if layer % 2 == 0:
            h = _even_mixer(h, hn, tabs, mla_rwkv_w_in[i], mla_q_norm[i], mla_w_uq[i], mla_kv_norm[i],
                            mla_w_ukv[i], mla_q_head_norm[i], mla_k_head_norm[i], rwkv_mu[i], rwkv_w0[i],
                            rwkv_w2[i], rwkv_a0[i], rwkv_a2[i], rwkv_g2[i], rwkv_k_k[i], rwkv_k_a[i], rwkv_r_k[i],
                            rwkv_ln_w[i], rwkv_ln_b[i], mla_rwkv_w_out[i])
        else:
            h = _odd_mixer(h, hn, tabs, swa_gla_w_in[i], swa_q_head_norm[i], swa_k_head_norm[i],
                           gla_w_gate_up[i], gla_b_gate[i], gla_norm[i], swa_gla_w_out[i])
        h = _ffn(h, ffn_norm, wg, wu, wd, layer, 1)
    return h.reshape(B, S, D)
```

```python
import functools

import numpy as np
import jax
import jax.numpy as jnp
from jax import lax
from jax.experimental import pallas as pl
from jax.experimental.pallas import tpu as pltpu

F32 = jnp.float32
BF16 = jnp.bfloat16

LANES = 128
VMEM_LIMIT = 56 * 1024 * 1024

D_MODEL = 2048
D_FF = 5632
MACARON_WEIGHT = 0.5
NORM_EPS = 1e-6
ROPE_THETA = 10000.0
LOG2E = float(np.log2(np.e))

MLA_HEADS = 8
MLA_NOPE = 128
MLA_ROPE = 64
MLA_QK = MLA_NOPE + MLA_ROPE
MLA_QK_PAD = 256
MLA_V = 128
MLA_Q_LORA = 512
MLA_KV_LORA = 256

RWKV_HEAD = 64
RWKV_WIDTH = 1024
RWKV_HEADS = 16
RWKV_W_LORA = 64
RWKV_A_LORA = 64
RWKV_G_LORA = 160
RWKV_GN_EPS = 64e-5
RWKV_SEC = 3584
RWKV_CHUNK = 64

SWA_HEADS = 8
SWA_DIM = 128
SWA_PATTERNS = ((128, 1), (512, 4), (2048, 16))

GLA_HEADS = 4
GLA_DK = 128
GLA_DV = 256
GLA_LORA = 16
GLA_NORMALIZER = 16.0
GLA_CHUNK = 64
GLA_SUB = 16


def _cparams(sem):
    return pltpu.CompilerParams(dimension_semantics=sem, vmem_limit_bytes=VMEM_LIMIT)


def _rms(x, g):
    return x * lax.rsqrt(jnp.mean(x * x, axis=-1, keepdims=True) + NORM_EPS) * g


def _dot(a, b):
    return jnp.dot(a.astype(BF16), b.astype(BF16), preferred_element_type=F32)


def _dot_nt(a, b):
    return lax.dot_general(a.astype(BF16), b.astype(BF16), (((1,), (1,)), ((), ())),
                           preferred_element_type=F32)


def _dot_split(a, b):
    b0 = b.astype(BF16)
    r1 = b - b0.astype(F32)
    b1 = r1.astype(BF16)
    b2 = (r1 - b1.astype(F32)).astype(BF16)
    a = a.astype(BF16)
    return (jnp.dot(a, b0, preferred_element_type=F32) + jnp.dot(a, b1, preferred_element_type=F32)
            + jnp.dot(a, b2, preferred_element_type=F32))


def _split_dot(a, b):
    a0 = a.astype(BF16)
    r1 = a - a0.astype(F32)
    a1 = r1.astype(BF16)
    a2 = (r1 - a1.astype(F32)).astype(BF16)
    b = b.astype(BF16)
    return (jnp.dot(a0, b, preferred_element_type=F32) + jnp.dot(a1, b, preferred_element_type=F32)
            + jnp.dot(a2, b, preferred_element_type=F32))


def _sigmoid(x):
    return 1.0 / (1.0 + jnp.exp(-x))


def _softplus(x):
    return jnp.maximum(x, 0.0) + jnp.log(1.0 + jnp.exp(-jnp.abs(x)))


def _rope_apply(x, cos, sin_signed):
    return x * cos + pltpu.roll(x, 64, 1) * sin_signed


def _ffn_kernel(h_ref, g_ref, wg_ref, wu_ref, wd_ref, *rest, emit_norm):
    if emit_norm:
        g2_ref, o_ref, hn_ref, xn_ref = rest
    else:
        o_ref, xn_ref = rest
    j = pl.program_id(1)

    @pl.when(j == 0)
    def _():
        x = h_ref[...]
        xn_ref[...] = _rms(x, g_ref[...]).astype(BF16)
        o_ref[...] = x

    xn = xn_ref[...]
    gate = jnp.dot(xn, wg_ref[...], preferred_element_type=F32)
    up = jnp.dot(xn, wu_ref[...], preferred_element_type=F32)
    act = (gate * _sigmoid(gate) * up * MACARON_WEIGHT).astype(BF16)
    o_ref[...] += jnp.dot(act, wd_ref[...], preferred_element_type=F32)

    if emit_norm:
        @pl.when(j == pl.num_programs(1) - 1)
        def _():
            hn_ref[...] = _rms(o_ref[...], g2_ref[...]).astype(hn_ref.dtype)


def _ffn(h, g, wg, wu, wd, layer, k, next_norm=None, *, tm=512, tf=512):
    S, D = h.shape
    F = wg.shape[-1]
    tm = min(tm, S)
    emit_norm = next_norm is not None
    wspec = pl.BlockSpec((None, None, D, tf), lambda i, j: (layer, k, 0, j))
    row = pl.BlockSpec((1, D), lambda i, j: (0, 0))
    tile = pl.BlockSpec((tm, D), lambda i, j: (i, 0))
    in_specs = [tile, row, wspec, wspec, pl.BlockSpec((None, None, tf, D), lambda i, j: (layer, k, j, 0))]
    args = [h, g[layer, k][None, :], wg, wu, wd]
    out_shape = jax.ShapeDtypeStruct((S, D), F32)
    out_specs = tile
    if emit_norm:
        in_specs.append(row)
        args.append(next_norm[None, :])
        out_shape = (out_shape, jax.ShapeDtypeStruct((S, D), BF16))
        out_specs = (tile, tile)
    return pl.pallas_call(
        functools.partial(_ffn_kernel, emit_norm=emit_norm),
        out_shape=out_shape,
        grid=(S // tm, F // tf),
        in_specs=in_specs,
        out_specs=out_specs,
        scratch_shapes=[pltpu.VMEM((tm, D), BF16)],
        compiler_params=_cparams(("parallel", "arbitrary")),
        name="ffn",
    )(*args)


def _rope_tab_kernel(pos_ref, f1_ref, s1_ref, f2_ref, s2_ref, c1_ref, n1_ref, c2_ref, n2_ref):
    pos = pos_ref[...].astype(F32)
    a1 = pos * f1_ref[...]
    c1_ref[...] = jnp.cos(a1)
    n1_ref[...] = jnp.sin(a1) * s1_ref[...]
    a2 = pos * f2_ref[...]
    c2_ref[...] = jnp.cos(a2)
    n2_ref[...] = jnp.sin(a2) * s2_ref[...]


def _rope_tables(positions, *, tm=1024):
    S = positions.shape[0]
    tm = min(tm, S)
    f32half = ROPE_THETA ** (-jnp.arange(MLA_ROPE // 2, dtype=F32) / (MLA_ROPE // 2))
    z = jnp.zeros((32,), F32)
    f1 = jnp.concatenate([f32half, z, f32half, z])[None]
    f64half = ROPE_THETA ** (-jnp.arange(SWA_DIM // 2, dtype=F32) / (SWA_DIM // 2))
    f2 = jnp.concatenate([f64half, f64half])[None]
    sign = jnp.concatenate([-jnp.ones((64,), F32), jnp.ones((64,), F32)])[None]
    row = pl.BlockSpec((1, LANES), lambda i: (0, 0))
    tab = pl.BlockSpec((tm, LANES), lambda i: (i, 0))
    shp = jax.ShapeDtypeStruct((S, LANES), F32)
    return pl.pallas_call(
        _rope_tab_kernel,
        out_shape=(shp, shp, shp, shp),
        grid=(S // tm,),
        in_specs=[pl.BlockSpec((tm, 1), lambda i: (i, 0)), row, row, row, row],
        out_specs=(tab, tab, tab, tab),
        compiler_params=_cparams(("parallel",)),
        name="rope_tables",
    )(positions.reshape(S, 1), f1, sign, f2, sign)


def _proj_kernel(x_ref, w_ref, o_ref):
    o_ref[...] = jnp.dot(x_ref[...], w_ref[...], preferred_element_type=F32).astype(o_ref.dtype)


def _proj(x, w, *, tm=512, tn, out_dtype=F32):
    S, K = x.shape
    N = w.shape[1]
    tm = min(tm, S)
    return pl.pallas_call(
        _proj_kernel,
        out_shape=jax.ShapeDtypeStruct((S, N), out_dtype),
        grid=(N // tn, S // tm),
        in_specs=[pl.BlockSpec((tm, K), lambda n, i: (i, 0)),
                  pl.BlockSpec((K, tn), lambda n, i: (0, n))],
        out_specs=pl.BlockSpec((tm, tn), lambda n, i: (i, n)),
        compiler_params=_cparams(("parallel", "parallel")),
        name="proj",
    )(x, w)


def _mla_qkv_kernel(cq_ref, ckv_ref, kr_ref, qn_ref, kvn_ref, wq_ref, wkv_ref, hgq_ref, gn_ref, gr_ref,
                    cos_ref, sin_ref, q_ref, k_ref, v_ref):
    cos = cos_ref[...]
    sin = sin_ref[...]
    accq = _dot(_rms(cq_ref[...], qn_ref[...]), wq_ref[...])
    acckv = _dot(_rms(ckv_ref[...], kvn_ref[...]), wkv_ref[...])
    kr = kr_ref[...]
    kr_ss = jnp.sum(kr * kr, axis=-1, keepdims=True)
    inv_d = 1.0 / MLA_QK
    for h in range(MLA_HEADS):
        lo = h * MLA_QK_PAD
        a = accq[:, lo:lo + MLA_QK_PAD]
        y = a * lax.rsqrt(jnp.sum(a * a, axis=-1, keepdims=True) * inv_d + NORM_EPS) * hgq_ref[...]
        q_ref[:, lo:lo + LANES] = y[:, :LANES].astype(q_ref.dtype)
        q_ref[:, lo + LANES:lo + MLA_QK_PAD] = _rope_apply(y[:, LANES:], cos, sin).astype(q_ref.dtype)
        kn = acckv[:, lo:lo + LANES]
        rs = lax.rsqrt((jnp.sum(kn * kn, axis=-1, keepdims=True) + kr_ss) * inv_d + NORM_EPS)
        k_ref[:, lo:lo + LANES] = (kn * rs * gn_ref[...]).astype(k_ref.dtype)
        k_ref[:, lo + LANES:lo + MLA_QK_PAD] = _rope_apply(kr * rs * gr_ref[...], cos, sin).astype(k_ref.dtype)
        v_ref[:, h * MLA_V:(h + 1) * MLA_V] = acckv[:, lo + LANES:lo + MLA_QK_PAD].astype(v_ref.dtype)


def _mla_qkv(p, qn, kvn, wq, wkv, hgq, gn, gr, cos, sin, *, tm=256):
    S = p.shape[0]
    tm = min(tm, S)
    cq_block = RWKV_SEC // MLA_Q_LORA
    ckv_block = (RWKV_SEC + MLA_Q_LORA) // MLA_KV_LORA
    kr_block = (RWKV_SEC + MLA_Q_LORA + MLA_KV_LORA) // LANES
    NQ = MLA_HEADS * MLA_QK_PAD
    tab = pl.BlockSpec((tm, LANES), lambda i: (i, 0))
    full = lambda a: pl.BlockSpec(a.shape, lambda i: (0, 0))
    return pl.pallas_call(
        _mla_qkv_kernel,
        out_shape=(jax.ShapeDtypeStruct((S, NQ), BF16), jax.ShapeDtypeStruct((S, NQ), BF16),
                   jax.ShapeDtypeStruct((S, MLA_HEADS * MLA_V), BF16)),
        grid=(S // tm,),
        in_specs=[pl.BlockSpec((tm, MLA_Q_LORA), lambda i: (i, cq_block)),
                  pl.BlockSpec((tm, MLA_KV_LORA), lambda i: (i, ckv_block)),
                  pl.BlockSpec((tm, LANES), lambda i: (i, kr_block)),
                  full(qn), full(kvn), full(wq), full(wkv), full(hgq), full(gn), full(gr), tab, tab],
        out_specs=(pl.BlockSpec((tm, NQ), lambda i: (i, 0)), pl.BlockSpec((tm, NQ), lambda i: (i, 0)),
                   pl.BlockSpec((tm, MLA_HEADS * MLA_V), lambda i: (i, 0))),
        compiler_params=_cparams(("parallel",)),
        name="mla_qkv",
    )(p, p, p, qn, kvn, wq, wkv, hgq, gn, gr, cos, sin)


def _flash_rows(s, v1, rows, m_sc, acc_sc):
    m_prev = m_sc[rows, :]
    m_new = jnp.maximum(m_prev, jnp.max(s, axis=-1, keepdims=True))
    alpha = jnp.exp2(m_prev - m_new)
    pr = jnp.exp2((s - jnp.concatenate([m_new] * (s.shape[1] // LANES), axis=1)).astype(BF16))
    acc_sc[rows, :] = (jnp.concatenate([alpha, alpha], axis=1) * acc_sc[rows, :]
                       + jnp.dot(pr, v1, preferred_element_type=F32))
    m_sc[rows, :] = m_new


def _flash_init(m_sc, acc_sc):
    m_sc[...] = jnp.full(m_sc.shape, -jnp.inf, F32)
    acc_sc[...] = jnp.zeros(acc_sc.shape, F32)


def _with_ones(v):
    return jnp.concatenate([v, jnp.ones(v.shape, v.dtype)], axis=1)


def _flash_finish(acc_sc, o_ref):
    acc = acc_sc[...]
    d = acc.shape[1] // 2
    o_ref[...] = (acc[:, :d] / acc[:, d:]).astype(o_ref.dtype)


def _mla_attn_kernel(qi_ref, ki_ref, q_ref, k_ref, v_ref, o_ref, m_sc, acc_sc, *, sb):
    p = pl.program_id(1)
    qi = qi_ref[p]
    ki = ki_ref[p]
    t = q_ref.shape[0]

    @pl.when(ki == 0)
    def _():
        _flash_init(m_sc, acc_sc)

    @pl.when(ki < qi)
    def _():
        v1 = _with_ones(v_ref[...])
        for rb in range(t // sb):
            rows = slice(rb * sb, (rb + 1) * sb)
            s = _dot_nt(q_ref[rows, :], k_ref[...])
            _flash_rows(s, v1, rows, m_sc, acc_sc)

    @pl.when(ki == qi)
    def _():
        v1 = _with_ones(v_ref[...])
        for rb in range(t // sb):
            rows = slice(rb * sb, (rb + 1) * sb)
            nc = (rb + 1) * sb
            s = _dot_nt(q_ref[rows, :], k_ref[0:nc, :])
            row = lax.broadcasted_iota(jnp.int32, s.shape, 0) + rb * sb
            col = lax.broadcasted_iota(jnp.int32, s.shape, 1)
            s = jnp.where(col <= row, s, -jnp.inf)
            _flash_rows(s, v1[0:nc, :], rows, m_sc, acc_sc)
        _flash_finish(acc_sc, o_ref)


def _mla_attention(q, k, v, *, t=2048, sb=256):
    S = q.shape[0]
    t = min(t, S)
    sb = min(sb, t)
    nb = S // t
    pairs = [(a, b) for a in range(nb) for b in range(a + 1)]
    qi_tab = jnp.asarray(np.array([a for a, _ in pairs], np.int32))
    ki_tab = jnp.asarray(np.array([b for _, b in pairs], np.int32))
    stat = pltpu.VMEM((t, LANES), F32)
    grid_spec = pltpu.PrefetchScalarGridSpec(
        num_scalar_prefetch=2,
        grid=(MLA_HEADS, len(pairs)),
        in_specs=[pl.BlockSpec((t, MLA_QK_PAD), lambda h, p, qt, kt: (qt[p], h)),
                  pl.BlockSpec((t, MLA_QK_PAD), lambda h, p, qt, kt: (kt[p], h)),
                  pl.BlockSpec((t, MLA_V), lambda h, p, qt, kt: (kt[p], h))],
        out_specs=pl.BlockSpec((t, MLA_V), lambda h, p, qt, kt: (qt[p], h)),
        scratch_shapes=[stat, pltpu.VMEM((t, 2 * MLA_V), F32)],
    )
    return pl.pallas_call(
        functools.partial(_mla_attn_kernel, sb=sb),
        out_shape=jax.ShapeDtypeStruct((S, MLA_HEADS * MLA_V), BF16),
        grid_spec=grid_spec,
        compiler_params=_cparams(("parallel", "arbitrary")),
        name="mla_attention",
    )(qi_tab, ki_tab, q, k, v)


def _seg_sum(x, e_ref):
    return _split_dot(x, e_ref[...])


def _rwkv_prep_kernel(p_ref, mu_ref, w0_ref, w2_ref, a0_ref, a2_ref, g2_ref, kk_ref, ka_ref, e_ref,
                      r_o, lw_o, k_o, v_o, kkn_o, b_o, g_o, carry):
    i = pl.program_id(0)

    @pl.when(i == 0)
    def _():
        carry[...] = jnp.zeros(carry.shape, F32)

    x = p_ref[...]
    tm = x.shape[0]
    row = lax.broadcasted_iota(jnp.int32, (tm, 1), 0)
    prev = jnp.where(row == 0, carry[...], pltpu.roll(x, 1, 0))
    carry[...] = x[tm - 1:tm, :]
    xs = x + (prev - x) * mu_ref[...]
    W = RWKV_WIDTH
    r = xs[:, :W]
    kb = xs[:, W:2 * W]
    vb = xs[:, 2 * W:3 * W]
    xw = xs[:, 3 * W:3 * W + LANES]
    xa = xs[:, 3 * W + LANES:3 * W + 2 * LANES]
    xg = xs[:, 3 * W + 2 * LANES:]
    w_raw = -_softplus(-(w0_ref[...] + _dot(jnp.tanh(xw), w2_ref[...]))) - 0.5
    a = _sigmoid(a0_ref[...] + _dot(xa, a2_ref[...]))
    kk = kb * kk_ref[...]
    nrm = jnp.sqrt(_seg_sum(kk * kk, e_ref))
    kkn = kk / jnp.maximum(nrm, 1e-12)
    r_o[...] = r
    lw_o[...] = -jnp.exp(w_raw)
    k_o[...] = kb * (1.0 + (a - 1.0) * ka_ref[...])
    v_o[...] = vb
    kkn_o[...] = kkn
    b_o[...] = kkn * a
    g_o[...] = _dot(_sigmoid(xg), g2_ref[...])


def _rwkv_prep(p, mu, w0, w2, a0, a2, g2, k_k, k_a, e, *, tm=256):
    S = p.shape[0]
    tm = min(tm, S)
    W = RWKV_WIDTH
    row = pl.BlockSpec((1, W), lambda i: (0, 0))
    out = pl.BlockSpec((tm, W), lambda i: (i, 0))
    shp = jax.ShapeDtypeStruct((S, W), F32)
    return pl.pallas_call(
        _rwkv_prep_kernel,
        out_shape=(shp,) * 7,
        grid=(S // tm,),
        in_specs=[pl.BlockSpec((tm, RWKV_SEC), lambda i: (i, 0)),
                  pl.BlockSpec((1, RWKV_SEC), lambda i: (0, 0)),
                  row, pl.BlockSpec((LANES, W), lambda i: (0, 0)),
                  row, pl.BlockSpec((LANES, W), lambda i: (0, 0)),
                  pl.BlockSpec((2 * LANES, W), lambda i: (0, 0)),
                  row, row, pl.BlockSpec((W, W), lambda i: (0, 0))],
        out_specs=(out,) * 7,
        scratch_shapes=[pltpu.VMEM((1, RWKV_SEC), F32)],
        compiler_params=_cparams(("arbitrary",)),
        name="rwkv_prep",
    )(p, mu, w0, w2, a0, a2, g2, k_k, k_a, e)


def _bmm(a, b):
    return jnp.einsum('gik,gkj->gij', a.astype(BF16), b.astype(BF16), preferred_element_type=F32)


def _bmm_nt(a, b):
    return jnp.einsum('gik,gjk->gij', a.astype(BF16), b.astype(BF16), preferred_element_type=F32)


def _bsplit_cumsum(tri, x):
    G = x.shape[0]
    tb = jnp.broadcast_to(tri.astype(BF16)[None], (G,) + tri.shape)
    x0 = x.astype(BF16)
    r1 = x - x0.astype(F32)
    x1 = r1.astype(BF16)
    x2 = (r1 - x1.astype(F32)).astype(BF16)
    f = lambda t: jnp.einsum('gts,gsl->gtl', tb, t, preferred_element_type=F32)
    return f(x0) + f(x1) + f(x2)


def _rwkv_kernel(r_ref, lw_ref, k_ref, v_ref, kk_ref, b_ref, g_ref, lnw_ref, lnb_ref, rk_ref, tri_ref, e_ref,
                 o_ref, st_sc, sall_sc, kc_sc, nc_sc, pc_sc, *, hg, G):
    C = RWKV_CHUNK
    LW = hg * RWKV_HEAD
    RW = hg * C
    c_idx = pl.program_id(1)

    @pl.when(c_idx == 0)
    def _():
        st_sc[...] = jnp.zeros(st_sc.shape, F32)

    lw3 = lw_ref[...].reshape(G, C, LW)
    cum3 = _bsplit_cumsum(tri_ref[...], lw3)
    clast = cum3[:, C - 1:C, :]
    r3 = r_ref[...].reshape(G, C, LW)
    k3 = k_ref[...].reshape(G, C, LW)
    v3 = v_ref[...].reshape(G, C, LW)
    kk3 = kk_ref[...].reshape(G, C, LW)
    b3 = b_ref[...].reshape(G, C, LW)
    einv = jnp.exp(-cum3)
    etail = jnp.exp(clast - cum3)
    rt = r3 * jnp.exp(cum3)
    at = -kk3 * jnp.exp(cum3 - lw3)
    bt = b3 * einv
    kt = k3 * einv
    bh = b3 * etail
    kh = k3 * etail

    lane_head = lax.broadcasted_iota(jnp.int32, (1, 1, LW), 2) // RWKV_HEAD

    def stack(x):
        return jnp.concatenate([jnp.where(lane_head == h, x, 0.0) for h in range(hg)], axis=1)

    a2, b2, k2, r2, v2, bh2, kh2 = (stack(t).astype(BF16) for t in (at, bt, kt, rt, v3, bh, kh))
    ri = lax.broadcasted_iota(jnp.int32, (1, RW, RW), 1)
    ci = lax.broadcasted_iota(jnp.int32, (1, RW, RW), 2)
    same = (ri // C) == (ci // C)
    strict = same & (ci < ri)
    incl = same & (ci <= ri)
    eye = (ri == ci).astype(F32)

    bk2 = jnp.concatenate([b2, k2], axis=1)
    a_bk = _bmm_nt(a2, bk2)
    r_bk = _bmm_nt(r2, bk2)
    lmat = jnp.where(strict, a_bk[:, :, :RW], 0.0)
    akm = jnp.where(strict, a_bk[:, :, RW:], 0.0)
    rbm = jnp.where(incl, r_bk[:, :, :RW], 0.0)
    rkm = jnp.where(incl, r_bk[:, :, RW:], 0.0)
    tinv = eye + lmat
    pw = lmat
    for _ in range(int(np.log2(C)) - 1):
        pw = _bmm(pw, pw)
        tinv = tinv + _bmm(tinv, pw)
    wm = _bmm(tinv, a2)
    z = _bmm(tinv, _bmm(akm, v2))
    qm = r2.astype(F32) + _bmm(rbm, wm)
    y0 = _bmm(rkm, v2) + _bmm(rbm, z)
    kc_sc[...] = _bmm(jnp.swapaxes(wm, 1, 2), bh2)
    nc_sc[...] = _bmm(jnp.swapaxes(z, 1, 2), bh2) + _bmm(jnp.swapaxes(v2, 1, 2), kh2)
    pc_sc[...] = jnp.exp(clast)

    def chunk(c, carry):
        st = st_sc[...]
        sall_sc[c] = st
        st_sc[...] = st * pc_sc[c] + _dot(st, kc_sc[c]) + nc_sc[c]
        return carry

    lax.fori_loop(0, G, chunk, 0, unroll=True)

    y2 = _bmm_nt(qm, sall_sc[...]) + y0
    y = y2[:, 0:C]
    for h in range(1, hg):
        y = y + y2[:, h * C:(h + 1) * C]

    y = y.reshape(G * C, LW)
    inv_n = 1.0 / RWKV_HEAD
    mu = _seg_sum(y, e_ref) * inv_n
    d = y - mu
    var = _seg_sum(d * d, e_ref) * inv_n
    yn = d * lax.rsqrt(var + RWKV_GN_EPS) * lnw_ref[...] + lnb_ref[...]
    bonus = _seg_sum(r_ref[...] * k_ref[...] * rk_ref[...], e_ref) * v_ref[...]
    o_ref[...] = ((yn + bonus) * g_ref[...]).astype(o_ref.dtype)


def _rwkv_mix(r, lw, k, v, kkn, b, g, ln_w, ln_b, r_k, *, hg=2, G=8):
    S = r.shape[0]
    C = RWKV_CHUNK
    G = min(G, S // C)
    TB = G * C
    LW = hg * RWKV_HEAD
    ngrp = RWKV_WIDTH // LW
    t = np.arange(C)
    tri = jnp.asarray((t[None, :] <= t[:, None]).astype(np.float32), BF16)
    l = np.arange(LW)
    e = jnp.asarray((l[:, None] // RWKV_HEAD == l[None, :] // RWKV_HEAD).astype(np.float32), BF16)
    blk = pl.BlockSpec((TB, LW), lambda gi, c: (c, gi))
    row = pl.BlockSpec((1, LW), lambda gi, c: (0, gi))
    sq = lambda: pltpu.VMEM((G, LW, LW), F32)
    return pl.pallas_call(
        functools.partial(_rwkv_kernel, hg=hg, G=G),
        out_shape=jax.ShapeDtypeStruct((S, RWKV_WIDTH), BF16),
        grid=(ngrp, S // TB),
        in_specs=[blk] * 7 + [row] * 3 + [pl.BlockSpec((C, C), lambda gi, c: (0, 0)),
                                          pl.BlockSpec((LW, LW), lambda gi, c: (0, 0))],
        out_specs=blk,
        scratch_shapes=[pltpu.VMEM((LW, LW), F32), sq(), sq(), sq(), pltpu.VMEM((G, 1, LW), F32)],
        compiler_params=_cparams(("parallel", "arbitrary")),
        name="rwkv_mix",
    )(r, lw, k, v, kkn, b, g, ln_w, ln_b, r_k, tri, e)


def _outproj_kernel(h_ref, a_ref, b_ref, wa_ref, wb_ref, o_ref):
    o_ref[...] = (h_ref[...] + jnp.dot(a_ref[...], wa_ref[...], preferred_element_type=F32)
                  + jnp.dot(b_ref[...], wb_ref[...], preferred_element_type=F32))


def _outproj(h, a, b, wa, wb, *, tm=512):
    S, D = h.shape
    tm = min(tm, S)
    Ka, Kb = a.shape[1], b.shape[1]
    return pl.pallas_call(
        _outproj_kernel,
        out_shape=jax.ShapeDtypeStruct((S, D), F32),
        grid=(S // tm,),
        in_specs=[pl.BlockSpec((tm, D), lambda i: (i, 0)),
                  pl.BlockSpec((tm, Ka), lambda i: (i, 0)),
                  pl.BlockSpec((tm, Kb), lambda i: (i, 0)),
                  pl.BlockSpec((Ka, D), lambda i: (0, 0)),
                  pl.BlockSpec((Kb, D), lambda i: (0, 0))],
        out_specs=pl.BlockSpec((tm, D), lambda i: (i, 0)),
        compiler_params=_cparams(("parallel",)),
        name="outproj",
    )(h, a, b, wa, wb)


def _pad_cols(w, width):
    return jnp.pad(w, ((0, 0), (0, width - w.shape[1])))


def _rope_gap_layout(w):
    z = jnp.zeros(w.shape[:-1] + (32,), w.dtype)
    return jnp.concatenate([w[..., :32], z, w[..., 32:], z], axis=-1)


def _even_mixer(h, hn, tabs, w_in, q_norm, w_uq, kv_norm, w_ukv, q_head_norm, k_head_norm,
                mu, w0, w2, a0, a2, g2, k_k, k_a, r_k, ln_w, ln_b, w_out):
    cos1, sin1 = tabs[0], tabs[1]
    W = RWKV_WIDTH
    o_cq = 0
    o_ckv = MLA_Q_LORA
    o_kr = o_ckv + MLA_KV_LORA
    o_rw = o_kr + MLA_ROPE
    rw = w_in[:, o_rw:]
    mu_r = mu[None, :]

    def rwkv_layout(t):
        return jnp.concatenate([t[:, :3 * W], _pad_cols(t[:, 3 * W:3 * W + RWKV_W_LORA], LANES),
                                _pad_cols(t[:, 3 * W + RWKV_W_LORA:3 * W + RWKV_W_LORA + RWKV_A_LORA], LANES),
                                _pad_cols(t[:, 3 * W + RWKV_W_LORA + RWKV_A_LORA:], 2 * LANES)], axis=1)

    w_in_p = jnp.concatenate([rwkv_layout(rw), w_in[:, o_cq:o_ckv], w_in[:, o_ckv:o_kr],
                              _rope_gap_layout(w_in[:, o_kr:o_rw])], axis=1)
    w_in_p = _pad_cols(w_in_p, 4608).astype(BF16)
    p = _proj(hn, w_in_p, tn=1536)

    wq = w_uq.reshape(MLA_Q_LORA, MLA_HEADS, MLA_QK)
    wq = jnp.concatenate([wq[..., :MLA_NOPE], _rope_gap_layout(wq[..., MLA_NOPE:])], axis=-1)
    wq = wq.reshape(MLA_Q_LORA, MLA_HEADS * MLA_QK_PAD).astype(BF16)
    scale = MLA_QK ** -0.5 * LOG2E
    hg_q = jnp.concatenate([q_head_norm[:MLA_NOPE], _rope_gap_layout(q_head_norm[MLA_NOPE:])])[None, :] * scale
    gn = k_head_norm[None, :MLA_NOPE]
    gr = _rope_gap_layout(k_head_norm[MLA_NOPE:])[None, :]
    q, kmat, vmat = _mla_qkv(p, q_norm[None, :], kv_norm[None, :], wq, w_ukv.astype(BF16), hg_q, gn, gr, cos1, sin1)
    o_a = _mla_attention(q, kmat, vmat)

    l = np.arange(W)
    e = jnp.asarray((l[:, None] // RWKV_HEAD == l[None, :] // RWKV_HEAD).astype(np.float32), BF16)
    w2p = jnp.pad(w2, ((0, LANES - RWKV_W_LORA), (0, 0))).astype(BF16)
    a2p = jnp.pad(a2, ((0, LANES - RWKV_A_LORA), (0, 0))).astype(BF16)
    g2p = jnp.pad(g2, ((0, 2 * LANES - RWKV_G_LORA), (0, 0))).astype(BF16)
    r, lw, k, v, kkn, b, g = _rwkv_prep(p, rwkv_layout(mu_r), w0[None, :], w2p, a0[None, :], a2p, g2p,
                                         k_k[None, :], k_a[None, :], e)
    o_b = _rwkv_mix(r, lw, k, v, kkn, b, g, ln_w[None, :], ln_b[None, :], r_k.reshape(1, W))
    wo = w_out.astype(BF16)
    return _outproj(h, o_a, o_b, wo[:MLA_HEADS * MLA_V], wo[MLA_HEADS * MLA_V:])


def _swa_qkv_kernel(x_ref, w_ref, hg_ref, cos_ref, sin_ref, o_ref):
    n = pl.program_id(0)
    acc = jnp.dot(x_ref[...], w_ref[...], preferred_element_type=F32)

    @pl.when(n < 2)
    def _():
        cos = cos_ref[...]
        sin = sin_ref[...]
        for c in range(acc.shape[1] // LANES):
            sl = slice(c * LANES, (c + 1) * LANES)
            o_ref[:, sl] = _rope_apply(_rms(acc[:, sl], hg_ref[:, sl]), cos, sin).astype(o_ref.dtype)

    @pl.when(n == 2)
    def _():
        o_ref[...] = acc.astype(o_ref.dtype)


def _swa_qkv(x, w, hg, cos, sin, *, tm=512):
    S, K = x.shape
    tn = SWA_HEADS * SWA_DIM
    tm = min(tm, S)
    tab = pl.BlockSpec((tm, LANES), lambda n, i: (i, 0))
    return pl.pallas_call(
        _swa_qkv_kernel,
        out_shape=jax.ShapeDtypeStruct((S, 3 * tn), BF16),
        grid=(3, S // tm),
        in_specs=[pl.BlockSpec((tm, K), lambda n, i: (i, 0)),
                  pl.BlockSpec((K, tn), lambda n, i: (0, n)),
                  pl.BlockSpec((1, tn), lambda n, i: (0, jnp.minimum(n, 1))),
                  tab, tab],
        out_specs=pl.BlockSpec((tm, tn), lambda n, i: (i, n)),
        compiler_params=_cparams(("parallel", "parallel")),
        name="swa_qkv",
    )(x, w, hg, cos, sin)


def _dilated_bias(t, nrel):
    row = np.arange(t)[:, None]
    col = np.arange(t)[None, :]
    out = np.empty((nrel, t, t), np.float32)
    for r in range(nrel):
        delta = r * t + row - col
        cnt = np.zeros((t, t), np.float64)
        for window, dilation in SWA_PATTERNS:
            cnt += (delta >= 0) & (delta <= window) & (delta % dilation == 0)
        with np.errstate(divide="ignore"):
            out[r] = np.log2(cnt)
    return out


def _dilated_kernel(q_ref, k_ref, v_ref, bias_ref, o_ref, m_sc, acc_sc, *, sb, nrel, live):
    qi = pl.program_id(1)
    r = pl.program_id(2)
    t = q_ref.shape[0]

    @pl.when(r == 0)
    def _():
        _flash_init(m_sc, acc_sc)

    for rr in range(nrel):
        @pl.when((r == rr) & (qi - r >= 0))
        def _(rr=rr):
            v1 = _with_ones(v_ref[...])
            for rb in range(t // sb):
                rows = slice(rb * sb, (rb + 1) * sb)
                lo, hi = live[rr][rb]
                s = _dot_nt(q_ref[rows, :], k_ref[lo:hi, :]) + bias_ref[rr, rows, lo:hi]
                _flash_rows(s, v1[lo:hi, :], rows, m_sc, acc_sc)

    @pl.when(r == nrel - 1)
    def _():
        _flash_finish(acc_sc, o_ref)


def _dilated_attention(qkv, *, t=1024, sb=256):
    S = qkv.shape[0]
    t = min(t, S)
    sb = min(sb, t)
    max_window = max(w for w, _ in SWA_PATTERNS)
    nrel = min(-(-max_window // t) + 1, S // t)
    H = SWA_HEADS
    bias = _dilated_bias(t, nrel)
    live = []
    for r in range(nrel):
        per_rb = []
        for rb in range(t // sb):
            cols = np.nonzero(np.isfinite(bias[r, rb * sb:(rb + 1) * sb]).any(axis=0))[0]
            lo = int(cols.min()) // LANES * LANES
            hi = -(-(int(cols.max()) + 1) // LANES) * LANES
            per_rb.append((lo, hi))
        live.append(per_rb)
    stat = pltpu.VMEM((t, LANES), F32)
    return pl.pallas_call(
        functools.partial(_dilated_kernel, sb=sb, nrel=nrel, live=live),
        out_shape=jax.ShapeDtypeStruct((S, H * SWA_DIM), BF16),
        grid=(H, S // t, nrel),
        in_specs=[pl.BlockSpec((t, SWA_DIM), lambda h, i, r: (i, h)),
                  pl.BlockSpec((t, SWA_DIM), lambda h, i, r: (jnp.maximum(i - r, 0), H + h)),
                  pl.BlockSpec((t, SWA_DIM), lambda h, i, r: (jnp.maximum(i - r, 0), 2 * H + h)),
                  pl.BlockSpec((nrel, t, t), lambda h, i, r: (0, 0, 0))],
        out_specs=pl.BlockSpec((t, SWA_DIM), lambda h, i, r: (i, h)),
        scratch_shapes=[stat, pltpu.VMEM((t, 2 * SWA_DIM), F32)],
        compiler_params=_cparams(("parallel", "parallel", "arbitrary")),
        name="dilated_attention",
    )(qkv, qkv, qkv, jnp.asarray(bias))


def _gla_kernel(q_ref, k_ref, v_ref, glr_ref, wgu_ref, bg_ref, rd_ref, gn_ref, tri_ref, o_ref,
                st_sc, qt_sc, gkv_sc, eb_sc, oi_sc, *, G):
    C = GLA_CHUNK
    SB = GLA_SUB
    DK = GLA_DK
    DV = GLA_DV
    c_idx = pl.program_id(1)

    @pl.when(c_idx == 0)
    def _():
        st_sc[...] = jnp.zeros(st_sc.shape, F32)

    z = _dot(glr_ref[...], wgu_ref[...]) + bg_ref[...]
    lg = -_softplus(-z) * (1.0 / GLA_NORMALIZER)
    bc = _dot_split(tri_ref[...], lg).reshape(G, C, DK)
    q = (q_ref[...] * (DK ** -0.5)).reshape(G, C, DK)
    k = k_ref[...].reshape(G, C, DK)
    v = v_ref[...].reshape(G, C, DV)
    blast = bc[:, C - 1:C, :]
    qt_sc[...] = q * jnp.exp(bc)
    khat = k * jnp.exp(blast - bc)
    gkv_sc[...] = _bmm(jnp.swapaxes(v, 1, 2), khat)
    eb_sc[...] = jnp.exp(blast)

    ti = lax.broadcasted_iota(jnp.int32, (1, SB, SB, 1), 1)
    si = lax.broadcasted_iota(jnp.int32, (1, SB, SB, 1), 2)
    causal = si <= ti
    for sb in range(C // SB):
        lo = sb * SB
        qs = q[:, lo:lo + SB]
        ks = k[:, lo:lo + SB]
        bs = bc[:, lo:lo + SB]
        rel = bs[:, :, None, :] - bs[:, None, :, :]
        dec = jnp.exp(jnp.where(causal, rel, -jnp.inf))
        att = jnp.sum(qs[:, :, None, :] * ks[:, None, :, :] * dec, axis=-1)
        o_sb = _bmm(att, v[:, lo:lo + SB])
        if sb > 0:
            bm = bc[:, lo - 1:lo]
            qsc = qs * jnp.exp(bs - bm)
            ksc = k[:, :lo] * jnp.exp(bm - bc[:, :lo])
            o_sb = o_sb + _bmm(_bmm_nt(qsc, ksc), v[:, :lo])
        oi_sc[:, lo:lo + SB, :] = o_sb

    def chunk(c, carry):
        st = st_sc[...]
        oi_sc[c] = oi_sc[c] + _dot_nt(qt_sc[c], st)
        st_sc[...] = st * eb_sc[c] + gkv_sc[c]
        return carry

    lax.fori_loop(0, G, chunk, 0)

    o = oi_sc[...].reshape(G * C, DV)
    rd = rd_ref[...]
    o_ref[...] = (_rms(o, gn_ref[...]) * (rd * _sigmoid(rd))).astype(o_ref.dtype)


def _gla(pb, wgu, bg, gn, *, G=8):
    S = pb.shape[0]
    C = GLA_CHUNK
    G = min(G, S // C)
    TB = G * C
    H, DK, DV = GLA_HEADS, GLA_DK, GLA_DV
    t = np.arange(TB)
    tri = jnp.asarray(((t[:, None] // C == t[None, :] // C) & (t[None, :] <= t[:, None])).astype(np.float32), BF16)
    glr_block = (2 * H * DK + 2 * H * DV) // LANES
    return pl.pallas_call(
        functools.partial(_gla_kernel, G=G),
        out_shape=jax.ShapeDtypeStruct((S, H * DV), BF16),
        grid=(H, S // TB),
        in_specs=[pl.BlockSpec((TB, DK), lambda h, c: (c, h)),
                  pl.BlockSpec((TB, DK), lambda h, c: (c, H + h)),
                  pl.BlockSpec((TB, DV), lambda h, c: (c, (2 * H * DK) // DV + h)),
                  pl.BlockSpec((TB, LANES), lambda h, c: (c, glr_block)),
                  pl.BlockSpec((LANES, DK), lambda h, c: (0, h)),
                  pl.BlockSpec((1, DK), lambda h, c: (0, h)),
                  pl.BlockSpec((TB, DV), lambda h, c: (c, (2 * H * DK + H * DV) // DV + h)),
                  pl.BlockSpec((1, DV), lambda h, c: (0, h)),
                  pl.BlockSpec((TB, TB), lambda h, c: (0, 0))],
        out_specs=pl.BlockSpec((TB, DV), lambda h, c: (c, h)),
        scratch_shapes=[pltpu.VMEM((DV, DK), F32), pltpu.VMEM((G, C, DK), F32), pltpu.VMEM((G, DV, DK), F32),
                        pltpu.VMEM((G, 1, DK), F32), pltpu.VMEM((G, C, DV), F32)],
        compiler_params=_cparams(("parallel", "arbitrary")),
        name="gla",
    )(pb, pb, pb, pb, wgu, bg, pb, gn, tri)


def _odd_mixer(h, hn, tabs, w_in, q_head_norm, k_head_norm, w_gate_up, b_gate, gla_norm, w_out):
    cos2, sin2 = tabs[2], tabs[3]
    nq = SWA_HEADS * SWA_DIM
    hg = jnp.concatenate([jnp.tile(q_head_norm * (SWA_DIM ** -0.5 * LOG2E), SWA_HEADS), jnp.tile(k_head_norm, SWA_HEADS)])[None, :]
    qkv = _swa_qkv(hn, w_in[:, :3 * nq].astype(BF16), hg, cos2, sin2)
    o_c = _dilated_attention(qkv)
    o = 3 * nq
    dk, dv = GLA_HEADS * GLA_DK, GLA_HEADS * GLA_DV
    wb = jnp.concatenate([w_in[:, o:o + 2 * dk + dv], w_in[:, o + 2 * dk + dv + GLA_LORA:],
                          _pad_cols(w_in[:, o + 2 * dk + dv:o + 2 * dk + dv + GLA_LORA], LANES)], axis=1)
    wb = _pad_cols(wb, 3584).astype(BF16)
    pb = _proj(hn, wb, tn=1792)
    wgu = jnp.pad(w_gate_up, ((0, LANES - GLA_LORA), (0, 0))).astype(BF16)
    o_d = _gla(pb, wgu, b_gate[None, :], gla_norm.reshape(1, dv))
    wo = w_out.astype(BF16)
    return _outproj(h, o_c, o_d, wo[:nq], wo[nq:])


def kernel(x, positions, ffn_norm, ffn_w_gate, ffn_w_up, ffn_w_down, mix_norm, mla_rwkv_w_in, mla_q_norm, mla_w_uq,
           mla_kv_norm, mla_w_ukv, mla_q_head_norm, mla_k_head_norm, rwkv_mu, rwkv_w0, rwkv_w2, rwkv_a0, rwkv_a2,
           rwkv_g2, rwkv_k_k, rwkv_k_a, rwkv_r_k, rwkv_ln_w, rwkv_ln_b, mla_rwkv_w_out, swa_gla_w_in,
           swa_q_head_norm, swa_k_head_norm, gla_w_gate_up, gla_b_gate, gla_norm, swa_gla_w_out):
    B, S, D = x.shape
    assert B == 1
    h = x.reshape(S, D)
    tabs = _rope_tables(positions.reshape(S))
    wg = ffn_w_gate.astype(BF16)
    wu = ffn_w_up.astype(BF16)
    wd = ffn_w_down.astype(BF16)
    depth = ffn_norm.shape[0]
    for layer in range(depth):
        i = layer // 2
        h, hn = _ffn(h, ffn_norm, wg, wu, wd, layer, 0, next_norm=mix_norm[layer])
        if layer % 2 == 0:
            h = _even_mixer(h, hn, tabs, mla_rwkv_w_in[i], mla_q_norm[i], mla_w_uq[i], mla_kv_norm[i],
                            mla_w_ukv[i], mla_q_head_norm[i], mla_k_head_norm[i], rwkv_mu[i], rwkv_w0[i],
                            rwkv_w2[i], rwkv_a0[i], rwkv_a2[i], rwkv_g2[i], rwkv_k_k[i], rwkv_k_a[i], rwkv_r_k[i],
                            rwkv_ln_w[i], rwkv_ln_b[i], mla_rwkv_w_out[i])
        else:
            h = _odd_mixer(h, hn, tabs, swa_gla_w_in[i], swa_q_head_norm[i], swa_k_head_norm[i],
                           gla_w_gate_up[i], gla_b_gate[i], gla_norm[i], swa_gla_w_out[i])
        h = _ffn(h, ffn_norm, wg, wu, wd, layer, 1)
    return h.reshape(B, S, D)
```

```python
import functools

import numpy as np
import jax
import jax.numpy as jnp
from jax import lax
from jax.experimental import pallas as pl
from jax.experimental.pallas import tpu as pltpu

F32 = jnp.float32
BF16 = jnp.bfloat16

LANES = 128
VMEM_LIMIT = 56 * 1024 * 1024

D_MODEL = 2048
D_FF = 5632
MACARON_WEIGHT = 0.5
NORM_EPS = 1e-6
ROPE_THETA = 10000.0
LOG2E = float(np.log2(np.e))
EXP_NEG_HALF = float(np.exp(-0.5))

MLA_HEADS = 8
MLA_NOPE = 128
MLA_ROPE = 64
MLA_QK = MLA_NOPE + MLA_ROPE
MLA_QK_PAD = 256
MLA_V = 128
MLA_Q_LORA = 512
MLA_KV_LORA = 256

RWKV_HEAD = 64
RWKV_WIDTH = 1024
RWKV_HEADS = 16
RWKV_W_LORA = 64
RWKV_A_LORA = 64
RWKV_G_LORA = 160
RWKV_GN_EPS = 64e-5
RWKV_SEC = 3584
RWKV_CHUNK = 64

SWA_HEADS = 8
SWA_DIM = 128
SWA_PATTERNS = ((128, 1), (512, 4), (2048, 16))

GLA_HEADS = 4
GLA_DK = 128
GLA_DV = 256
GLA_LORA = 16
GLA_NORMALIZER = 16.0
GLA_CHUNK = 64
GLA_SUB = 16


def _cparams(sem):
    return pltpu.CompilerParams(dimension_semantics=sem, vmem_limit_bytes=VMEM_LIMIT)


def _rms(x, g):
    return x * lax.rsqrt(jnp.mean(x * x, axis=-1, keepdims=True) + NORM_EPS) * g


def _dot(a, b):
    return jnp.dot(a.astype(BF16), b.astype(BF16), preferred_element_type=F32)


def _dot_nt(a, b):
    return lax.dot_general(a.astype(BF16), b.astype(BF16), (((1,), (1,)), ((), ())),
                           preferred_element_type=F32)


def _split_dot(a, b):
    a0 = a.astype(BF16)
    r1 = a - a0.astype(F32)
    a1 = r1.astype(BF16)
    a2 = (r1 - a1.astype(F32)).astype(BF16)
    b = b.astype(BF16)
    return (jnp.dot(a0, b, preferred_element_type=F32) + jnp.dot(a1, b, preferred_element_type=F32)
            + jnp.dot(a2, b, preferred_element_type=F32))


def _sigmoid(x):
    return 0.5 * jnp.tanh(0.5 * x) + 0.5


def _softplus(x):
    return jnp.maximum(x, 0.0) + jnp.log(1.0 + jnp.exp(-jnp.abs(x)))


def _rope_apply(x, cos, sin_signed):
    return x * cos + pltpu.roll(x, 64, 1) * sin_signed


def _ffn_kernel(h_ref, g_ref, wg_ref, wu_ref, wd_ref, *rest, emit_norm):
    if emit_norm:
        g2_ref, o_ref, hn_ref, xn_ref = rest
    else:
        o_ref, xn_ref = rest
    j = pl.program_id(1)

    @pl.when(j == 0)
    def _():
        x = h_ref[...]
        xn_ref[...] = _rms(x, g_ref[...]).astype(BF16)
        o_ref[...] = x

    xn = xn_ref[...]
    gate = jnp.dot(xn, wg_ref[...], preferred_element_type=F32)
    up = jnp.dot(xn, wu_ref[...], preferred_element_type=F32)
    act = (gate * _sigmoid(gate) * up * MACARON_WEIGHT).astype(BF16)
    o_ref[...] += jnp.dot(act, wd_ref[...], preferred_element_type=F32)

    if emit_norm:
        @pl.when(j == pl.num_programs(1) - 1)
        def _():
            hn_ref[...] = _rms(o_ref[...], g2_ref[...]).astype(hn_ref.dtype)


def _ffn(h, g, wg, wu, wd, layer, k, next_norm=None, *, tm=512, tf=512):
    S, D = h.shape
    F = wg.shape[-1]
    tm = min(tm, S)
    emit_norm = next_norm is not None
    wspec = pl.BlockSpec((None, None, D, tf), lambda i, j: (layer, k, 0, j))
    row = pl.BlockSpec((1, D), lambda i, j: (0, 0))
    tile = pl.BlockSpec((tm, D), lambda i, j: (i, 0))
    in_specs = [tile, row, wspec, wspec, pl.BlockSpec((None, None, tf, D), lambda i, j: (layer, k, j, 0))]
    args = [h, g[layer, k][None, :], wg, wu, wd]
    out_shape = jax.ShapeDtypeStruct((S, D), F32)
    out_specs = tile
    if emit_norm:
        in_specs.append(row)
        args.append(next_norm[None, :])
        out_shape = (out_shape, jax.ShapeDtypeStruct((S, D), BF16))
        out_specs = (tile, tile)
    return pl.pallas_call(
        functools.partial(_ffn_kernel, emit_norm=emit_norm),
        out_shape=out_shape,
        grid=(S // tm, F // tf),
        in_specs=in_specs,
        out_specs=out_specs,
        scratch_shapes=[pltpu.VMEM((tm, D), BF16)],
        compiler_params=_cparams(("parallel", "arbitrary")),
        name="ffn",
    )(*args)


def _rope_tab_kernel(pos_ref, f1_ref, s1_ref, f2_ref, s2_ref, c1_ref, n1_ref, c2_ref, n2_ref):
    pos = pos_ref[...].astype(F32)
    a1 = pos * f1_ref[...]
    c1_ref[...] = jnp.cos(a1)
    n1_ref[...] = jnp.sin(a1) * s1_ref[...]
    a2 = pos * f2_ref[...]
    c2_ref[...] = jnp.cos(a2)
    n2_ref[...] = jnp.sin(a2) * s2_ref[...]


def _rope_tables(positions, *, tm=1024):
    S = positions.shape[0]
    tm = min(tm, S)
    f32half = ROPE_THETA ** (-jnp.arange(MLA_ROPE // 2, dtype=F32) / (MLA_ROPE // 2))
    z = jnp.zeros((32,), F32)
    f1 = jnp.concatenate([f32half, z, f32half, z])[None]
    f64half = ROPE_THETA ** (-jnp.arange(SWA_DIM // 2, dtype=F32) / (SWA_DIM // 2))
    f2 = jnp.concatenate([f64half, f64half])[None]
    sign = jnp.concatenate([-jnp.ones((64,), F32), jnp.ones((64,), F32)])[None]
    row = pl.BlockSpec((1, LANES), lambda i: (0, 0))
    tab = pl.BlockSpec((tm, LANES), lambda i: (i, 0))
    shp = jax.ShapeDtypeStruct((S, LANES), F32)
    return pl.pallas_call(
        _rope_tab_kernel,
        out_shape=(shp, shp, shp, shp),
        grid=(S // tm,),
        in_specs=[pl.BlockSpec((tm, 1), lambda i: (i, 0)), row, row, row, row],
        out_specs=(tab, tab, tab, tab),
        compiler_params=_cparams(("parallel",)),
        name="rope_tables",
    )(positions.reshape(S, 1), f1, sign, f2, sign)


def _proj_kernel(x_ref, w_ref, o_ref):
    o_ref[...] = jnp.dot(x_ref[...], w_ref[...], preferred_element_type=F32).astype(o_ref.dtype)


def _proj(x, w, *, tm=512, tn, out_dtype=F32):
    S, K = x.shape
    N = w.shape[1]
    tm = min(tm, S)
    return pl.pallas_call(
        _proj_kernel,
        out_shape=jax.ShapeDtypeStruct((S, N), out_dtype),
        grid=(N // tn, S // tm),
        in_specs=[pl.BlockSpec((tm, K), lambda n, i: (i, 0)),
                  pl.BlockSpec((K, tn), lambda n, i: (0, n))],
        out_specs=pl.BlockSpec((tm, tn), lambda n, i: (i, n)),
        compiler_params=_cparams(("parallel", "parallel")),
        name="proj",
    )(x, w)


def _mla_qkv_kernel(cq_ref, ckv_ref, kr_ref, qn_ref, kvn_ref, wq_ref, wkv_ref, hgq_ref, gn_ref, gr_ref,
                    cos_ref, sin_ref, q_ref, k_ref, v_ref):
    cos = cos_ref[...]
    sin = sin_ref[...]
    accq = _dot(_rms(cq_ref[...], qn_ref[...]), wq_ref[...])
    acckv = _dot(_rms(ckv_ref[...], kvn_ref[...]), wkv_ref[...])
    kr = kr_ref[...]
    kr_ss = jnp.sum(kr * kr, axis=-1, keepdims=True)
    inv_d = 1.0 / MLA_QK
    for h in range(MLA_HEADS):
        lo = h * MLA_QK_PAD
        a = accq[:, lo:lo + MLA_QK_PAD]
        y = a * lax.rsqrt(jnp.sum(a * a, axis=-1, keepdims=True) * inv_d + NORM_EPS) * hgq_ref[...]
        q_ref[:, lo:lo + LANES] = y[:, :LANES].astype(q_ref.dtype)
        q_ref[:, lo + LANES:lo + MLA_QK_PAD] = _rope_apply(y[:, LANES:], cos, sin).astype(q_ref.dtype)
        kn = acckv[:, lo:lo + LANES]
        rs = lax.rsqrt((jnp.sum(kn * kn, axis=-1, keepdims=True) + kr_ss) * inv_d + NORM_EPS)
        k_ref[:, lo:lo + LANES] = (kn * rs * gn_ref[...]).astype(k_ref.dtype)
        k_ref[:, lo + LANES:lo + MLA_QK_PAD] = _rope_apply(kr * rs * gr_ref[...], cos, sin).astype(k_ref.dtype)
        v_ref[:, h * MLA_V:(h + 1) * MLA_V] = acckv[:, lo + LANES:lo + MLA_QK_PAD].astype(v_ref.dtype)


def _mla_qkv(p, qn, kvn, wq, wkv, hgq, gn, gr, cos, sin, *, tm=256):
    S = p.shape[0]
    tm = min(tm, S)
    cq_block = RWKV_SEC // MLA_Q_LORA
    ckv_block = (RWKV_SEC + MLA_Q_LORA) // MLA_KV_LORA
    kr_block = (RWKV_SEC + MLA_Q_LORA + MLA_KV_LORA) // LANES
    NQ = MLA_HEADS * MLA_QK_PAD
    tab = pl.BlockSpec((tm, LANES), lambda i: (i, 0))
    full = lambda a: pl.BlockSpec(a.shape, lambda i: (0, 0))
    return pl.pallas_call(
        _mla_qkv_kernel,
        out_shape=(jax.ShapeDtypeStruct((S, NQ), BF16), jax.ShapeDtypeStruct((S, NQ), BF16),
                   jax.ShapeDtypeStruct((S, MLA_HEADS * MLA_V), BF16)),
        grid=(S // tm,),
        in_specs=[pl.BlockSpec((tm, MLA_Q_LORA), lambda i: (i, cq_block)),
                  pl.BlockSpec((tm, MLA_KV_LORA), lambda i: (i, ckv_block)),
                  pl.BlockSpec((tm, LANES), lambda i: (i, kr_block)),
                  full(qn), full(kvn), full(wq), full(wkv), full(hgq), full(gn), full(gr), tab, tab],
        out_specs=(pl.BlockSpec((tm, NQ), lambda i: (i, 0)), pl.BlockSpec((tm, NQ), lambda i: (i, 0)),
                   pl.BlockSpec((tm, MLA_HEADS * MLA_V), lambda i: (i, 0))),
        compiler_params=_cparams(("parallel",)),
        name="mla_qkv",
    )(p, p, p, qn, kvn, wq, wkv, hgq, gn, gr, cos, sin)


def _flash_rows(s, v1, rows, m_sc, acc_sc):
    m_prev = m_sc[rows, :]
    m_new = jnp.maximum(m_prev, jnp.max(s, axis=-1, keepdims=True))
    alpha = jnp.exp2(m_prev - m_new)
    pr = jnp.exp2((s - jnp.concatenate([m_new] * (s.shape[1] // LANES), axis=1)).astype(BF16))
    acc_sc[rows, :] = (jnp.concatenate([alpha, alpha], axis=1) * acc_sc[rows, :]
                       + jnp.dot(pr, v1, preferred_element_type=F32))
    m_sc[rows, :] = m_new


def _flash_init(m_sc, acc_sc):
    m_sc[...] = jnp.full(m_sc.shape, -jnp.inf, F32)
    acc_sc[...] = jnp.zeros(acc_sc.shape, F32)


def _with_ones(v):
    return jnp.concatenate([v, jnp.ones(v.shape, v.dtype)], axis=1)


def _flash_finish(acc_sc, o_ref):
    acc = acc_sc[...]
    d = acc.shape[1] // 2
    o_ref[...] = (acc[:, :d] / acc[:, d:]).astype(o_ref.dtype)


def _mla_attn_kernel(qi_ref, ki_ref, q_ref, k_ref, v_ref, o_ref, m_sc, acc_sc, *, sb):
    p = pl.program_id(1)
    qi = qi_ref[p]
    ki = ki_ref[p]
    t = q_ref.shape[0]

    @pl.when(ki == 0)
    def _():
        _flash_init(m_sc, acc_sc)

    @pl.when(ki < qi)
    def _():
        v1 = _with_ones(v_ref[...])
        for rb in range(t // sb):
            rows = slice(rb * sb, (rb + 1) * sb)
            s = _dot_nt(q_ref[rows, :], k_ref[...])
            _flash_rows(s, v1, rows, m_sc, acc_sc)

    @pl.when(ki == qi)
    def _():
        v1 = _with_ones(v_ref[...])
        for rb in range(t // sb):
            rows = slice(rb * sb, (rb + 1) * sb)
            nc = (rb + 1) * sb
            s = _dot_nt(q_ref[rows, :], k_ref[0:nc, :])
            row = lax.broadcasted_iota(jnp.int32, s.shape, 0) + rb * sb
            col = lax.broadcasted_iota(jnp.int32, s.shape, 1)
            s = jnp.where(col <= row, s, -jnp.inf)
            _flash_rows(s, v1[0:nc, :], rows, m_sc, acc_sc)
        _flash_finish(acc_sc, o_ref)


def _mla_attention(q, k, v, *, t=2048, sb=256):
    S = q.shape[0]
    t = min(t, S)
    sb = min(sb, t)
    nb = S // t
    pairs = [(a, b) for a in range(nb) for b in range(a + 1)]
    qi_tab = jnp.asarray(np.array([a for a, _ in pairs], np.int32))
    ki_tab = jnp.asarray(np.array([b for _, b in pairs], np.int32))
    stat = pltpu.VMEM((t, LANES), F32)
    grid_spec = pltpu.PrefetchScalarGridSpec(
        num_scalar_prefetch=2,
        grid=(MLA_HEADS, len(pairs)),
        in_specs=[pl.BlockSpec((t, MLA_QK_PAD), lambda h, p, qt, kt: (qt[p], h)),
                  pl.BlockSpec((t, MLA_QK_PAD), lambda h, p, qt, kt: (kt[p], h)),
                  pl.BlockSpec((t, MLA_V), lambda h, p, qt, kt: (kt[p], h))],
        out_specs=pl.BlockSpec((t, MLA_V), lambda h, p, qt, kt: (qt[p], h)),
        scratch_shapes=[stat, pltpu.VMEM((t, 2 * MLA_V), F32)],
    )
    return pl.pallas_call(
        functools.partial(_mla_attn_kernel, sb=sb),
        out_shape=jax.ShapeDtypeStruct((S, MLA_HEADS * MLA_V), BF16),
        grid_spec=grid_spec,
        compiler_params=_cparams(("parallel", "arbitrary")),
        name="mla_attention",
    )(qi_tab, ki_tab, q, k, v)


def _seg_sum(x, e_ref):
    return _split_dot(x, e_ref[...])


def _bmm(a, b):
    return jnp.einsum('gik,gkj->gij', a.astype(BF16), b.astype(BF16), preferred_element_type=F32)


def _bmm_nt(a, b):
    return jnp.einsum('gik,gjk->gij', a.astype(BF16), b.astype(BF16), preferred_element_type=F32)


def _bsplit_cumsum(tri, x):
    G = x.shape[0]
    tb = jnp.broadcast_to(tri.astype(BF16)[None], (G,) + tri.shape)
    x0 = x.astype(BF16)
    r1 = x - x0.astype(F32)
    x1 = r1.astype(BF16)
    x2 = (r1 - x1.astype(F32)).astype(BF16)
    f = lambda t: jnp.einsum('gts,gsl->gtl', tb, t, preferred_element_type=F32)
    return f(x0) + f(x1) + f(x2)


def _rwkv_kernel(pr_ref, pk_ref, pv_ref, px_ref, qr_ref, qk_ref, qv_ref, qx_ref, mur_ref, muk_ref, muv_ref, mux_ref,
                 w0_ref, w2_ref, a0_ref, a2_ref, g2_ref, kk_ref, ka_ref, lnw_ref, lnb_ref, rk_ref, tri_ref, e_ref,
                 o_ref, st_sc, sall_sc, kc_sc, nc_sc, pc_sc, *, hg, G):
    C = RWKV_CHUNK
    LW = hg * RWKV_HEAD
    RW = hg * C
    c_idx = pl.program_id(1)

    @pl.when(c_idx == 0)
    def _():
        st_sc[...] = jnp.zeros(st_sc.shape, F32)

    row = lax.broadcasted_iota(jnp.int32, (G * C, 1), 0)

    def shift_mix(x_ref, prev_ref, mu_ref):
        x = x_ref[...]
        last = jnp.where(c_idx == 0, 0.0, prev_ref[7:8, :])
        prev = jnp.where(row == 0, last, pltpu.roll(x, 1, 0))
        return x + (prev - x) * mu_ref[...]

    r = shift_mix(pr_ref, qr_ref, mur_ref)
    kb = shift_mix(pk_ref, qk_ref, muk_ref)
    v = shift_mix(pv_ref, qv_ref, muv_ref)
    xx = shift_mix(px_ref, qx_ref, mux_ref)
    lw = -EXP_NEG_HALF * _sigmoid(w0_ref[...] + _dot(jnp.tanh(xx[:, :LANES]), w2_ref[...]))
    a = _sigmoid(a0_ref[...] + _dot(xx[:, LANES:2 * LANES], a2_ref[...]))
    g = _dot(_sigmoid(xx[:, 2 * LANES:]), g2_ref[...])
    kk = kb * kk_ref[...]
    kkn = kk * lax.rsqrt(jnp.maximum(_seg_sum(kk * kk, e_ref), 1e-24))
    k = kb * (1.0 + (a - 1.0) * ka_ref[...])
    b = kkn * a

    lw3 = lw.reshape(G, C, LW)
    cum3 = _bsplit_cumsum(tri_ref[...], lw3)
    clast = cum3[:, C - 1:C, :]
    r3 = r.reshape(G, C, LW)
    k3 = k.reshape(G, C, LW)
    v3 = v.reshape(G, C, LW)
    kk3 = kkn.reshape(G, C, LW)
    b3 = b.reshape(G, C, LW)
    einv = jnp.exp(-cum3)
    pc = jnp.exp(clast)
    etail = pc * einv
    rt = r3 * jnp.exp(cum3)
    at = -kk3 * jnp.exp(cum3 - lw3)
    bt = b3 * einv
    kt = k3 * einv
    bh = b3 * etail
    kh = k3 * etail

    lane_head = lax.broadcasted_iota(jnp.int32, (1, 1, LW), 2) // RWKV_HEAD

    def stack(x):
        return jnp.concatenate([jnp.where(lane_head == h, x, 0.0) for h in range(hg)], axis=1)

    a2, b2, k2, r2, v2, bh2, kh2 = (stack(t).astype(BF16) for t in (at, bt, kt, rt, v3, bh, kh))
    ri = lax.broadcasted_iota(jnp.int32, (1, RW, RW), 1)
    ci = lax.broadcasted_iota(jnp.int32, (1, RW, RW), 2)
    same = (ri // C) == (ci // C)
    strict = same & (ci < ri)
    incl = same & (ci <= ri)
    eye = (ri == ci).astype(F32)

    bk2 = jnp.concatenate([b2, k2], axis=1)
    a_bk = _bmm_nt(a2, bk2)
    r_bk = _bmm_nt(r2, bk2)
    lmat = jnp.where(strict, a_bk[:, :, :RW], 0.0)
    akm = jnp.where(strict, a_bk[:, :, RW:], 0.0)
    rbm = jnp.where(incl, r_bk[:, :, :RW], 0.0)
    rkm = jnp.where(incl, r_bk[:, :, RW:], 0.0)
    tinv = eye + lmat
    pw = lmat
    for _ in range(int(np.log2(C)) - 1):
        pw = _bmm(pw, pw)
        tinv = tinv + _bmm(tinv, pw)
    wm = _bmm(tinv, a2)
    z = _bmm(tinv, _bmm(akm, v2))
    qm = r2.astype(F32) + _bmm(rbm, wm)
    y0 = _bmm(rkm, v2) + _bmm(rbm, z)
    kc_sc[...] = _bmm(jnp.swapaxes(wm, 1, 2), bh2)
    nc_sc[...] = _bmm(jnp.swapaxes(z, 1, 2), bh2) + _bmm(jnp.swapaxes(v2, 1, 2), kh2)
    pc_sc[...] = pc

    def chunk(c, carry):
        st = st_sc[...]
        sall_sc[c] = st
        st_sc[...] = st * pc_sc[c] + _dot(st, kc_sc[c]) + nc_sc[c]
        return carry

    lax.fori_loop(0, G, chunk, 0, unroll=True)

    y2 = _bmm_nt(qm, sall_sc[...]) + y0
    y = y2[:, 0:C]
    for h in range(1, hg):
        y = y + y2[:, h * C:(h + 1) * C]

    y = y.reshape(G * C, LW)
    inv_n = 1.0 / RWKV_HEAD
    mu = _seg_sum(y, e_ref) * inv_n
    d = y - mu
    var = _seg_sum(d * d, e_ref) * inv_n
    yn = d * lax.rsqrt(var + RWKV_GN_EPS) * lnw_ref[...] + lnb_ref[...]
    bonus = _seg_sum(r * k * rk_ref[...], e_ref) * v
    o_ref[...] = ((yn + bonus) * g).astype(o_ref.dtype)


def _rwkv_mix(p, mu, w0, w2, a0, a2, g2, k_k, k_a, ln_w, ln_b, r_k, *, hg=2, G=16):
    S = p.shape[0]
    C = RWKV_CHUNK
    G = min(G, S // C)
    TB = G * C
    LW = hg * RWKV_HEAD
    W = RWKV_WIDTH
    ngrp = W // LW
    XW = RWKV_SEC - 3 * W
    t = np.arange(C)
    tri = jnp.asarray((t[None, :] <= t[:, None]).astype(np.float32), BF16)
    l = np.arange(LW)
    e = jnp.asarray((l[:, None] // RWKV_HEAD == l[None, :] // RWKV_HEAD).astype(np.float32), BF16)

    def cols(width, first):
        blk = pl.BlockSpec((TB, width), lambda gi, c: (c, first(gi)))
        prev = pl.BlockSpec((8, width), lambda gi, c: (jnp.maximum(c * (TB // 8) - 1, 0), first(gi)))
        par = pl.BlockSpec((1, width), lambda gi, c: (0, first(gi)))
        return blk, prev, par

    rb, rp, rm = cols(LW, lambda gi: gi)
    kb, kp, km = cols(LW, lambda gi: ngrp + gi)
    vb, vp, vm = cols(LW, lambda gi: 2 * ngrp + gi)
    xb, xp, xm = cols(XW, lambda gi: 3 * W // XW)
    row = pl.BlockSpec((1, LW), lambda gi, c: (0, gi))
    lora = lambda n: pl.BlockSpec((n, LW), lambda gi, c: (0, gi))
    sq = lambda: pltpu.VMEM((G, LW, LW), F32)
    return pl.pallas_call(
        functools.partial(_rwkv_kernel, hg=hg, G=G),
        out_shape=jax.ShapeDtypeStruct((S, W), BF16),
        grid=(ngrp, S // TB),
        in_specs=[rb, kb, vb, xb, rp, kp, vp, xp, rm, km, vm, xm,
                  row, lora(LANES), row, lora(LANES), lora(2 * LANES), row, row, row, row, row,
                  pl.BlockSpec((C, C), lambda gi, c: (0, 0)), pl.BlockSpec((LW, LW), lambda gi, c: (0, 0))],
        out_specs=pl.BlockSpec((TB, LW), lambda gi, c: (c, gi)),
        scratch_shapes=[pltpu.VMEM((LW, LW), F32), sq(), sq(), sq(), pltpu.VMEM((G, 1, LW), F32)],
        compiler_params=_cparams(("parallel", "arbitrary")),
        name="rwkv_mix",
    )(p, p, p, p, p, p, p, p, mu, mu, mu, mu, w0, w2, a0, a2, g2, k_k, k_a, ln_w, ln_b, r_k, tri, e)


def _outproj_kernel(h_ref, a_ref, b_ref, wa_ref, wb_ref, o_ref):
    o_ref[...] = (h_ref[...] + jnp.dot(a_ref[...], wa_ref[...], preferred_element_type=F32)
                  + jnp.dot(b_ref[...], wb_ref[...], preferred_element_type=F32))


def _outproj(h, a, b, w, *, tm=512):
    S, D = h.shape
    tm = min(tm, S)
    Ka, Kb = a.shape[1], b.shape[1]
    assert Ka == Kb and w.shape[0] == Ka + Kb
    return pl.pallas_call(
        _outproj_kernel,
        out_shape=jax.ShapeDtypeStruct((S, D), F32),
        grid=(S // tm,),
        in_specs=[pl.BlockSpec((tm, D), lambda i: (i, 0)),
                  pl.BlockSpec((tm, Ka), lambda i: (i, 0)),
                  pl.BlockSpec((tm, Kb), lambda i: (i, 0)),
                  pl.BlockSpec((Ka, D), lambda i: (0, 0)),
                  pl.BlockSpec((Kb, D), lambda i: (1, 0))],
        out_specs=pl.BlockSpec((tm, D), lambda i: (i, 0)),
        compiler_params=_cparams(("parallel",)),
        name="outproj",
    )(h, a, b, w, w)


def _pad_cols(w, width):
    return jnp.pad(w, ((0, 0), (0, width - w.shape[1])))


def _rope_gap_layout(w):
    z = jnp.zeros(w.shape[:-1] + (32,), w.dtype)
    return jnp.concatenate([w[..., :32], z, w[..., 32:], z], axis=-1)


def _even_mixer(h, hn, tabs, w_in, q_norm, w_uq, kv_norm, w_ukv, q_head_norm, k_head_norm,
                mu, w0, w2, a0, a2, g2, k_k, k_a, r_k, ln_w, ln_b, w_out):
    cos1, sin1 = tabs[0], tabs[1]
    W = RWKV_WIDTH
    o_cq = 0
    o_ckv = MLA_Q_LORA
    o_kr = o_ckv + MLA_KV_LORA
    o_rw = o_kr + MLA_ROPE
    w_in = w_in.astype(BF16)
    rw = w_in[:, o_rw:]

    def rwkv_layout(t):
        return jnp.concatenate([t[:, :3 * W], _pad_cols(t[:, 3 * W:3 * W + RWKV_W_LORA], LANES),
                                _pad_cols(t[:, 3 * W + RWKV_W_LORA:3 * W + RWKV_W_LORA + RWKV_A_LORA], LANES),
                                _pad_cols(t[:, 3 * W + RWKV_W_LORA + RWKV_A_LORA:], 2 * LANES)], axis=1)

    w_in_p = jnp.concatenate([rwkv_layout(rw), w_in[:, o_cq:o_ckv], w_in[:, o_ckv:o_kr],
                              _rope_gap_layout(w_in[:, o_kr:o_rw])], axis=1)
    w_in_p = _pad_cols(w_in_p, 4608)
    p = _proj(hn, w_in_p, tn=1536)

    wq = w_uq.astype(BF16).reshape(MLA_Q_LORA, MLA_HEADS, MLA_QK)
    wq = jnp.concatenate([wq[..., :MLA_NOPE], _rope_gap_layout(wq[..., MLA_NOPE:])], axis=-1)
    wq = wq.reshape(MLA_Q_LORA, MLA_HEADS * MLA_QK_PAD)
    scale = MLA_QK ** -0.5 * LOG2E
    hg_q = jnp.concatenate([q_head_norm[:MLA_NOPE], _rope_gap_layout(q_head_norm[MLA_NOPE:])])[None, :] * scale
    gn = k_head_norm[None, :MLA_NOPE]
    gr = _rope_gap_layout(k_head_norm[MLA_NOPE:])[None, :]
    q, kmat, vmat = _mla_qkv(p, q_norm[None, :], kv_norm[None, :], wq, w_ukv.astype(BF16), hg_q, gn, gr, cos1, sin1)
    o_a = _mla_attention(q, kmat, vmat)

    w2p = jnp.pad(w2, ((0, LANES - RWKV_W_LORA), (0, 0))).astype(BF16)
    a2p = jnp.pad(a2, ((0, LANES - RWKV_A_LORA), (0, 0))).astype(BF16)
    g2p = jnp.pad(g2, ((0, 2 * LANES - RWKV_G_LORA), (0, 0))).astype(BF16)
    o_b = _rwkv_mix(p, rwkv_layout(mu[None, :]), w0[None, :], w2p, a0[None, :], a2p, g2p, k_k[None, :], k_a[None, :],
                    ln_w[None, :], ln_b[None, :], r_k.reshape(1, W))
    return _outproj(h, o_a, o_b, w_out.astype(BF16))


def _swa_qkv_kernel(x_ref, w_ref, hg_ref, cos_ref, sin_ref, o_ref):
    n = pl.program_id(0)
    acc = jnp.dot(x_ref[...], w_ref[...], preferred_element_type=F32)

    @pl.when(n < 2)
    def _():
        cos = cos_ref[...]
        sin = sin_ref[...]
        for c in range(acc.shape[1] // LANES):
            sl = slice(c * LANES, (c + 1) * LANES)
            o_ref[:, sl] = _rope_apply(_rms(acc[:, sl], hg_ref[:, sl]), cos, sin).astype(o_ref.dtype)

    @pl.when(n == 2)
    def _():
        o_ref[...] = acc.astype(o_ref.dtype)


def _swa_qkv(x, w, hg, cos, sin, *, tm=512):
    S, K = x.shape
    tn = SWA_HEADS * SWA_DIM
    tm = min(tm, S)
    tab = pl.BlockSpec((tm, LANES), lambda n, i: (i, 0))
    return pl.pallas_call(
        _swa_qkv_kernel,
        out_shape=jax.ShapeDtypeStruct((S, 3 * tn), BF16),
        grid=(3, S // tm),
        in_specs=[pl.BlockSpec((tm, K), lambda n, i: (i, 0)),
                  pl.BlockSpec((K, tn), lambda n, i: (0, n)),
                  pl.BlockSpec((1, tn), lambda n, i: (0, jnp.minimum(n, 1))),
                  tab, tab],
        out_specs=pl.BlockSpec((tm, tn), lambda n, i: (i, n)),
        compiler_params=_cparams(("parallel", "parallel")),
        name="swa_qkv",
    )(x, w, hg, cos, sin)


def _dilated_bias(t, nrel):
    row = np.arange(t)[:, None]
    col = np.arange(t)[None, :]
    out = np.empty((nrel, t, t), np.float32)
    for r in range(nrel):
        delta = r * t + row - col
        cnt = np.zeros((t, t), np.float64)
        for window, dilation in SWA_PATTERNS:
            cnt += (delta >= 0) & (delta <= window) & (delta % dilation == 0)
        with np.errstate(divide="ignore"):
            out[r] = np.log2(cnt)
    return out


def _dilated_kernel(q_ref, k_ref, v_ref, bias_ref, o_ref, m_sc, acc_sc, *, sb, nrel, live):
    qi = pl.program_id(1)
    r = pl.program_id(2)
    t = q_ref.shape[0]

    @pl.when(r == 0)
    def _():
        _flash_init(m_sc, acc_sc)

    for rr in range(nrel):
        @pl.when((r == rr) & (qi - r >= 0))
        def _(rr=rr):
            v1 = _with_ones(v_ref[...])
            for rb in range(t // sb):
                rows = slice(rb * sb, (rb + 1) * sb)
                lo, hi = live[rr][rb]
                s = _dot_nt(q_ref[rows, :], k_ref[lo:hi, :]) + bias_ref[rr, rows, lo:hi]
                _flash_rows(s, v1[lo:hi, :], rows, m_sc, acc_sc)

    @pl.when(r == nrel - 1)
    def _():
        _flash_finish(acc_sc, o_ref)


def _dilated_attention(qkv, *, t=1024, sb=256):
    S = qkv.shape[0]
    t = min(t, S)
    sb = min(sb, t)
    max_window = max(w for w, _ in SWA_PATTERNS)
    nrel = min(-(-max_window // t) + 1, S // t)
    H = SWA_HEADS
    bias = _dilated_bias(t, nrel)
    live = []
    for r in range(nrel):
        per_rb = []
        for rb in range(t // sb):
            cols = np.nonzero(np.isfinite(bias[r, rb * sb:(rb + 1) * sb]).any(axis=0))[0]
            lo = int(cols.min()) // LANES * LANES
            hi = -(-(int(cols.max()) + 1) // LANES) * LANES
            per_rb.append((lo, hi))
        live.append(per_rb)
    stat = pltpu.VMEM((t, LANES), F32)
    return pl.pallas_call(
        functools.partial(_dilated_kernel, sb=sb, nrel=nrel, live=live),
        out_shape=jax.ShapeDtypeStruct((S, H * SWA_DIM), BF16),
        grid=(H, S // t, nrel),
        in_specs=[pl.BlockSpec((t, SWA_DIM), lambda h, i, r: (i, h)),
                  pl.BlockSpec((t, SWA_DIM), lambda h, i, r: (jnp.maximum(i - r, 0), H + h)),
                  pl.BlockSpec((t, SWA_DIM), lambda h, i, r: (jnp.maximum(i - r, 0), 2 * H + h)),
                  pl.BlockSpec((nrel, t, t), lambda h, i, r: (0, 0, 0))],
        out_specs=pl.BlockSpec((t, SWA_DIM), lambda h, i, r: (i, h)),
        scratch_shapes=[stat, pltpu.VMEM((t, 2 * SWA_DIM), F32)],
        compiler_params=_cparams(("parallel", "parallel", "arbitrary")),
        name="dilated_attention",
    )(qkv, qkv, qkv, jnp.asarray(bias))


def _gla_kernel(q_ref, k_ref, v_ref, glr_ref, wgu_ref, bg_ref, rd_ref, gn_ref, tri_ref, o_ref,
                st_sc, qt_sc, gkv_sc, eb_sc, oi_sc, *, G):
    C = GLA_CHUNK
    SB = GLA_SUB
    DK = GLA_DK
    DV = GLA_DV
    c_idx = pl.program_id(1)

    @pl.when(c_idx == 0)
    def _():
        st_sc[...] = jnp.zeros(st_sc.shape, F32)

    z = _dot(glr_ref[...], wgu_ref[...]) + bg_ref[...]
    lg = -_softplus(-z) * (1.0 / GLA_NORMALIZER)
    bc = _bsplit_cumsum(tri_ref[...], lg.reshape(G, C, DK))
    q = (q_ref[...] * (DK ** -0.5)).reshape(G, C, DK)
    k = k_ref[...].reshape(G, C, DK)
    v = v_ref[...].reshape(G, C, DV)
    blast = bc[:, C - 1:C, :]
    qt_sc[...] = q * jnp.exp(bc)
    khat = k * jnp.exp(blast - bc)
    gkv_sc[...] = _bmm(jnp.swapaxes(v, 1, 2), khat)
    eb_sc[...] = jnp.exp(blast)

    ti = lax.broadcasted_iota(jnp.int32, (1, SB, SB, 1), 1)
    si = lax.broadcasted_iota(jnp.int32, (1, SB, SB, 1), 2)
    causal = si <= ti
    for sb in range(C // SB):
        lo = sb * SB
        qs = q[:, lo:lo + SB]
        ks = k[:, lo:lo + SB]
        bs = bc[:, lo:lo + SB]
        rel = bs[:, :, None, :] - bs[:, None, :, :]
        dec = jnp.exp(jnp.where(causal, rel, -jnp.inf))
        att = jnp.sum(qs[:, :, None, :] * ks[:, None, :, :] * dec, axis=-1)
        o_sb = _bmm(att, v[:, lo:lo + SB])
        if sb > 0:
            bm = bc[:, lo - 1:lo]
            qsc = qs * jnp.exp(bs - bm)
            ksc = k[:, :lo] * jnp.exp(bm - bc[:, :lo])
            o_sb = o_sb + _bmm(_bmm_nt(qsc, ksc), v[:, :lo])
        oi_sc[:, lo:lo + SB, :] = o_sb

    def chunk(c, carry):
        st = st_sc[...]
        oi_sc[c] = oi_sc[c] + _dot_nt(qt_sc[c], st)
        st_sc[...] = st * eb_sc[c] + gkv_sc[c]
        return carry

    lax.fori_loop(0, G, chunk, 0)

    o = oi_sc[...].reshape(G * C, DV)
    rd = rd_ref[...]
    o_ref[...] = (_rms(o, gn_ref[...]) * (rd * _sigmoid(rd))).astype(o_ref.dtype)


def _gla(pb, wgu, bg, gn, *, G=16):
    S = pb.shape[0]
    C = GLA_CHUNK
    G = min(G, S // C)
    TB = G * C
    H, DK, DV = GLA_HEADS, GLA_DK, GLA_DV
    t = np.arange(C)
    tri = jnp.asarray((t[None, :] <= t[:, None]).astype(np.float32), BF16)
    glr_block = (2 * H * DK + 2 * H * DV) // LANES
    return pl.pallas_call(
        functools.partial(_gla_kernel, G=G),
        out_shape=jax.ShapeDtypeStruct((S, H * DV), BF16),
        grid=(H, S // TB),
        in_specs=[pl.BlockSpec((TB, DK), lambda h, c: (c, h)),
                  pl.BlockSpec((TB, DK), lambda h, c: (c, H + h)),
                  pl.BlockSpec((TB, DV), lambda h, c: (c, (2 * H * DK) // DV + h)),
                  pl.BlockSpec((TB, LANES), lambda h, c: (c, glr_block)),
                  pl.BlockSpec((LANES, DK), lambda h, c: (0, h)),
                  pl.BlockSpec((1, DK), lambda h, c: (0, h)),
                  pl.BlockSpec((TB, DV), lambda h, c: (c, (2 * H * DK + H * DV) // DV + h)),
                  pl.BlockSpec((1, DV), lambda h, c: (0, h)),
                  pl.BlockSpec((C, C), lambda h, c: (0, 0))],
        out_specs=pl.BlockSpec((TB, DV), lambda h, c: (c, h)),
        scratch_shapes=[pltpu.VMEM((DV, DK), F32), pltpu.VMEM((G, C, DK), F32), pltpu.VMEM((G, DV, DK), F32),
                        pltpu.VMEM((G, 1, DK), F32), pltpu.VMEM((G, C, DV), F32)],
        compiler_params=_cparams(("parallel", "arbitrary")),
        name="gla",
    )(pb, pb, pb, pb, wgu, bg, pb, gn, tri)


def _odd_mixer(h, hn, tabs, w_in, q_head_norm, k_head_norm, w_gate_up, b_gate, gla_norm, w_out):
    cos2, sin2 = tabs[2], tabs[3]
    nq = SWA_HEADS * SWA_DIM
    hg = jnp.concatenate([jnp.tile(q_head_norm * (SWA_DIM ** -0.5 * LOG2E), SWA_HEADS), jnp.tile(k_head_norm, SWA_HEADS)])[None, :]
    w_in = w_in.astype(BF16)
    qkv = _swa_qkv(hn, w_in, hg, cos2, sin2)
    o_c = _dilated_attention(qkv)
    o = 3 * nq
    dk, dv = GLA_HEADS * GLA_DK, GLA_HEADS * GLA_DV
    wb = jnp.concatenate([w_in[:, o:o + 2 * dk + dv], w_in[:, o + 2 * dk + dv + GLA_LORA:],
                          _pad_cols(w_in[:, o + 2 * dk + dv:o + 2 * dk + dv + GLA_LORA], LANES)], axis=1)
    wb = _pad_cols(wb, 3584)
    pb = _proj(hn, wb, tn=1792)
    wgu = jnp.pad(w_gate_up, ((0, LANES - GLA_LORA), (0, 0))).astype(BF16)
    o_d = _gla(pb, wgu, b_gate[None, :], gla_norm.reshape(1, dv))
    return _outproj(h, o_c, o_d, w_out.astype(BF16))


def kernel(x, positions, ffn_norm, ffn_w_gate, ffn_w_up, ffn_w_down, mix_norm, mla_rwkv_w_in, mla_q_norm, mla_w_uq,
           mla_kv_norm, mla_w_ukv, mla_q_head_norm, mla_k_head_norm, rwkv_mu, rwkv_w0, rwkv_w2, rwkv_a0, rwkv_a2,
           rwkv_g2, rwkv_k_k, rwkv_k_a, rwkv_r_k, rwkv_ln_w, rwkv_ln_b, mla_rwkv_w_out, swa_gla_w_in,
           swa_q_head_norm, swa_k_head_norm, gla_w_gate_up, gla_b_gate, gla_norm, swa_gla_w_out):
    B, S, D = x.shape
    assert B == 1
    h = x.reshape(S, D)
    tabs = _rope_tables(positions.reshape(S))
    wg = ffn_w_gate.astype(BF16)
    wu = ffn_w_up.astype(BF16)
    wd = ffn_w_down.astype(BF16)
    depth = ffn_norm.shape[0]
    for layer in range(depth):
        i = layer // 2
        h, hn = _ffn(h, ffn_norm, wg, wu, wd, layer, 0, next_norm=mix_norm[layer])
        if layer % 2 == 0:
            h = _even_mixer(h, hn, tabs, mla_rwkv_w_in[i], mla_q_norm[i], mla_w_uq[i], mla_kv_norm[i],
                            mla_w_ukv[i], mla_q_head_norm[i], mla_k_head_norm[i], rwkv_mu[i], rwkv_w0[i],
                            rwkv_w2[i], rwkv_a0[i], rwkv_a2[i], rwkv_g2[i], rwkv_k_k[i], rwkv_k_a[i], rwkv_r_k[i],
                            rwkv_ln_w[i], rwkv_ln_b[i], mla_rwkv_w_out[i])
        else:
            h = _odd_mixer(h, hn, tabs, swa_gla_w_in[i], swa_q_head_norm[i], swa_k_head_norm[i],
                           gla_w_gate_up[i], gla_b_gate[i], gla_norm[i], swa_gla_w_out[i])
        h = _ffn(h, ffn_norm, wg, wu, wd, layer, 1)
    return h.reshape(B, S, D)
```

```python
import functools

import numpy as np
import jax
import jax.numpy as jnp
from jax import lax
from jax.experimental import pallas as pl
from jax.experimental.pallas import tpu as pltpu

F32 = jnp.float32
BF16 = jnp.bfloat16

LANES = 128
VMEM_LIMIT = 56 * 1024 * 1024

D_MODEL = 2048
D_FF = 5632
MACARON_WEIGHT = 0.5
NORM_EPS = 1e-6
ROPE_THETA = 10000.0
LOG2E = float(np.log2(np.e))
EXP_NEG_HALF = float(np.exp(-0.5))

MLA_HEADS = 8
MLA_NOPE = 128
MLA_ROPE = 64
MLA_QK = MLA_NOPE + MLA_ROPE
MLA_QK_PAD = 256
MLA_V = 128
MLA_Q_LORA = 512
MLA_KV_LORA = 256

RWKV_HEAD = 64
RWKV_WIDTH = 1024
RWKV_HEADS = 16
RWKV_W_LORA = 64
RWKV_A_LORA = 64
RWKV_G_LORA = 160
RWKV_GN_EPS = 64e-5
RWKV_SEC = 3584
RWKV_CHUNK = 64

SWA_HEADS = 8
SWA_DIM = 128
SWA_PATTERNS = ((128, 1), (512, 4), (2048, 16))

GLA_HEADS = 4
GLA_DK = 128
GLA_DV = 256
GLA_LORA = 16
GLA_NORMALIZER = 16.0
GLA_CHUNK = 64
GLA_SUB = 8


def _cparams(sem):
    return pltpu.CompilerParams(dimension_semantics=sem, vmem_limit_bytes=VMEM_LIMIT)


def _rms(x, g):
    return x * lax.rsqrt(jnp.mean(x * x, axis=-1, keepdims=True) + NORM_EPS) * g


def _dot(a, b):
    return jnp.dot(a.astype(BF16), b.astype(BF16), preferred_element_type=F32)


def _dot_nt(a, b):
    return lax.dot_general(a.astype(BF16), b.astype(BF16), (((1,), (1,)), ((), ())),
                           preferred_element_type=F32)


def _split_dot(a, b):
    a0 = a.astype(BF16)
    r1 = a - a0.astype(F32)
    a1 = r1.astype(BF16)
    a2 = (r1 - a1.astype(F32)).astype(BF16)
    b = b.astype(BF16)
    return (jnp.dot(a0, b, preferred_element_type=F32) + jnp.dot(a1, b, preferred_element_type=F32)
            + jnp.dot(a2, b, preferred_element_type=F32))


def _sigmoid(x):
    return 0.5 * jnp.tanh(0.5 * x) + 0.5


def _softplus(x):
    return jnp.maximum(x, 0.0) + jnp.log(1.0 + jnp.exp(-jnp.abs(x)))


def _rope_apply(x, cos, sin_signed):
    return x * cos + pltpu.roll(x, 64, 1) * sin_signed


def _ffn_kernel(h_ref, g_ref, wg_ref, wu_ref, wd_ref, *rest, emit_norm):
    if emit_norm:
        g2_ref, o_ref, hn_ref, xn_ref = rest
    else:
        o_ref, xn_ref = rest
    j = pl.program_id(1)

    @pl.when(j == 0)
    def _():
        x = h_ref[...]
        xn_ref[...] = _rms(x, g_ref[...]).astype(BF16)
        o_ref[...] = x

    xn = xn_ref[...]
    gate = jnp.dot(xn, wg_ref[...], preferred_element_type=F32)
    up = jnp.dot(xn, wu_ref[...], preferred_element_type=F32)
    act = (gate * _sigmoid(gate) * up * MACARON_WEIGHT).astype(BF16)
    o_ref[...] += jnp.dot(act, wd_ref[...], preferred_element_type=F32)

    if emit_norm:
        @pl.when(j == pl.num_programs(1) - 1)
        def _():
            hn_ref[...] = _rms(o_ref[...], g2_ref[...]).astype(hn_ref.dtype)


def _ffn(h, g, wg, wu, wd, layer, k, next_norm=None, *, tm=512, tf=512):
    S, D = h.shape
    F = wg.shape[-1]
    tm = min(tm, S)
    emit_norm = next_norm is not None
    wspec = pl.BlockSpec((None, None, D, tf), lambda i, j: (layer, k, 0, j))
    row = pl.BlockSpec((1, D), lambda i, j: (0, 0))
    tile = pl.BlockSpec((tm, D), lambda i, j: (i, 0))
    in_specs = [tile, row, wspec, wspec, pl.BlockSpec((None, None, tf, D), lambda i, j: (layer, k, j, 0))]
    args = [h, g[layer, k][None, :], wg, wu, wd]
    out_shape = jax.ShapeDtypeStruct((S, D), F32)
    out_specs = tile
    if emit_norm:
        in_specs.append(row)
        args.append(next_norm[None, :])
        out_shape = (out_shape, jax.ShapeDtypeStruct((S, D), BF16))
        out_specs = (tile, tile)
    return pl.pallas_call(
        functools.partial(_ffn_kernel, emit_norm=emit_norm),
        out_shape=out_shape,
        grid=(S // tm, F // tf),
        in_specs=in_specs,
        out_specs=out_specs,
        scratch_shapes=[pltpu.VMEM((tm, D), BF16)],
        compiler_params=_cparams(("parallel", "arbitrary")),
        name="ffn",
    )(*args)


def _rope_tab_kernel(pos_ref, f1_ref, s1_ref, f2_ref, s2_ref, c1_ref, n1_ref, c2_ref, n2_ref):
    pos = pos_ref[...].astype(F32)
    a1 = pos * f1_ref[...]
    c1_ref[...] = jnp.cos(a1)
    n1_ref[...] = jnp.sin(a1) * s1_ref[...]
    a2 = pos * f2_ref[...]
    c2_ref[...] = jnp.cos(a2)
    n2_ref[...] = jnp.sin(a2) * s2_ref[...]


def _rope_tables(positions, *, tm=1024):
    S = positions.shape[0]
    tm = min(tm, S)
    f32half = ROPE_THETA ** (-jnp.arange(MLA_ROPE // 2, dtype=F32) / (MLA_ROPE // 2))
    z = jnp.zeros((32,), F32)
    f1 = jnp.concatenate([f32half, z, f32half, z])[None]
    f64half = ROPE_THETA ** (-jnp.arange(SWA_DIM // 2, dtype=F32) / (SWA_DIM // 2))
    f2 = jnp.concatenate([f64half, f64half])[None]
    sign = jnp.concatenate([-jnp.ones((64,), F32), jnp.ones((64,), F32)])[None]
    row = pl.BlockSpec((1, LANES), lambda i: (0, 0))
    tab = pl.BlockSpec((tm, LANES), lambda i: (i, 0))
    shp = jax.ShapeDtypeStruct((S, LANES), F32)
    return pl.pallas_call(
        _rope_tab_kernel,
        out_shape=(shp, shp, shp, shp),
        grid=(S // tm,),
        in_specs=[pl.BlockSpec((tm, 1), lambda i: (i, 0)), row, row, row, row],
        out_specs=(tab, tab, tab, tab),
        compiler_params=_cparams(("parallel",)),
        name="rope_tables",
    )(positions.reshape(S, 1), f1, sign, f2, sign)


def _proj_kernel(x_ref, w_ref, o_ref):
    o_ref[...] = jnp.dot(x_ref[...], w_ref[...], preferred_element_type=F32).astype(o_ref.dtype)


def _proj(x, w, *, tm=512, tn, out_dtype=F32):
    S, K = x.shape
    N = w.shape[1]
    tm = min(tm, S)
    return pl.pallas_call(
        _proj_kernel,
        out_shape=jax.ShapeDtypeStruct((S, N), out_dtype),
        grid=(N // tn, S // tm),
        in_specs=[pl.BlockSpec((tm, K), lambda n, i: (i, 0)),
                  pl.BlockSpec((K, tn), lambda n, i: (0, n))],
        out_specs=pl.BlockSpec((tm, tn), lambda n, i: (i, n)),
        compiler_params=_cparams(("parallel", "parallel")),
        name="proj",
    )(x, w)


def _mla_qkv_kernel(cq_ref, ckv_ref, kr_ref, qn_ref, kvn_ref, wq_ref, wkv_ref, hgq_ref, gn_ref, gr_ref,
                    cos_ref, sin_ref, q_ref, k_ref, v_ref):
    cos = cos_ref[...]
    sin = sin_ref[...]
    xq = _rms(cq_ref[...], qn_ref[...]).astype(BF16)
    xkv = _rms(ckv_ref[...], kvn_ref[...]).astype(BF16)
    kr = kr_ref[...]
    kr_ss = jnp.sum(kr * kr, axis=-1, keepdims=True)
    inv_d = 1.0 / MLA_QK
    for h in range(MLA_HEADS):
        lo = h * MLA_QK_PAD
        a = jnp.dot(xq, wq_ref[:, lo:lo + MLA_QK_PAD], preferred_element_type=F32)
        y = a * lax.rsqrt(jnp.sum(a * a, axis=-1, keepdims=True) * inv_d + NORM_EPS) * hgq_ref[...]
        q_ref[:, lo:lo + LANES] = y[:, :LANES].astype(q_ref.dtype)
        q_ref[:, lo + LANES:lo + MLA_QK_PAD] = _rope_apply(y[:, LANES:], cos, sin).astype(q_ref.dtype)
        kv = jnp.dot(xkv, wkv_ref[:, lo:lo + MLA_QK_PAD], preferred_element_type=F32)
        kn = kv[:, :LANES]
        rs = lax.rsqrt((jnp.sum(kn * kn, axis=-1, keepdims=True) + kr_ss) * inv_d + NORM_EPS)
        k_ref[:, lo:lo + LANES] = (kn * rs * gn_ref[...]).astype(k_ref.dtype)
        k_ref[:, lo + LANES:lo + MLA_QK_PAD] = _rope_apply(kr * rs * gr_ref[...], cos, sin).astype(k_ref.dtype)
        v_ref[:, h * MLA_V:(h + 1) * MLA_V] = kv[:, LANES:].astype(v_ref.dtype)


def _mla_qkv(p, qn, kvn, wq, wkv, hgq, gn, gr, cos, sin, *, tm=512):
    S = p.shape[0]
    tm = min(tm, S)
    cq_block = RWKV_SEC // MLA_Q_LORA
    ckv_block = (RWKV_SEC + MLA_Q_LORA) // MLA_KV_LORA
    kr_block = (RWKV_SEC + MLA_Q_LORA + MLA_KV_LORA) // LANES
    NQ = MLA_HEADS * MLA_QK_PAD
    tab = pl.BlockSpec((tm, LANES), lambda i: (i, 0))
    full = lambda a: pl.BlockSpec(a.shape, lambda i: (0, 0))
    return pl.pallas_call(
        _mla_qkv_kernel,
        out_shape=(jax.ShapeDtypeStruct((S, NQ), BF16), jax.ShapeDtypeStruct((S, NQ), BF16),
                   jax.ShapeDtypeStruct((S, MLA_HEADS * MLA_V), BF16)),
        grid=(S // tm,),
        in_specs=[pl.BlockSpec((tm, MLA_Q_LORA), lambda i: (i, cq_block)),
                  pl.BlockSpec((tm, MLA_KV_LORA), lambda i: (i, ckv_block)),
                  pl.BlockSpec((tm, LANES), lambda i: (i, kr_block)),
                  full(qn), full(kvn), full(wq), full(wkv), full(hgq), full(gn), full(gr), tab, tab],
        out_specs=(pl.BlockSpec((tm, NQ), lambda i: (i, 0)), pl.BlockSpec((tm, NQ), lambda i: (i, 0)),
                   pl.BlockSpec((tm, MLA_HEADS * MLA_V), lambda i: (i, 0))),
        compiler_params=_cparams(("parallel",)),
        name="mla_qkv",
    )(p, p, p, qn, kvn, wq, wkv, hgq, gn, gr, cos, sin)


def _flash_rows(s, v1, rows, m_sc, acc_sc):
    m_prev = m_sc[rows, :]
    m_new = jnp.maximum(m_prev, jnp.max(s, axis=-1, keepdims=True))
    alpha = jnp.exp2(m_prev - m_new)
    pr = jnp.exp2((s - jnp.concatenate([m_new] * (s.shape[1] // LANES), axis=1)).astype(BF16))
    acc_sc[rows, :] = (jnp.concatenate([alpha, alpha], axis=1) * acc_sc[rows, :]
                       + jnp.dot(pr, v1, preferred_element_type=F32))
    m_sc[rows, :] = m_new


def _flash_init(m_sc, acc_sc):
    m_sc[...] = jnp.full(m_sc.shape, -jnp.inf, F32)
    acc_sc[...] = jnp.zeros(acc_sc.shape, F32)


def _with_ones(v):
    return jnp.concatenate([v, jnp.ones(v.shape, v.dtype)], axis=1)


def _flash_finish(acc_sc, o_ref):
    acc = acc_sc[...]
    d = acc.shape[1] // 2
    o_ref[...] = (acc[:, :d] / acc[:, d:]).astype(o_ref.dtype)


def _mla_attn_kernel(qi_ref, ki_ref, q_ref, k_ref, v_ref, o_ref, m_sc, acc_sc, *, sb):
    p = pl.program_id(1)
    qi = qi_ref[p]
    ki = ki_ref[p]
    t = q_ref.shape[0]

    @pl.when(ki == 0)
    def _():
        _flash_init(m_sc, acc_sc)

    @pl.when(ki < qi)
    def _():
        v1 = _with_ones(v_ref[...])
        for rb in range(t // sb):
            rows = slice(rb * sb, (rb + 1) * sb)
            s = _dot_nt(q_ref[rows, :], k_ref[...])
            _flash_rows(s, v1, rows, m_sc, acc_sc)

    @pl.when(ki == qi)
    def _():
        v1 = _with_ones(v_ref[...])
        for rb in range(t // sb):
            rows = slice(rb * sb, (rb + 1) * sb)
            nc = (rb + 1) * sb
            s = _dot_nt(q_ref[rows, :], k_ref[0:nc, :])
            row = lax.broadcasted_iota(jnp.int32, s.shape, 0) + rb * sb
            col = lax.broadcasted_iota(jnp.int32, s.shape, 1)
            s = jnp.where(col <= row, s, -jnp.inf)
            _flash_rows(s, v1[0:nc, :], rows, m_sc, acc_sc)
        _flash_finish(acc_sc, o_ref)


def _mla_attention(q, k, v, *, t=2048, sb=256):
    S = q.shape[0]
    t = min(t, S)
    sb = min(sb, t)
    nb = S // t
    pairs = [(a, b) for a in range(nb) for b in range(a + 1)]
    qi_tab = jnp.asarray(np.array([a for a, _ in pairs], np.int32))
    ki_tab = jnp.asarray(np.array([b for _, b in pairs], np.int32))
    stat = pltpu.VMEM((t, LANES), F32)
    grid_spec = pltpu.PrefetchScalarGridSpec(
        num_scalar_prefetch=2,
        grid=(MLA_HEADS, len(pairs)),
        in_specs=[pl.BlockSpec((t, MLA_QK_PAD), lambda h, p, qt, kt: (qt[p], h)),
                  pl.BlockSpec((t, MLA_QK_PAD), lambda h, p, qt, kt: (kt[p], h)),
                  pl.BlockSpec((t, MLA_V), lambda h, p, qt, kt: (kt[p], h))],
        out_specs=pl.BlockSpec((t, MLA_V), lambda h, p, qt, kt: (qt[p], h)),
        scratch_shapes=[stat, pltpu.VMEM((t, 2 * MLA_V), F32)],
    )
    return pl.pallas_call(
        functools.partial(_mla_attn_kernel, sb=sb),
        out_shape=jax.ShapeDtypeStruct((S, MLA_HEADS * MLA_V), BF16),
        grid_spec=grid_spec,
        compiler_params=_cparams(("parallel", "arbitrary")),
        name="mla_attention",
    )(qi_tab, ki_tab, q, k, v)


def _seg_sum(x, e_ref):
    return _split_dot(x, e_ref[...])


def _bmm(a, b):
    return jnp.einsum('gik,gkj->gij', a.astype(BF16), b.astype(BF16), preferred_element_type=F32)


def _bmm_nt(a, b):
    return jnp.einsum('gik,gjk->gij', a.astype(BF16), b.astype(BF16), preferred_element_type=F32)


def _bsplit_cumsum(tri, x):
    G = x.shape[0]
    tb = jnp.broadcast_to(tri.astype(BF16)[None], (G,) + tri.shape)
    x0 = x.astype(BF16)
    r1 = x - x0.astype(F32)
    x1 = r1.astype(BF16)
    x2 = (r1 - x1.astype(F32)).astype(BF16)
    f = lambda t: jnp.einsum('gts,gsl->gtl', tb, t, preferred_element_type=F32)
    return f(x0) + f(x1) + f(x2)


def _rwkv_kernel(pr_ref, pk_ref, pv_ref, px_ref, qr_ref, qk_ref, qv_ref, qx_ref, mur_ref, muk_ref, muv_ref, mux_ref,
                 w0_ref, w2_ref, a0_ref, a2_ref, g2_ref, kk_ref, ka_ref, lnw_ref, lnb_ref, rk_ref, tri_ref, e_ref,
                 o_ref, st_sc, sall_sc, kc_sc, nc_sc, pc_sc, *, hg, G):
    C = RWKV_CHUNK
    LW = hg * RWKV_HEAD
    RW = hg * C
    c_idx = pl.program_id(1)

    @pl.when(c_idx == 0)
    def _():
        st_sc[...] = jnp.zeros(st_sc.shape, F32)

    row = lax.broadcasted_iota(jnp.int32, (G * C, 1), 0)

    def shift_mix(x_ref, prev_ref, mu_ref):
        x = x_ref[...]
        last = jnp.where(c_idx == 0, 0.0, prev_ref[7:8, :])
        prev = jnp.where(row == 0, last, pltpu.roll(x, 1, 0))
        return x + (prev - x) * mu_ref[...]

    r = shift_mix(pr_ref, qr_ref, mur_ref)
    kb = shift_mix(pk_ref, qk_ref, muk_ref)
    v = shift_mix(pv_ref, qv_ref, muv_ref)
    xx = shift_mix(px_ref, qx_ref, mux_ref)
    lw = -EXP_NEG_HALF * _sigmoid(w0_ref[...] + _dot(jnp.tanh(xx[:, :LANES]), w2_ref[...]))
    a = _sigmoid(a0_ref[...] + _dot(xx[:, LANES:2 * LANES], a2_ref[...]))
    g = _dot(_sigmoid(xx[:, 2 * LANES:]), g2_ref[...])
    kk = kb * kk_ref[...]
    kkn = kk * lax.rsqrt(jnp.maximum(_seg_sum(kk * kk, e_ref), 1e-24))
    k = kb * (1.0 + (a - 1.0) * ka_ref[...])
    b = kkn * a

    lw3 = lw.reshape(G, C, LW)
    cum3 = _bsplit_cumsum(tri_ref[...], lw3)
    clast = cum3[:, C - 1:C, :]
    r3 = r.reshape(G, C, LW)
    k3 = k.reshape(G, C, LW)
    v3 = v.reshape(G, C, LW)
    kk3 = kkn.reshape(G, C, LW)
    b3 = b.reshape(G, C, LW)
    einv = jnp.exp(-cum3)
    pc = jnp.exp(clast)
    etail = pc * einv
    rt = r3 * jnp.exp(cum3)
    at = -kk3 * jnp.exp(cum3 - lw3)
    bt = b3 * einv
    kt = k3 * einv
    bh = b3 * etail
    kh = k3 * etail

    lane_head = lax.broadcasted_iota(jnp.int32, (1, 1, LW), 2) // RWKV_HEAD

    def stack(x):
        return jnp.concatenate([jnp.where(lane_head == h, x, 0.0) for h in range(hg)], axis=1)

    a2, b2, k2, r2, v2, bh2, kh2 = (stack(t).astype(BF16) for t in (at, bt, kt, rt, v3, bh, kh))
    ri = lax.broadcasted_iota(jnp.int32, (1, RW, RW), 1)
    ci = lax.broadcasted_iota(jnp.int32, (1, RW, RW), 2)
    same = (ri // C) == (ci // C)
    strict = same & (ci < ri)
    incl = same & (ci <= ri)
    eye = (ri == ci).astype(F32)

    bk2 = jnp.concatenate([b2, k2], axis=1)
    a_bk = _bmm_nt(a2, bk2)
    r_bk = _bmm_nt(r2, bk2)
    lmat = jnp.where(strict, a_bk[:, :, :RW], 0.0)
    akm = jnp.where(strict, a_bk[:, :, RW:], 0.0)
    rbm = jnp.where(incl, r_bk[:, :, :RW], 0.0)
    rkm = jnp.where(incl, r_bk[:, :, RW:], 0.0)
    tinv = eye + lmat
    pw = lmat
    for _ in range(int(np.log2(C)) - 1):
        pw = _bmm(pw, pw)
        tinv = tinv + _bmm(tinv, pw)
    wm = _bmm(tinv, a2)
    z = _bmm(tinv, _bmm(akm, v2))
    qm = r2.astype(F32) + _bmm(rbm, wm)
    y0 = _bmm(rkm, v2) + _bmm(rbm, z)
    kc_sc[...] = _bmm(jnp.swapaxes(wm, 1, 2), bh2)
    nc_sc[...] = _bmm(jnp.swapaxes(z, 1, 2), bh2) + _bmm(jnp.swapaxes(v2, 1, 2), kh2)
    pc_sc[...] = pc

    def chunk(c, carry):
        st = st_sc[...]
        sall_sc[c] = st
        st_sc[...] = st * pc_sc[c] + _dot(st, kc_sc[c]) + nc_sc[c]
        return carry

    lax.fori_loop(0, G, chunk, 0, unroll=True)

    y2 = _bmm_nt(qm, sall_sc[...]) + y0
    y = y2[:, 0:C]
    for h in range(1, hg):
        y = y + y2[:, h * C:(h + 1) * C]

    y = y.reshape(G * C, LW)
    inv_n = 1.0 / RWKV_HEAD
    mu = _seg_sum(y, e_ref) * inv_n
    d = y - mu
    var = _seg_sum(d * d, e_ref) * inv_n
    yn = d * lax.rsqrt(var + RWKV_GN_EPS) * lnw_ref[...] + lnb_ref[...]
    bonus = _seg_sum(r * k * rk_ref[...], e_ref) * v
    o_ref[...] = ((yn + bonus) * g).astype(o_ref.dtype)


def _rwkv_mix(p, mu, w0, w2, a0, a2, g2, k_k, k_a, ln_w, ln_b, r_k, *, hg=2, G=16):
    S = p.shape[0]
    C = RWKV_CHUNK
    G = min(G, S // C)
    TB = G * C
    LW = hg * RWKV_HEAD
    W = RWKV_WIDTH
    ngrp = W // LW
    XW = RWKV_SEC - 3 * W
    t = np.arange(C)
    tri = jnp.asarray((t[None, :] <= t[:, None]).astype(np.float32), BF16)
    l = np.arange(LW)
    e = jnp.asarray((l[:, None] // RWKV_HEAD == l[None, :] // RWKV_HEAD).astype(np.float32), BF16)

    def cols(width, first):
        blk = pl.BlockSpec((TB, width), lambda gi, c: (c, first(gi)))
        prev = pl.BlockSpec((8, width), lambda gi, c: (jnp.maximum(c * (TB // 8) - 1, 0), first(gi)))
        par = pl.BlockSpec((1, width), lambda gi, c: (0, first(gi)))
        return blk, prev, par

    rb, rp, rm = cols(LW, lambda gi: gi)
    kb, kp, km = cols(LW, lambda gi: ngrp + gi)
    vb, vp, vm = cols(LW, lambda gi: 2 * ngrp + gi)
    xb, xp, xm = cols(XW, lambda gi: 3 * W // XW)
    row = pl.BlockSpec((1, LW), lambda gi, c: (0, gi))
    lora = lambda n: pl.BlockSpec((n, LW), lambda gi, c: (0, gi))
    sq = lambda: pltpu.VMEM((G, LW, LW), F32)
    return pl.pallas_call(
        functools.partial(_rwkv_kernel, hg=hg, G=G),
        out_shape=jax.ShapeDtypeStruct((S, W), BF16),
        grid=(ngrp, S // TB),
        in_specs=[rb, kb, vb, xb, rp, kp, vp, xp, rm, km, vm, xm,
                  row, lora(LANES), row, lora(LANES), lora(2 * LANES), row, row, row, row, row,
                  pl.BlockSpec((C, C), lambda gi, c: (0, 0)), pl.BlockSpec((LW, LW), lambda gi, c: (0, 0))],
        out_specs=pl.BlockSpec((TB, LW), lambda gi, c: (c, gi)),
        scratch_shapes=[pltpu.VMEM((LW, LW), F32), sq(), sq(), sq(), pltpu.VMEM((G, 1, LW), F32)],
        compiler_params=_cparams(("parallel", "arbitrary")),
        name="rwkv_mix",
    )(p, p, p, p, p, p, p, p, mu, mu, mu, mu, w0, w2, a0, a2, g2, k_k, k_a, ln_w, ln_b, r_k, tri, e)


def _outproj_kernel(h_ref, a_ref, b_ref, wa_ref, wb_ref, o_ref):
    o_ref[...] = (h_ref[...] + jnp.dot(a_ref[...], wa_ref[...], preferred_element_type=F32)
                  + jnp.dot(b_ref[...], wb_ref[...], preferred_element_type=F32))


def _outproj(h, a, b, w, *, tm=512):
    S, D = h.shape
    tm = min(tm, S)
    Ka, Kb = a.shape[1], b.shape[1]
    assert Ka == Kb and w.shape[0] == Ka + Kb
    return pl.pallas_call(
        _outproj_kernel,
        out_shape=jax.ShapeDtypeStruct((S, D), F32),
        grid=(S // tm,),
        in_specs=[pl.BlockSpec((tm, D), lambda i: (i, 0)),
                  pl.BlockSpec((tm, Ka), lambda i: (i, 0)),
                  pl.BlockSpec((tm, Kb), lambda i: (i, 0)),
                  pl.BlockSpec((Ka, D), lambda i: (0, 0)),
                  pl.BlockSpec((Kb, D), lambda i: (1, 0))],
        out_specs=pl.BlockSpec((tm, D), lambda i: (i, 0)),
        compiler_params=_cparams(("parallel",)),
        name="outproj",
    )(h, a, b, w, w)


def _pad_cols(w, width):
    return jnp.pad(w, ((0, 0), (0, width - w.shape[1])))


def _rope_gap_layout(w):
    z = jnp.zeros(w.shape[:-1] + (32,), w.dtype)
    return jnp.concatenate([w[..., :32], z, w[..., 32:], z], axis=-1)


def _even_mixer(h, hn, tabs, w_in, q_norm, w_uq, kv_norm, w_ukv, q_head_norm, k_head_norm,
                mu, w0, w2, a0, a2, g2, k_k, k_a, r_k, ln_w, ln_b, w_out):
    cos1, sin1 = tabs[0], tabs[1]
    W = RWKV_WIDTH
    o_cq = 0
    o_ckv = MLA_Q_LORA
    o_kr = o_ckv + MLA_KV_LORA
    o_rw = o_kr + MLA_ROPE
    w_in = w_in.astype(BF16)
    rw = w_in[:, o_rw:]

    def rwkv_layout(t):
        return jnp.concatenate([t[:, :3 * W], _pad_cols(t[:, 3 * W:3 * W + RWKV_W_LORA], LANES),
                                _pad_cols(t[:, 3 * W + RWKV_W_LORA:3 * W + RWKV_W_LORA + RWKV_A_LORA], LANES),
                                _pad_cols(t[:, 3 * W + RWKV_W_LORA + RWKV_A_LORA:], 2 * LANES)], axis=1)

    w_in_p = jnp.concatenate([rwkv_layout(rw), w_in[:, o_cq:o_ckv], w_in[:, o_ckv:o_kr],
                              _rope_gap_layout(w_in[:, o_kr:o_rw])], axis=1)
    w_in_p = _pad_cols(w_in_p, 4608)
    p = _proj(hn, w_in_p, tn=1536)

    wq = w_uq.astype(BF16).reshape(MLA_Q_LORA, MLA_HEADS, MLA_QK)
    wq = jnp.concatenate([wq[..., :MLA_NOPE], _rope_gap_layout(wq[..., MLA_NOPE:])], axis=-1)
    wq = wq.reshape(MLA_Q_LORA, MLA_HEADS * MLA_QK_PAD)
    scale = MLA_QK ** -0.5 * LOG2E
    hg_q = jnp.concatenate([q_head_norm[:MLA_NOPE], _rope_gap_layout(q_head_norm[MLA_NOPE:])])[None, :] * scale
    gn = k_head_norm[None, :MLA_NOPE]
    gr = _rope_gap_layout(k_head_norm[MLA_NOPE:])[None, :]
    q, kmat, vmat = _mla_qkv(p, q_norm[None, :], kv_norm[None, :], wq, w_ukv.astype(BF16), hg_q, gn, gr, cos1, sin1)
    o_a = _mla_attention(q, kmat, vmat)

    w2p = jnp.pad(w2, ((0, LANES - RWKV_W_LORA), (0, 0))).astype(BF16)
    a2p = jnp.pad(a2, ((0, LANES - RWKV_A_LORA), (0, 0))).astype(BF16)
    g2p = jnp.pad(g2, ((0, 2 * LANES - RWKV_G_LORA), (0, 0))).astype(BF16)
    o_b = _rwkv_mix(p, rwkv_layout(mu[None, :]), w0[None, :], w2p, a0[None, :], a2p, g2p, k_k[None, :], k_a[None, :],
                    ln_w[None, :], ln_b[None, :], r_k.reshape(1, W))
    return _outproj(h, o_a, o_b, w_out.astype(BF16))


def _swa_qkv_kernel(x_ref, w_ref, hg_ref, cos_ref, sin_ref, o_ref):
    n = pl.program_id(0)
    half = x_ref.shape[0] // 2

    @pl.when(n < 2)
    def _():
        for r0 in (0, half):
            rows = slice(r0, r0 + half)
            acc = jnp.dot(x_ref[rows, :], w_ref[...], preferred_element_type=F32)
            cos = cos_ref[rows, :]
            sin = sin_ref[rows, :]
            for c in range(acc.shape[1] // LANES):
                sl = slice(c * LANES, (c + 1) * LANES)
                o_ref[rows, sl] = _rope_apply(_rms(acc[:, sl], hg_ref[:, sl]), cos, sin).astype(o_ref.dtype)

    @pl.when(n == 2)
    def _():
        o_ref[...] = jnp.dot(x_ref[...], w_ref[...], preferred_element_type=F32).astype(o_ref.dtype)


def _swa_qkv(x, w, hg, cos, sin, *, tm=1024):
    S, K = x.shape
    tn = SWA_HEADS * SWA_DIM
    tm = min(tm, S)
    tab = pl.BlockSpec((tm, LANES), lambda n, i: (i, 0))
    return pl.pallas_call(
        _swa_qkv_kernel,
        out_shape=jax.ShapeDtypeStruct((S, 3 * tn), BF16),
        grid=(3, S // tm),
        in_specs=[pl.BlockSpec((tm, K), lambda n, i: (i, 0)),
                  pl.BlockSpec((K, tn), lambda n, i: (0, n)),
                  pl.BlockSpec((1, tn), lambda n, i: (0, jnp.minimum(n, 1))),
                  tab, tab],
        out_specs=pl.BlockSpec((tm, tn), lambda n, i: (i, n)),
        compiler_params=_cparams(("parallel", "parallel")),
        name="swa_qkv",
    )(x, w, hg, cos, sin)


def _dilated_bias(t, nrel):
    row = np.arange(t)[:, None]
    col = np.arange(t)[None, :]
    out = np.empty((nrel, t, t), np.float32)
    for r in range(nrel):
        delta = r * t + row - col
        cnt = np.zeros((t, t), np.float64)
        for window, dilation in SWA_PATTERNS:
            cnt += (delta >= 0) & (delta <= window) & (delta % dilation == 0)
        with np.errstate(divide="ignore"):
            out[r] = np.log2(cnt)
    return out


def _dilated_kernel(q_ref, k_ref, v_ref, bias_ref, o_ref, m_sc, acc_sc, *, sb, nrel, live):
    qi = pl.program_id(1)
    r = pl.program_id(2)
    t = q_ref.shape[0]

    @pl.when(r == 0)
    def _():
        _flash_init(m_sc, acc_sc)

    for rr in range(nrel):
        @pl.when((r == rr) & (qi - r >= 0))
        def _(rr=rr):
            v1 = _with_ones(v_ref[...])
            for rb in range(t // sb):
                rows = slice(rb * sb, (rb + 1) * sb)
                lo, hi = live[rr][rb]
                s = _dot_nt(q_ref[rows, :], k_ref[lo:hi, :]) + bias_ref[rr, rows, lo:hi]
                _flash_rows(s, v1[lo:hi, :], rows, m_sc, acc_sc)

    @pl.when(r == nrel - 1)
    def _():
        _flash_finish(acc_sc, o_ref)


def _dilated_attention(qkv, *, t=1024, sb=256):
    S = qkv.shape[0]
    t = min(t, S)
    sb = min(sb, t)
    max_window = max(w for w, _ in SWA_PATTERNS)
    nrel = min(-(-max_window // t) + 1, S // t)
    H = SWA_HEADS
    bias = _dilated_bias(t, nrel)
    live = []
    for r in range(nrel):
        per_rb = []
        for rb in range(t // sb):
            cols = np.nonzero(np.isfinite(bias[r, rb * sb:(rb + 1) * sb]).any(axis=0))[0]
            lo = int(cols.min()) // LANES * LANES
            hi = -(-(int(cols.max()) + 1) // LANES) * LANES
            per_rb.append((lo, hi))
        live.append(per_rb)
    stat = pltpu.VMEM((t, LANES), F32)
    return pl.pallas_call(
        functools.partial(_dilated_kernel, sb=sb, nrel=nrel, live=live),
        out_shape=jax.ShapeDtypeStruct((S, H * SWA_DIM), BF16),
        grid=(H, S // t, nrel),
        in_specs=[pl.BlockSpec((t, SWA_DIM), lambda h, i, r: (i, h)),
                  pl.BlockSpec((t, SWA_DIM), lambda h, i, r: (jnp.maximum(i - r, 0), H + h)),
                  pl.BlockSpec((t, SWA_DIM), lambda h, i, r: (jnp.maximum(i - r, 0), 2 * H + h)),
                  pl.BlockSpec((nrel, t, t), lambda h, i, r: (0, 0, 0))],
        out_specs=pl.BlockSpec((t, SWA_DIM), lambda h, i, r: (i, h)),
        scratch_shapes=[stat, pltpu.VMEM((t, 2 * SWA_DIM), F32)],
        compiler_params=_cparams(("parallel", "parallel", "arbitrary")),
        name="dilated_attention",
    )(qkv, qkv, qkv, jnp.asarray(bias))


def _gla_kernel(q_ref, k_ref, v_ref, glr_ref, wgu_ref, bg_ref, rd_ref, gn_ref, tri_ref, o_ref,
                st_sc, sall_sc, oi_sc, *, G):
    C = GLA_CHUNK
    SB = GLA_SUB
    DK = GLA_DK
    DV = GLA_DV
    c_idx = pl.program_id(1)

    @pl.when(c_idx == 0)
    def _():
        st_sc[...] = jnp.zeros(st_sc.shape, F32)

    z = _dot(glr_ref[...], wgu_ref[...]) + bg_ref[...]
    lg = -_softplus(-z) * (1.0 / GLA_NORMALIZER)
    bc = _bsplit_cumsum(tri_ref[...], lg.reshape(G, C, DK))
    q = (q_ref[...] * (DK ** -0.5)).reshape(G, C, DK)
    k = k_ref[...].reshape(G, C, DK)
    v = v_ref[...].reshape(G, C, DV)
    blast = bc[:, C - 1:C, :]
    khat = k * jnp.exp(blast - bc)
    gkv = _bmm(jnp.swapaxes(v, 1, 2), khat)
    eb = jnp.exp(blast)

    st = st_sc[...]
    for c in range(G):
        sall_sc[c] = st
        st = st * eb[c] + gkv[c]
    st_sc[...] = st

    ti = lax.broadcasted_iota(jnp.int32, (1, SB, SB, 1), 1)
    si = lax.broadcasted_iota(jnp.int32, (1, SB, SB, 1), 2)
    causal = si <= ti
    for sb in range(C // SB):
        lo = sb * SB
        qs = q[:, lo:lo + SB]
        ks = k[:, lo:lo + SB]
        bs = bc[:, lo:lo + SB]
        rel = bs[:, :, None, :] - bs[:, None, :, :]
        dec = jnp.exp(jnp.where(causal, rel, -jnp.inf))
        att = jnp.sum(qs[:, :, None, :] * ks[:, None, :, :] * dec, axis=-1)
        o_sb = _bmm(att, v[:, lo:lo + SB])
        if sb > 0:
            bm = bc[:, lo - 1:lo]
            qsc = qs * jnp.exp(bs - bm)
            ksc = k[:, :lo] * jnp.exp(bm - bc[:, :lo])
            o_sb = o_sb + _bmm(_bmm_nt(qsc, ksc), v[:, :lo])
        oi_sc[:, lo:lo + SB, :] = o_sb

    o = (oi_sc[...] + _bmm_nt(q * jnp.exp(bc), sall_sc[...])).reshape(G * C, DV)
    rd = rd_ref[...]
    o_ref[...] = (_rms(o, gn_ref[...]) * (rd * _sigmoid(rd))).astype(o_ref.dtype)


def _gla(pb, wgu, bg, gn, *, G=16):
    S = pb.shape[0]
    C = GLA_CHUNK
    G = min(G, S // C)
    TB = G * C
    H, DK, DV = GLA_HEADS, GLA_DK, GLA_DV
    t = np.arange(C)
    tri = jnp.asarray((t[None, :] <= t[:, None]).astype(np.float32), BF16)
    glr_block = (2 * H * DK + 2 * H * DV) // LANES
    return pl.pallas_call(
        functools.partial(_gla_kernel, G=G),
        out_shape=jax.ShapeDtypeStruct((S, H * DV), BF16),
        grid=(H, S // TB),
        in_specs=[pl.BlockSpec((TB, DK), lambda h, c: (c, h)),
                  pl.BlockSpec((TB, DK), lambda h, c: (c, H + h)),
                  pl.BlockSpec((TB, DV), lambda h, c: (c, (2 * H * DK) // DV + h)),
                  pl.BlockSpec((TB, LANES), lambda h, c: (c, glr_block)),
                  pl.BlockSpec((LANES, DK), lambda h, c: (0, h)),
                  pl.BlockSpec((1, DK), lambda h, c: (0, h)),
                  pl.BlockSpec((TB, DV), lambda h, c: (c, (2 * H * DK + H * DV) // DV + h)),
                  pl.BlockSpec((1, DV), lambda h, c: (0, h)),
                  pl.BlockSpec((C, C), lambda h, c: (0, 0))],
        out_specs=pl.BlockSpec((TB, DV), lambda h, c: (c, h)),
        scratch_shapes=[pltpu.VMEM((DV, DK), F32), pltpu.VMEM((G, DV, DK), F32), pltpu.VMEM((G, C, DV), F32)],
        compiler_params=_cparams(("parallel", "arbitrary")),
        name="gla",
    )(pb, pb, pb, pb, wgu, bg, pb, gn, tri)


def _odd_mixer(h, hn, tabs, w_in, q_head_norm, k_head_norm, w_gate_up, b_gate, gla_norm, w_out):
    cos2, sin2 = tabs[2], tabs[3]
    nq = SWA_HEADS * SWA_DIM
    hg = jnp.concatenate([jnp.tile(q_head_norm * (SWA_DIM ** -0.5 * LOG2E), SWA_HEADS), jnp.tile(k_head_norm, SWA_HEADS)])[None, :]
    w_in = w_in.astype(BF16)
    qkv = _swa_qkv(hn, w_in, hg, cos2, sin2)
    o_c = _dilated_attention(qkv)
    o = 3 * nq
    dk, dv = GLA_HEADS * GLA_DK, GLA_HEADS * GLA_DV
    wb = jnp.concatenate([w_in[:, o:o + 2 * dk + dv], w_in[:, o + 2 * dk + dv + GLA_LORA:],
                          _pad_cols(w_in[:, o + 2 * dk + dv:o + 2 * dk + dv + GLA_LORA], LANES)], axis=1)
    wb = _pad_cols(wb, 3584)
    pb = _proj(hn, wb, tn=1792)
    wgu = jnp.pad(w_gate_up, ((0, LANES - GLA_LORA), (0, 0))).astype(BF16)
    o_d = _gla(pb, wgu, b_gate[None, :], gla_norm.reshape(1, dv))
    return _outproj(h, o_c, o_d, w_out.astype(BF16))


def kernel(x, positions, ffn_norm, ffn_w_gate, ffn_w_up, ffn_w_down, mix_norm, mla_rwkv_w_in, mla_q_norm, mla_w_uq,
           mla_kv_norm, mla_w_ukv, mla_q_head_norm, mla_k_head_norm, rwkv_mu, rwkv_w0, rwkv_w2, rwkv_a0, rwkv_a2,
           rwkv_g2, rwkv_k_k, rwkv_k_a, rwkv_r_k, rwkv_ln_w, rwkv_ln_b, mla_rwkv_w_out, swa_gla_w_in,
           swa_q_head_norm, swa_k_head_norm, gla_w_gate_up, gla_b_gate, gla_norm, swa_gla_w_out):
    B, S, D = x.shape
    assert B == 1
    h = x.reshape(S, D)
    tabs = _rope_tables(positions.reshape(S))
    wg = ffn_w_gate.astype(BF16)
    wu = ffn_w_up.astype(BF16)
    wd = ffn_w_down.astype(BF16)
    depth = ffn_norm.shape[0]
    for layer in range(depth):
        i = layer // 2
        h, hn = _ffn(h, ffn_norm, wg, wu, wd, layer, 0, next_norm=mix_norm[layer])
        if layer % 2 == 0:
            h = _even_mixer(h, hn, tabs, mla_rwkv_w_in[i], mla_q_norm[i], mla_w_uq[i], mla_kv_norm[i],
                            mla_w_ukv[i], mla_q_head_norm[i], mla_k_head_norm[i], rwkv_mu[i], rwkv_w0[i],
                            rwkv_w2[i], rwkv_a0[i], rwkv_a2[i], rwkv_g2[i], rwkv_k_k[i], rwkv_k_a[i], rwkv_r_k[i],
                            rwkv_ln_w[i], rwkv_ln_b[i], mla_rwkv_w_out[i])
        else:
            h = _odd_mixer(h, hn, tabs, swa_gla_w_in[i], swa_q_head_norm[i], swa_k_head_norm[i],
                           gla_w_gate_up[i], gla_b_gate[i], gla_norm[i], swa_gla_w_out[i])
        h = _ffn(h, ffn_norm, wg, wu, wd, layer, 1)
    return h.reshape(B, S, D)
```

```python
import functools

import numpy as np
import jax
import jax.numpy as jnp
from jax import lax
from jax.experimental import pallas as pl
from jax.experimental.pallas import tpu as pltpu

F32 = jnp.float32
BF16 = jnp.bfloat16

LANES = 128
VMEM_LIMIT = 56 * 1024 * 1024

D_MODEL = 2048
D_FF = 5632
MACARON_WEIGHT = 0.5
NORM_EPS = 1e-6
ROPE_THETA = 10000.0
LOG2E = float(np.log2(np.e))
EXP_NEG_HALF = float(np.exp(-0.5))

MLA_HEADS = 8
MLA_NOPE = 128
MLA_ROPE = 64
MLA_QK = MLA_NOPE + MLA_ROPE
MLA_QK_PAD = 256
MLA_V = 128
MLA_Q_LORA = 512
MLA_KV_LORA = 256

RWKV_HEAD = 64
RWKV_WIDTH = 1024
RWKV_HEADS = 16
RWKV_W_LORA = 64
RWKV_A_LORA = 64
RWKV_G_LORA = 160
RWKV_GN_EPS = 64e-5
RWKV_SEC = 3584
RWKV_CHUNK = 64

SWA_HEADS = 8
SWA_DIM = 128
SWA_PATTERNS = ((128, 1), (512, 4), (2048, 16))

GLA_HEADS = 4
GLA_DK = 128
GLA_DV = 256
GLA_LORA = 16
GLA_NORMALIZER = 16.0
GLA_CHUNK = 64
GLA_SUB = 8


def _cparams(sem):
    return pltpu.CompilerParams(dimension_semantics=sem, vmem_limit_bytes=VMEM_LIMIT)


def _rms(x, g):
    return x * lax.rsqrt(jnp.mean(x * x, axis=-1, keepdims=True) + NORM_EPS) * g


def _dot(a, b):
    return jnp.dot(a.astype(BF16), b.astype(BF16), preferred_element_type=F32)


def _dot_nt(a, b):
    return lax.dot_general(a.astype(BF16), b.astype(BF16), (((1,), (1,)), ((), ())),
                           preferred_element_type=F32)


def _split_dot(a, b):
    a0 = a.astype(BF16)
    r1 = a - a0.astype(F32)
    a1 = r1.astype(BF16)
    a2 = (r1 - a1.astype(F32)).astype(BF16)
    b = b.astype(BF16)
    return (jnp.dot(a0, b, preferred_element_type=F32) + jnp.dot(a1, b, preferred_element_type=F32)
            + jnp.dot(a2, b, preferred_element_type=F32))


def _sigmoid(x):
    return 0.5 * jnp.tanh(0.5 * x) + 0.5


def _softplus(x):
    return jnp.maximum(x, 0.0) + jnp.log(1.0 + jnp.exp(-jnp.abs(x)))


def _rope_apply(x, cos, sin_signed):
    return x * cos + pltpu.roll(x, 64, 1) * sin_signed


def _ffn_kernel(*refs, emit_norm, mix):
    it = iter(refs)
    h_ref, g_ref, wg_ref, wu_ref, wd_ref = (next(it) for _ in range(5))
    if mix:
        a_ref, b_ref, wa_ref, wb_ref = (next(it) for _ in range(4))
    if emit_norm:
        g2_ref = next(it)
    o_ref = next(it)
    if emit_norm:
        hn_ref = next(it)
    xn_ref = next(it)
    j = pl.program_id(1)

    @pl.when(j == 0)
    def _():
        x = h_ref[...]
        if mix:
            x = (x + jnp.dot(a_ref[...], wa_ref[...], preferred_element_type=F32)
                 + jnp.dot(b_ref[...], wb_ref[...], preferred_element_type=F32))
        xn_ref[...] = _rms(x, g_ref[...]).astype(BF16)
        o_ref[...] = x

    xn = xn_ref[...]
    gate = jnp.dot(xn, wg_ref[...], preferred_element_type=F32)
    up = jnp.dot(xn, wu_ref[...], preferred_element_type=F32)
    act = (gate * _sigmoid(gate) * up * MACARON_WEIGHT).astype(BF16)
    o_ref[...] += jnp.dot(act, wd_ref[...], preferred_element_type=F32)

    if emit_norm:
        @pl.when(j == pl.num_programs(1) - 1)
        def _():
            hn_ref[...] = _rms(o_ref[...], g2_ref[...]).astype(hn_ref.dtype)


def _ffn(h, g, wg, wu, wd, layer, k, next_norm=None, mix=None, *, tm=512, tf=512):
    S, D = h.shape
    F = wg.shape[-1]
    tm = min(tm, S)
    emit_norm = next_norm is not None
    wspec = pl.BlockSpec((None, None, D, tf), lambda i, j: (layer, k, 0, j))
    row = pl.BlockSpec((1, D), lambda i, j: (0, 0))
    tile = pl.BlockSpec((tm, D), lambda i, j: (i, 0))
    in_specs = [tile, row, wspec, wspec, pl.BlockSpec((None, None, tf, D), lambda i, j: (layer, k, j, 0))]
    args = [h, g[layer, k][None, :], wg, wu, wd]
    if mix is not None:
        a, b, w_out = mix
        ka = a.shape[1]
        assert b.shape[1] == ka and w_out.shape == (2 * ka, D)
        half = pl.BlockSpec((tm, ka), lambda i, j: (i, 0))
        in_specs += [half, half, pl.BlockSpec((ka, D), lambda i, j: (0, 0)), pl.BlockSpec((ka, D), lambda i, j: (1, 0))]
        args += [a, b, w_out, w_out]
    out_shape = jax.ShapeDtypeStruct((S, D), F32)
    out_specs = tile
    if emit_norm:
        in_specs.append(row)
        args.append(next_norm[None, :])
        out_shape = (out_shape, jax.ShapeDtypeStruct((S, D), BF16))
        out_specs = (tile, tile)
    return pl.pallas_call(
        functools.partial(_ffn_kernel, emit_norm=emit_norm, mix=mix is not None),
        out_shape=out_shape,
        grid=(S // tm, F // tf),
        in_specs=in_specs,
        out_specs=out_specs,
        scratch_shapes=[pltpu.VMEM((tm, D), BF16)],
        compiler_params=_cparams(("parallel", "arbitrary")),
        name="ffn",
    )(*args)


def _rope_tab_kernel(pos_ref, f1_ref, s1_ref, f2_ref, s2_ref, c1_ref, n1_ref, c2_ref, n2_ref):
    pos = pos_ref[...].astype(F32)
    a1 = pos * f1_ref[...]
    c1_ref[...] = jnp.cos(a1)
    n1_ref[...] = jnp.sin(a1) * s1_ref[...]
    a2 = pos * f2_ref[...]
    c2_ref[...] = jnp.cos(a2)
    n2_ref[...] = jnp.sin(a2) * s2_ref[...]


def _rope_tables(positions, *, tm=1024):
    S = positions.shape[0]
    tm = min(tm, S)
    f32half = ROPE_THETA ** (-jnp.arange(MLA_ROPE // 2, dtype=F32) / (MLA_ROPE // 2))
    z = jnp.zeros((32,), F32)
    f1 = jnp.concatenate([f32half, z, f32half, z])[None]
    f64half = ROPE_THETA ** (-jnp.arange(SWA_DIM // 2, dtype=F32) / (SWA_DIM // 2))
    f2 = jnp.concatenate([f64half, f64half])[None]
    sign = jnp.concatenate([-jnp.ones((64,), F32), jnp.ones((64,), F32)])[None]
    row = pl.BlockSpec((1, LANES), lambda i: (0, 0))
    tab = pl.BlockSpec((tm, LANES), lambda i: (i, 0))
    shp = jax.ShapeDtypeStruct((S, LANES), F32)
    return pl.pallas_call(
        _rope_tab_kernel,
        out_shape=(shp, shp, shp, shp),
        grid=(S // tm,),
        in_specs=[pl.BlockSpec((tm, 1), lambda i: (i, 0)), row, row, row, row],
        out_specs=(tab, tab, tab, tab),
        compiler_params=_cparams(("parallel",)),
        name="rope_tables",
    )(positions.reshape(S, 1), f1, sign, f2, sign)


def _proj_kernel(x_ref, w_ref, o_ref):
    o_ref[...] = jnp.dot(x_ref[...], w_ref[...], preferred_element_type=F32).astype(o_ref.dtype)


def _proj(x, w, *, tm=512, tn, out_dtype=F32):
    S, K = x.shape
    N = w.shape[1]
    tm = min(tm, S)
    return pl.pallas_call(
        _proj_kernel,
        out_shape=jax.ShapeDtypeStruct((S, N), out_dtype),
        grid=(N // tn, S // tm),
        in_specs=[pl.BlockSpec((tm, K), lambda n, i: (i, 0)),
                  pl.BlockSpec((K, tn), lambda n, i: (0, n))],
        out_specs=pl.BlockSpec((tm, tn), lambda n, i: (i, n)),
        compiler_params=_cparams(("parallel", "parallel")),
        name="proj",
    )(x, w)


def _mla_qkv_kernel(cq_ref, ckv_ref, kr_ref, qn_ref, kvn_ref, wq_ref, wkv_ref, hgq_ref, gn_ref, gr_ref,
                    cos_ref, sin_ref, q_ref, k_ref, v_ref):
    cos = cos_ref[...]
    sin = sin_ref[...]
    xq = _rms(cq_ref[...], qn_ref[...]).astype(BF16)
    xkv = _rms(ckv_ref[...], kvn_ref[...]).astype(BF16)
    kr = kr_ref[...]
    kr_ss = jnp.sum(kr * kr, axis=-1, keepdims=True)
    inv_d = 1.0 / MLA_QK
    for h in range(MLA_HEADS):
        lo = h * MLA_QK_PAD
        a = jnp.dot(xq, wq_ref[:, lo:lo + MLA_QK_PAD], preferred_element_type=F32)
        y = a * lax.rsqrt(jnp.sum(a * a, axis=-1, keepdims=True) * inv_d + NORM_EPS) * hgq_ref[...]
        q_ref[:, lo:lo + LANES] = y[:, :LANES].astype(q_ref.dtype)
        q_ref[:, lo + LANES:lo + MLA_QK_PAD] = _rope_apply(y[:, LANES:], cos, sin).astype(q_ref.dtype)
        kv = jnp.dot(xkv, wkv_ref[:, lo:lo + MLA_QK_PAD], preferred_element_type=F32)
        kn = kv[:, :LANES]
        rs = lax.rsqrt((jnp.sum(kn * kn, axis=-1, keepdims=True) + kr_ss) * inv_d + NORM_EPS)
        k_ref[:, lo:lo + LANES] = (kn * rs * gn_ref[...]).astype(k_ref.dtype)
        k_ref[:, lo + LANES:lo + MLA_QK_PAD] = _rope_apply(kr * rs * gr_ref[...], cos, sin).astype(k_ref.dtype)
        v_ref[:, h * MLA_V:(h + 1) * MLA_V] = kv[:, LANES:].astype(v_ref.dtype)


def _mla_qkv(p, qn, kvn, wq, wkv, hgq, gn, gr, cos, sin, *, tm=512):
    S = p.shape[0]
    tm = min(tm, S)
    cq_block = RWKV_SEC // MLA_Q_LORA
    ckv_block = (RWKV_SEC + MLA_Q_LORA) // MLA_KV_LORA
    kr_block = (RWKV_SEC + MLA_Q_LORA + MLA_KV_LORA) // LANES
    NQ = MLA_HEADS * MLA_QK_PAD
    tab = pl.BlockSpec((tm, LANES), lambda i: (i, 0))
    full = lambda a: pl.BlockSpec(a.shape, lambda i: (0, 0))
    return pl.pallas_call(
        _mla_qkv_kernel,
        out_shape=(jax.ShapeDtypeStruct((S, NQ), BF16), jax.ShapeDtypeStruct((S, NQ), BF16),
                   jax.ShapeDtypeStruct((S, MLA_HEADS * MLA_V), BF16)),
        grid=(S // tm,),
        in_specs=[pl.BlockSpec((tm, MLA_Q_LORA), lambda i: (i, cq_block)),
                  pl.BlockSpec((tm, MLA_KV_LORA), lambda i: (i, ckv_block)),
                  pl.BlockSpec((tm, LANES), lambda i: (i, kr_block)),
                  full(qn), full(kvn), full(wq), full(wkv), full(hgq), full(gn), full(gr), tab, tab],
        out_specs=(pl.BlockSpec((tm, NQ), lambda i: (i, 0)), pl.BlockSpec((tm, NQ), lambda i: (i, 0)),
                   pl.BlockSpec((tm, MLA_HEADS * MLA_V), lambda i: (i, 0))),
        compiler_params=_cparams(("parallel",)),
        name="mla_qkv",
    )(p, p, p, qn, kvn, wq, wkv, hgq, gn, gr, cos, sin)


def _flash_rows(s, v1, rows, m_sc, acc_sc):
    m_prev = m_sc[rows, :]
    m_new = jnp.maximum(m_prev, jnp.max(s, axis=-1, keepdims=True))
    alpha = jnp.exp2(m_prev - m_new)
    pr = jnp.exp2((s - jnp.concatenate([m_new] * (s.shape[1] // LANES), axis=1)).astype(BF16))
    acc_sc[rows, :] = (jnp.concatenate([alpha, alpha], axis=1) * acc_sc[rows, :]
                       + jnp.dot(pr, v1, preferred_element_type=F32))
    m_sc[rows, :] = m_new


def _flash_init(m_sc, acc_sc):
    m_sc[...] = jnp.full(m_sc.shape, -jnp.inf, F32)
    acc_sc[...] = jnp.zeros(acc_sc.shape, F32)


def _with_ones(v):
    return jnp.concatenate([v, jnp.ones(v.shape, v.dtype)], axis=1)


def _flash_finish(acc_sc, o_ref):
    acc = acc_sc[...]
    d = acc.shape[1] // 2
    o_ref[...] = (acc[:, :d] / acc[:, d:]).astype(o_ref.dtype)


def _mla_attn_kernel(qi_ref, ki_ref, q_ref, k_ref, v_ref, o_ref, m_sc, acc_sc, *, sb):
    p = pl.program_id(1)
    qi = qi_ref[p]
    ki = ki_ref[p]
    t = q_ref.shape[0]

    @pl.when(ki == 0)
    def _():
        _flash_init(m_sc, acc_sc)

    @pl.when(ki < qi)
    def _():
        v1 = _with_ones(v_ref[...])
        for rb in range(t // sb):
            rows = slice(rb * sb, (rb + 1) * sb)
            s = _dot_nt(q_ref[rows, :], k_ref[...])
            _flash_rows(s, v1, rows, m_sc, acc_sc)

    @pl.when(ki == qi)
    def _():
        v1 = _with_ones(v_ref[...])
        for rb in range(t // sb):
            rows = slice(rb * sb, (rb + 1) * sb)
            nc = (rb + 1) * sb
            s = _dot_nt(q_ref[rows, :], k_ref[0:nc, :])
            row = lax.broadcasted_iota(jnp.int32, s.shape, 0) + rb * sb
            col = lax.broadcasted_iota(jnp.int32, s.shape, 1)
            s = jnp.where(col <= row, s, -jnp.inf)
            _flash_rows(s, v1[0:nc, :], rows, m_sc, acc_sc)
        _flash_finish(acc_sc, o_ref)


def _mla_attention(q, k, v, *, t=2048, sb=256):
    S = q.shape[0]
    t = min(t, S)
    sb = min(sb, t)
    nb = S // t
    pairs = [(a, b) for a in range(nb) for b in range(a + 1)]
    qi_tab = jnp.asarray(np.array([a for a, _ in pairs], np.int32))
    ki_tab = jnp.asarray(np.array([b for _, b in pairs], np.int32))
    stat = pltpu.VMEM((t, LANES), F32)
    grid_spec = pltpu.PrefetchScalarGridSpec(
        num_scalar_prefetch=2,
        grid=(MLA_HEADS, len(pairs)),
        in_specs=[pl.BlockSpec((t, MLA_QK_PAD), lambda h, p, qt, kt: (qt[p], h)),
                  pl.BlockSpec((t, MLA_QK_PAD), lambda h, p, qt, kt: (kt[p], h)),
                  pl.BlockSpec((t, MLA_V), lambda h, p, qt, kt: (kt[p], h))],
        out_specs=pl.BlockSpec((t, MLA_V), lambda h, p, qt, kt: (qt[p], h)),
        scratch_shapes=[stat, pltpu.VMEM((t, 2 * MLA_V), F32)],
    )
    return pl.pallas_call(
        functools.partial(_mla_attn_kernel, sb=sb),
        out_shape=jax.ShapeDtypeStruct((S, MLA_HEADS * MLA_V), BF16),
        grid_spec=grid_spec,
        compiler_params=_cparams(("parallel", "arbitrary")),
        name="mla_attention",
    )(qi_tab, ki_tab, q, k, v)


def _seg_sum(x, e_ref):
    return _split_dot(x, e_ref[...])


def _bmm(a, b):
    return jnp.einsum('gik,gkj->gij', a.astype(BF16), b.astype(BF16), preferred_element_type=F32)


def _bmm_nt(a, b):
    return jnp.einsum('gik,gjk->gij', a.astype(BF16), b.astype(BF16), preferred_element_type=F32)


def _bsplit_cumsum(tri, x):
    G = x.shape[0]
    tb = jnp.broadcast_to(tri.astype(BF16)[None], (G,) + tri.shape)
    x0 = x.astype(BF16)
    r1 = x - x0.astype(F32)
    x1 = r1.astype(BF16)
    x2 = (r1 - x1.astype(F32)).astype(BF16)
    f = lambda t: jnp.einsum('gts,gsl->gtl', tb, t, preferred_element_type=F32)
    return f(x0) + f(x1) + f(x2)


def _rwkv_kernel(pr_ref, pk_ref, pv_ref, px_ref, qr_ref, qk_ref, qv_ref, qx_ref, mur_ref, muk_ref, muv_ref, mux_ref,
                 w0_ref, w2_ref, a0_ref, a2_ref, g2_ref, kk_ref, ka_ref, lnw_ref, lnb_ref, rk_ref, tri_ref, e_ref,
                 o_ref, st_sc, sall_sc, kc_sc, nc_sc, pc_sc, *, hg, G):
    C = RWKV_CHUNK
    LW = hg * RWKV_HEAD
    RW = hg * C
    c_idx = pl.program_id(1)

    @pl.when(c_idx == 0)
    def _():
        st_sc[...] = jnp.zeros(st_sc.shape, F32)

    row = lax.broadcasted_iota(jnp.int32, (G * C, 1), 0)

    def shift_mix(x_ref, prev_ref, mu_ref):
        x = x_ref[...]
        last = jnp.where(c_idx == 0, 0.0, prev_ref[7:8, :])
        prev = jnp.where(row == 0, last, pltpu.roll(x, 1, 0))
        return x + (prev - x) * mu_ref[...]

    r = shift_mix(pr_ref, qr_ref, mur_ref)
    kb = shift_mix(pk_ref, qk_ref, muk_ref)
    v = shift_mix(pv_ref, qv_ref, muv_ref)
    xx = shift_mix(px_ref, qx_ref, mux_ref)
    lw = -EXP_NEG_HALF * _sigmoid(w0_ref[...] + _dot(jnp.tanh(xx[:, :LANES]), w2_ref[...]))
    a = _sigmoid(a0_ref[...] + _dot(xx[:, LANES:2 * LANES], a2_ref[...]))
    g = _dot(_sigmoid(xx[:, 2 * LANES:]), g2_ref[...])
    kk = kb * kk_ref[...]
    kkn = kk * lax.rsqrt(jnp.maximum(_seg_sum(kk * kk, e_ref), 1e-24))
    k = kb * (1.0 + (a - 1.0) * ka_ref[...])
    b = kkn * a

    lw3 = lw.reshape(G, C, LW)
    cum3 = _bsplit_cumsum(tri_ref[...], lw3)
    clast = cum3[:, C - 1:C, :]
    r3 = r.reshape(G, C, LW)
    k3 = k.reshape(G, C, LW)
    v3 = v.reshape(G, C, LW)
    kk3 = kkn.reshape(G, C, LW)
    b3 = b.reshape(G, C, LW)
    einv = jnp.exp(-cum3)
    pc = jnp.exp(clast)
    etail = pc * einv
    rt = r3 * jnp.exp(cum3)
    at = -kk3 * jnp.exp(cum3 - lw3)
    bt = b3 * einv
    kt = k3 * einv
    bh = b3 * etail
    kh = k3 * etail

    lane_head = lax.broadcasted_iota(jnp.int32, (1, 1, LW), 2) // RWKV_HEAD

    def stack(x):
        return jnp.concatenate([jnp.where(lane_head == h, x, 0.0) for h in range(hg)], axis=1)

    a2, b2, k2, r2, v2, bh2, kh2 = (stack(t).astype(BF16) for t in (at, bt, kt, rt, v3, bh, kh))
    ri = lax.broadcasted_iota(jnp.int32, (1, RW, RW), 1)
    ci = lax.broadcasted_iota(jnp.int32, (1, RW, RW), 2)
    same = (ri // C) == (ci // C)
    strict = same & (ci < ri)
    incl = same & (ci <= ri)
    eye = (ri == ci).astype(F32)

    bk2 = jnp.concatenate([b2, k2], axis=1)
    a_bk = _bmm_nt(a2, bk2)
    r_bk = _bmm_nt(r2, bk2)
    lmat = jnp.where(strict, a_bk[:, :, :RW], 0.0)
    akm = jnp.where(strict, a_bk[:, :, RW:], 0.0)
    rbm = jnp.where(incl, r_bk[:, :, :RW], 0.0)
    rkm = jnp.where(incl, r_bk[:, :, RW:], 0.0)
    tinv = eye + lmat
    pw = lmat
    for _ in range(int(np.log2(C)) - 1):
        pw = _bmm(pw, pw)
        tinv = tinv + _bmm(tinv, pw)
    wm = _bmm(tinv, a2)
    z = _bmm(tinv, _bmm(akm, v2))
    qm = r2.astype(F32) + _bmm(rbm, wm)
    y0 = _bmm(rkm, v2) + _bmm(rbm, z)
    kc_sc[...] = _bmm(jnp.swapaxes(wm, 1, 2), bh2)
    nc_sc[...] = _bmm(jnp.swapaxes(z, 1, 2), bh2) + _bmm(jnp.swapaxes(v2, 1, 2), kh2)
    pc_sc[...] = pc

    def chunk(c, carry):
        st = st_sc[...]
        sall_sc[c] = st
        st_sc[...] = st * pc_sc[c] + _dot(st, kc_sc[c]) + nc_sc[c]
        return carry

    lax.fori_loop(0, G, chunk, 0, unroll=True)

    y2 = _bmm_nt(qm, sall_sc[...]) + y0
    y = y2[:, 0:C]
    for h in range(1, hg):
        y = y + y2[:, h * C:(h + 1) * C]

    y = y.reshape(G * C, LW)
    inv_n = 1.0 / RWKV_HEAD
    mu = _seg_sum(y, e_ref) * inv_n
    d = y - mu
    var = _seg_sum(d * d, e_ref) * inv_n
    yn = d * lax.rsqrt(var + RWKV_GN_EPS) * lnw_ref[...] + lnb_ref[...]
    bonus = _seg_sum(r * k * rk_ref[...], e_ref) * v
    o_ref[...] = ((yn + bonus) * g).astype(o_ref.dtype)


def _rwkv_mix(p, mu, w0, w2, a0, a2, g2, k_k, k_a, ln_w, ln_b, r_k, *, hg=2, G=16):
    S = p.shape[0]
    C = RWKV_CHUNK
    G = min(G, S // C)
    TB = G * C
    LW = hg * RWKV_HEAD
    W = RWKV_WIDTH
    ngrp = W // LW
    XW = RWKV_SEC - 3 * W
    t = np.arange(C)
    tri = jnp.asarray((t[None, :] <= t[:, None]).astype(np.float32), BF16)
    l = np.arange(LW)
    e = jnp.asarray((l[:, None] // RWKV_HEAD == l[None, :] // RWKV_HEAD).astype(np.float32), BF16)

    def cols(width, first):
        blk = pl.BlockSpec((TB, width), lambda gi, c: (c, first(gi)))
        prev = pl.BlockSpec((8, width), lambda gi, c: (jnp.maximum(c * (TB // 8) - 1, 0), first(gi)))
        par = pl.BlockSpec((1, width), lambda gi, c: (0, first(gi)))
        return blk, prev, par

    rb, rp, rm = cols(LW, lambda gi: gi)
    kb, kp, km = cols(LW, lambda gi: ngrp + gi)
    vb, vp, vm = cols(LW, lambda gi: 2 * ngrp + gi)
    xb, xp, xm = cols(XW, lambda gi: 3 * W // XW)
    row = pl.BlockSpec((1, LW), lambda gi, c: (0, gi))
    lora = lambda n: pl.BlockSpec((n, LW), lambda gi, c: (0, gi))
    sq = lambda: pltpu.VMEM((G, LW, LW), F32)
    return pl.pallas_call(
        functools.partial(_rwkv_kernel, hg=hg, G=G),
        out_shape=jax.ShapeDtypeStruct((S, W), BF16),
        grid=(ngrp, S // TB),
        in_specs=[rb, kb, vb, xb, rp, kp, vp, xp, rm, km, vm, xm,
                  row, lora(LANES), row, lora(LANES), lora(2 * LANES), row, row, row, row, row,
                  pl.BlockSpec((C, C), lambda gi, c: (0, 0)), pl.BlockSpec((LW, LW), lambda gi, c: (0, 0))],
        out_specs=pl.BlockSpec((TB, LW), lambda gi, c: (c, gi)),
        scratch_shapes=[pltpu.VMEM((LW, LW), F32), sq(), sq(), sq(), pltpu.VMEM((G, 1, LW), F32)],
        compiler_params=_cparams(("parallel", "arbitrary")),
        name="rwkv_mix",
    )(p, p, p, p, p, p, p, p, mu, mu, mu, mu, w0, w2, a0, a2, g2, k_k, k_a, ln_w, ln_b, r_k, tri, e)


def _pad_cols(w, width):
    return jnp.pad(w, ((0, 0), (0, width - w.shape[1])))


def _rope_gap_layout(w):
    z = jnp.zeros(w.shape[:-1] + (32,), w.dtype)
    return jnp.concatenate([w[..., :32], z, w[..., 32:], z], axis=-1)


def _even_mixer(hn, tabs, w_in, q_norm, w_uq, kv_norm, w_ukv, q_head_norm, k_head_norm,
                mu, w0, w2, a0, a2, g2, k_k, k_a, r_k, ln_w, ln_b, w_out):
    cos1, sin1 = tabs[0], tabs[1]
    W = RWKV_WIDTH
    o_cq = 0
    o_ckv = MLA_Q_LORA
    o_kr = o_ckv + MLA_KV_LORA
    o_rw = o_kr + MLA_ROPE
    w_in = w_in.astype(BF16)
    rw = w_in[:, o_rw:]

    def rwkv_layout(t):
        return jnp.concatenate([t[:, :3 * W], _pad_cols(t[:, 3 * W:3 * W + RWKV_W_LORA], LANES),
                                _pad_cols(t[:, 3 * W + RWKV_W_LORA:3 * W + RWKV_W_LORA + RWKV_A_LORA], LANES),
                                _pad_cols(t[:, 3 * W + RWKV_W_LORA + RWKV_A_LORA:], 2 * LANES)], axis=1)

    w_in_p = jnp.concatenate([rwkv_layout(rw), w_in[:, o_cq:o_ckv], w_in[:, o_ckv:o_kr],
                              _rope_gap_layout(w_in[:, o_kr:o_rw])], axis=1)
    w_in_p = _pad_cols(w_in_p, 4608)
    p = _proj(hn, w_in_p, tn=1536)

    wq = w_uq.astype(BF16).reshape(MLA_Q_LORA, MLA_HEADS, MLA_QK)
    wq = jnp.concatenate([wq[..., :MLA_NOPE], _rope_gap_layout(wq[..., MLA_NOPE:])], axis=-1)
    wq = wq.reshape(MLA_Q_LORA, MLA_HEADS * MLA_QK_PAD)
    scale = MLA_QK ** -0.5 * LOG2E
    hg_q = jnp.concatenate([q_head_norm[:MLA_NOPE], _rope_gap_layout(q_head_norm[MLA_NOPE:])])[None, :] * scale
    gn = k_head_norm[None, :MLA_NOPE]
    gr = _rope_gap_layout(k_head_norm[MLA_NOPE:])[None, :]
    q, kmat, vmat = _mla_qkv(p, q_norm[None, :], kv_norm[None, :], wq, w_ukv.astype(BF16), hg_q, gn, gr, cos1, sin1)
    o_a = _mla_attention(q, kmat, vmat)

    w2p = jnp.pad(w2, ((0, LANES - RWKV_W_LORA), (0, 0))).astype(BF16)
    a2p = jnp.pad(a2, ((0, LANES - RWKV_A_LORA), (0, 0))).astype(BF16)
    g2p = jnp.pad(g2, ((0, 2 * LANES - RWKV_G_LORA), (0, 0))).astype(BF16)
    o_b = _rwkv_mix(p, rwkv_layout(mu[None, :]), w0[None, :], w2p, a0[None, :], a2p, g2p, k_k[None, :], k_a[None, :],
                    ln_w[None, :], ln_b[None, :], r_k.reshape(1, W))
    return o_a, o_b, w_out.astype(BF16)


def _swa_qkv_kernel(x_ref, w_ref, hg_ref, cos_ref, sin_ref, o_ref):
    n = pl.program_id(0)
    half = x_ref.shape[0] // 2

    @pl.when(n < 2)
    def _():
        for r0 in (0, half):
            rows = slice(r0, r0 + half)
            acc = jnp.dot(x_ref[rows, :], w_ref[...], preferred_element_type=F32)
            cos = cos_ref[rows, :]
            sin = sin_ref[rows, :]
            for c in range(acc.shape[1] // LANES):
                sl = slice(c * LANES, (c + 1) * LANES)
                o_ref[rows, sl] = _rope_apply(_rms(acc[:, sl], hg_ref[:, sl]), cos, sin).astype(o_ref.dtype)

    @pl.when(n == 2)
    def _():
        o_ref[...] = jnp.dot(x_ref[...], w_ref[...], preferred_element_type=F32).astype(o_ref.dtype)


def _swa_qkv(x, w, hg, cos, sin, *, tm=1024):
    S, K = x.shape
    tn = SWA_HEADS * SWA_DIM
    tm = min(tm, S)
    tab = pl.BlockSpec((tm, LANES), lambda n, i: (i, 0))
    return pl.pallas_call(
        _swa_qkv_kernel,
        out_shape=jax.ShapeDtypeStruct((S, 3 * tn), BF16),
        grid=(3, S // tm),
        in_specs=[pl.BlockSpec((tm, K), lambda n, i: (i, 0)),
                  pl.BlockSpec((K, tn), lambda n, i: (0, n)),
                  pl.BlockSpec((1, tn), lambda n, i: (0, jnp.minimum(n, 1))),
                  tab, tab],
        out_specs=pl.BlockSpec((tm, tn), lambda n, i: (i, n)),
        compiler_params=_cparams(("parallel", "parallel")),
        name="swa_qkv",
    )(x, w, hg, cos, sin)


def _dilated_bias(t, nrel):
    row = np.arange(t)[:, None]
    col = np.arange(t)[None, :]
    out = np.empty((nrel, t, t), np.float32)
    for r in range(nrel):
        delta = r * t + row - col
        cnt = np.zeros((t, t), np.float64)
        for window, dilation in SWA_PATTERNS:
            cnt += (delta >= 0) & (delta <= window) & (delta % dilation == 0)
        with np.errstate(divide="ignore"):
            out[r] = np.log2(cnt)
    return out


def _dilated_kernel(q_ref, k_ref, v_ref, bias_ref, o_ref, m_sc, acc_sc, *, sb, nrel, live):
    qi = pl.program_id(1)
    r = pl.program_id(2)
    t = q_ref.shape[0]

    @pl.when(r == 0)
    def _():
        _flash_init(m_sc, acc_sc)

    for rr in range(nrel):
        @pl.when((r == rr) & (qi - r >= 0))
        def _(rr=rr):
            v1 = _with_ones(v_ref[...])
            for rb in range(t // sb):
                rows = slice(rb * sb, (rb + 1) * sb)
                lo, hi = live[rr][rb]
                s = _dot_nt(q_ref[rows, :], k_ref[lo:hi, :]) + bias_ref[rr, rows, lo:hi]
                _flash_rows(s, v1[lo:hi, :], rows, m_sc, acc_sc)

    @pl.when(r == nrel - 1)
    def _():
        _flash_finish(acc_sc, o_ref)


def _dilated_attention(qkv, *, t=1024, sb=256):
    S = qkv.shape[0]
    t = min(t, S)
    sb = min(sb, t)
    max_window = max(w for w, _ in SWA_PATTERNS)
    nrel = min(-(-max_window // t) + 1, S // t)
    H = SWA_HEADS
    bias = _dilated_bias(t, nrel)
    live = []
    for r in range(nrel):
        per_rb = []
        for rb in range(t // sb):
            cols = np.nonzero(np.isfinite(bias[r, rb * sb:(rb + 1) * sb]).any(axis=0))[0]
            lo = int(cols.min()) // LANES * LANES
            hi = -(-(int(cols.max()) + 1) // LANES) * LANES
            per_rb.append((lo, hi))
        live.append(per_rb)
    stat = pltpu.VMEM((t, LANES), F32)
    return pl.pallas_call(
        functools.partial(_dilated_kernel, sb=sb, nrel=nrel, live=live),
        out_shape=jax.ShapeDtypeStruct((S, H * SWA_DIM), BF16),
        grid=(H, S // t, nrel),
        in_specs=[pl.BlockSpec((t, SWA_DIM), lambda h, i, r: (i, h)),
                  pl.BlockSpec((t, SWA_DIM), lambda h, i, r: (jnp.maximum(i - r, 0), H + h)),
                  pl.BlockSpec((t, SWA_DIM), lambda h, i, r: (jnp.maximum(i - r, 0), 2 * H + h)),
                  pl.BlockSpec((nrel, t, t), lambda h, i, r: (0, 0, 0))],
        out_specs=pl.BlockSpec((t, SWA_DIM), lambda h, i, r: (i, h)),
        scratch_shapes=[stat, pltpu.VMEM((t, 2 * SWA_DIM), F32)],
        compiler_params=_cparams(("parallel", "parallel", "arbitrary")),
        name="dilated_attention",
    )(qkv, qkv, qkv, jnp.asarray(bias))


def _gla_kernel(q_ref, k_ref, v_ref, glr_ref, wgu_ref, bg_ref, rd_ref, gn_ref, tri_ref, o_ref,
                st_sc, sall_sc, oi_sc, *, G):
    C = GLA_CHUNK
    SB = GLA_SUB
    DK = GLA_DK
    DV = GLA_DV
    c_idx = pl.program_id(1)

    @pl.when(c_idx == 0)
    def _():
        st_sc[...] = jnp.zeros(st_sc.shape, F32)

    z = _dot(glr_ref[...], wgu_ref[...]) + bg_ref[...]
    lg = -_softplus(-z) * (1.0 / GLA_NORMALIZER)
    bc = _bsplit_cumsum(tri_ref[...], lg.reshape(G, C, DK))
    q = (q_ref[...] * (DK ** -0.5)).reshape(G, C, DK)
    k = k_ref[...].reshape(G, C, DK)
    v = v_ref[...].reshape(G, C, DV)
    blast = bc[:, C - 1:C, :]
    khat = k * jnp.exp(blast - bc)
    gkv = _bmm(jnp.swapaxes(v, 1, 2), khat)
    eb = jnp.exp(blast)

    st = st_sc[...]
    for c in range(G):
        sall_sc[c] = st
        st = st * eb[c] + gkv[c]
    st_sc[...] = st

    ti = lax.broadcasted_iota(jnp.int32, (1, SB, SB, 1), 1)
    si = lax.broadcasted_iota(jnp.int32, (1, SB, SB, 1), 2)
    causal = si <= ti
    for sb in range(C // SB):
        lo = sb * SB
        qs = q[:, lo:lo + SB]
        ks = k[:, lo:lo + SB]
        bs = bc[:, lo:lo + SB]
        rel = bs[:, :, None, :] - bs[:, None, :, :]
        dec = jnp.exp(jnp.where(causal, rel, -jnp.inf))
        att = jnp.sum(qs[:, :, None, :] * ks[:, None, :, :] * dec, axis=-1)
        o_sb = _bmm(att, v[:, lo:lo + SB])
        if sb > 0:
            bm = bc[:, lo - 1:lo]
            qsc = qs * jnp.exp(bs - bm)
            ksc = k[:, :lo] * jnp.exp(bm - bc[:, :lo])
            o_sb = o_sb + _bmm(_bmm_nt(qsc, ksc), v[:, :lo])
        oi_sc[:, lo:lo + SB, :] = o_sb

    o = (oi_sc[...] + _bmm_nt(q * jnp.exp(bc), sall_sc[...])).reshape(G * C, DV)
    rd = rd_ref[...]
    o_ref[...] = (_rms(o, gn_ref[...]) * (rd * _sigmoid(rd))).astype(o_ref.dtype)


def _gla(pb, wgu, bg, gn, *, G=16):
    S = pb.shape[0]
    C = GLA_CHUNK
    G = min(G, S // C)
    TB = G * C
    H, DK, DV = GLA_HEADS, GLA_DK, GLA_DV
    t = np.arange(C)
    tri = jnp.asarray((t[None, :] <= t[:, None]).astype(np.float32), BF16)
    glr_block = (2 * H * DK + 2 * H * DV) // LANES
    return pl.pallas_call(
        functools.partial(_gla_kernel, G=G),
        out_shape=jax.ShapeDtypeStruct((S, H * DV), BF16),
        grid=(H, S // TB),
        in_specs=[pl.BlockSpec((TB, DK), lambda h, c: (c, h)),
                  pl.BlockSpec((TB, DK), lambda h, c: (c, H + h)),
                  pl.BlockSpec((TB, DV), lambda h, c: (c, (2 * H * DK) // DV + h)),
                  pl.BlockSpec((TB, LANES), lambda h, c: (c, glr_block)),
                  pl.BlockSpec((LANES, DK), lambda h, c: (0, h)),
                  pl.BlockSpec((1, DK), lambda h, c: (0, h)),
                  pl.BlockSpec((TB, DV), lambda h, c: (c, (2 * H * DK + H * DV) // DV + h)),
                  pl.BlockSpec((1, DV), lambda h, c: (0, h)),
                  pl.BlockSpec((C, C), lambda h, c: (0, 0))],
        out_specs=pl.BlockSpec((TB, DV), lambda h, c: (c, h)),
        scratch_shapes=[pltpu.VMEM((DV, DK), F32), pltpu.VMEM((G, DV, DK), F32), pltpu.VMEM((G, C, DV), F32)],
        compiler_params=_cparams(("parallel", "arbitrary")),
        name="gla",
    )(pb, pb, pb, pb, wgu, bg, pb, gn, tri)


def _odd_mixer(hn, tabs, w_in, q_head_norm, k_head_norm, w_gate_up, b_gate, gla_norm, w_out):
    cos2, sin2 = tabs[2], tabs[3]
    nq = SWA_HEADS * SWA_DIM
    hg = jnp.concatenate([jnp.tile(q_head_norm * (SWA_DIM ** -0.5 * LOG2E), SWA_HEADS), jnp.tile(k_head_norm, SWA_HEADS)])[None, :]
    w_in = w_in.astype(BF16)
    qkv = _swa_qkv(hn, w_in, hg, cos2, sin2)
    o_c = _dilated_attention(qkv)
    o = 3 * nq
    dk, dv = GLA_HEADS * GLA_DK, GLA_HEADS * GLA_DV
    wb = jnp.concatenate([w_in[:, o:o + 2 * dk + dv], w_in[:, o + 2 * dk + dv + GLA_LORA:],
                          _pad_cols(w_in[:, o + 2 * dk + dv:o + 2 * dk + dv + GLA_LORA], LANES)], axis=1)
    wb = _pad_cols(wb, 3584)
    pb = _proj(hn, wb, tn=1792)
    wgu = jnp.pad(w_gate_up, ((0, LANES - GLA_LORA), (0, 0))).astype(BF16)
    o_d = _gla(pb, wgu, b_gate[None, :], gla_norm.reshape(1, dv))
    return o_c, o_d, w_out.astype(BF16)


def kernel(x, positions, ffn_norm, ffn_w_gate, ffn_w_up, ffn_w_down, mix_norm, mla_rwkv_w_in, mla_q_norm, mla_w_uq,
           mla_kv_norm, mla_w_ukv, mla_q_head_norm, mla_k_head_norm, rwkv_mu, rwkv_w0, rwkv_w2, rwkv_a0, rwkv_a2,
           rwkv_g2, rwkv_k_k, rwkv_k_a, rwkv_r_k, rwkv_ln_w, rwkv_ln_b, mla_rwkv_w_out, swa_gla_w_in,
           swa_q_head_norm, swa_k_head_norm, gla_w_gate_up, gla_b_gate, gla_norm, swa_gla_w_out):
    B, S, D = x.shape
    assert B == 1
    h = x.reshape(S, D)
    tabs = _rope_tables(positions.reshape(S))
    wg = ffn_w_gate.astype(BF16)
    wu = ffn_w_up.astype(BF16)
    wd = ffn_w_down.astype(BF16)
    depth = ffn_norm.shape[0]
    for layer in range(depth):
        i = layer // 2
        h, hn = _ffn(h, ffn_norm, wg, wu, wd, layer, 0, next_norm=mix_norm[layer])
        if layer % 2 == 0:
            mix = _even_mixer(hn, tabs, mla_rwkv_w_in[i], mla_q_norm[i], mla_w_uq[i], mla_kv_norm[i],
                            mla_w_ukv[i], mla_q_head_norm[i], mla_k_head_norm[i], rwkv_mu[i], rwkv_w0[i],
                            rwkv_w2[i], rwkv_a0[i], rwkv_a2[i], rwkv_g2[i], rwkv_k_k[i], rwkv_k_a[i], rwkv_r_k[i],
                            rwkv_ln_w[i], rwkv_ln_b[i], mla_rwkv_w_out[i])
        else:
            mix = _odd_mixer(hn, tabs, swa_gla_w_in[i], swa_q_head_norm[i], swa_k_head_norm[i],
                           gla_w_gate_up[i], gla_b_gate[i], gla_norm[i], swa_gla_w_out[i])
        h = _ffn(h, ffn_norm, wg, wu, wd, layer, 1, mix=mix)
    return h.reshape(B, S, D)
```

```python
import functools

import numpy as np
import jax
import jax.numpy as jnp
from jax import lax
from jax.experimental import pallas as pl
from jax.experimental.pallas import tpu as pltpu

F32 = jnp.float32
BF16 = jnp.bfloat16

LANES = 128
VMEM_LIMIT = 56 * 1024 * 1024

D_MODEL = 2048
D_FF = 5632
MACARON_WEIGHT = 0.5
NORM_EPS = 1e-6
ROPE_THETA = 10000.0
LOG2E = float(np.log2(np.e))
EXP_NEG_HALF = float(np.exp(-0.5))

MLA_HEADS = 8
MLA_NOPE = 128
MLA_ROPE = 64
MLA_QK = MLA_NOPE + MLA_ROPE
MLA_QK_PAD = 256
MLA_V = 128
MLA_Q_LORA = 512
MLA_KV_LORA = 256

RWKV_HEAD = 64
RWKV_WIDTH = 1024
RWKV_HEADS = 16
RWKV_W_LORA = 64
RWKV_A_LORA = 64
RWKV_G_LORA = 160
RWKV_GN_EPS = 64e-5
RWKV_SEC = 3584
RWKV_CHUNK = 64

SWA_HEADS = 8
SWA_DIM = 128
SWA_PATTERNS = ((128, 1), (512, 4), (2048, 16))

GLA_HEADS = 4
GLA_DK = 128
GLA_DV = 256
GLA_LORA = 16
GLA_NORMALIZER = 16.0
GLA_CHUNK = 64
GLA_SUB = 8


def _cparams(sem):
    return pltpu.CompilerParams(dimension_semantics=sem, vmem_limit_bytes=VMEM_LIMIT)


def _rms(x, g):
    return x * lax.rsqrt(jnp.mean(x * x, axis=-1, keepdims=True) + NORM_EPS) * g


def _dot(a, b):
    return jnp.dot(a.astype(BF16), b.astype(BF16), preferred_element_type=F32)


def _dot_nt(a, b):
    return lax.dot_general(a.astype(BF16), b.astype(BF16), (((1,), (1,)), ((), ())),
                           preferred_element_type=F32)


def _split_dot(a, b):
    a0 = a.astype(BF16)
    a1 = (a - a0.astype(F32)).astype(BF16)
    b = b.astype(BF16)
    return jnp.dot(a0, b, preferred_element_type=F32) + jnp.dot(a1, b, preferred_element_type=F32)


def _sigmoid(x):
    return 0.5 * jnp.tanh(0.5 * x) + 0.5


def _softplus(x):
    return jnp.maximum(x, 0.0) + jnp.log(1.0 + jnp.exp(-jnp.abs(x)))


def _rope_apply(x, cos, sin_signed):
    return x * cos + pltpu.roll(x, 64, 1) * sin_signed


def _ffn_kernel(*refs, emit_norm, mix):
    it = iter(refs)
    h_ref, g_ref, wg_ref, wu_ref, wd_ref = (next(it) for _ in range(5))
    if mix:
        a_ref, b_ref, wa_ref, wb_ref = (next(it) for _ in range(4))
    if emit_norm:
        g2_ref = next(it)
    o_ref = next(it)
    if emit_norm:
        hn_ref = next(it)
    xn_ref = next(it)
    j = pl.program_id(1)

    @pl.when(j == 0)
    def _():
        x = h_ref[...]
        if mix:
            x = (x + jnp.dot(a_ref[...], wa_ref[...], preferred_element_type=F32)
                 + jnp.dot(b_ref[...], wb_ref[...], preferred_element_type=F32))
        xn_ref[...] = _rms(x, g_ref[...]).astype(BF16)
        o_ref[...] = x

    xn = xn_ref[...]
    gate = jnp.dot(xn, wg_ref[...], preferred_element_type=F32)
    up = jnp.dot(xn, wu_ref[...], preferred_element_type=F32)
    act = (gate * _sigmoid(gate) * up * MACARON_WEIGHT).astype(BF16)
    o_ref[...] += jnp.dot(act, wd_ref[...], preferred_element_type=F32)

    if emit_norm:
        @pl.when(j == pl.num_programs(1) - 1)
        def _():
            hn_ref[...] = _rms(o_ref[...], g2_ref[...]).astype(hn_ref.dtype)


def _ffn(h, g, wg, wu, wd, layer, k, next_norm=None, mix=None, *, tm=512, tf=512):
    S, D = h.shape
    F = wg.shape[-1]
    tm = min(tm, S)
    emit_norm = next_norm is not None
    wspec = pl.BlockSpec((None, None, D, tf), lambda i, j: (layer, k, 0, j))
    row = pl.BlockSpec((1, D), lambda i, j: (0, 0))
    tile = pl.BlockSpec((tm, D), lambda i, j: (i, 0))
    in_specs = [tile, row, wspec, wspec, pl.BlockSpec((None, None, tf, D), lambda i, j: (layer, k, j, 0))]
    args = [h, g[layer, k][None, :], wg, wu, wd]
    if mix is not None:
        a, b, w_out = mix
        ka = a.shape[1]
        assert b.shape[1] == ka and w_out.shape == (2 * ka, D)
        half = pl.BlockSpec((tm, ka), lambda i, j: (i, 0))
        in_specs += [half, half, pl.BlockSpec((ka, D), lambda i, j: (0, 0)), pl.BlockSpec((ka, D), lambda i, j: (1, 0))]
        args += [a, b, w_out, w_out]
    out_shape = jax.ShapeDtypeStruct((S, D), F32)
    out_specs = tile
    if emit_norm:
        in_specs.append(row)
        args.append(next_norm[None, :])
        out_shape = (out_shape, jax.ShapeDtypeStruct((S, D), BF16))
        out_specs = (tile, tile)
    return pl.pallas_call(
        functools.partial(_ffn_kernel, emit_norm=emit_norm, mix=mix is not None),
        out_shape=out_shape,
        grid=(S // tm, F // tf),
        in_specs=in_specs,
        out_specs=out_specs,
        scratch_shapes=[pltpu.VMEM((tm, D), BF16)],
        compiler_params=_cparams(("parallel", "arbitrary")),
        name="ffn",
    )(*args)


def _rope_tab_kernel(pos_ref, f1_ref, s1_ref, f2_ref, s2_ref, c1_ref, n1_ref, c2_ref, n2_ref):
    pos = pos_ref[...].astype(F32)
    a1 = pos * f1_ref[...]
    c1_ref[...] = jnp.cos(a1)
    n1_ref[...] = jnp.sin(a1) * s1_ref[...]
    a2 = pos * f2_ref[...]
    c2_ref[...] = jnp.cos(a2)
    n2_ref[...] = jnp.sin(a2) * s2_ref[...]


def _rope_tables(positions, *, tm=1024):
    S = positions.shape[0]
    tm = min(tm, S)
    f32half = ROPE_THETA ** (-jnp.arange(MLA_ROPE // 2, dtype=F32) / (MLA_ROPE // 2))
    z = jnp.zeros((32,), F32)
    f1 = jnp.concatenate([f32half, z, f32half, z])[None]
    f64half = ROPE_THETA ** (-jnp.arange(SWA_DIM // 2, dtype=F32) / (SWA_DIM // 2))
    f2 = jnp.concatenate([f64half, f64half])[None]
    sign = jnp.concatenate([-jnp.ones((64,), F32), jnp.ones((64,), F32)])[None]
    row = pl.BlockSpec((1, LANES), lambda i: (0, 0))
    tab = pl.BlockSpec((tm, LANES), lambda i: (i, 0))
    shp = jax.ShapeDtypeStruct((S, LANES), F32)
    return pl.pallas_call(
        _rope_tab_kernel,
        out_shape=(shp, shp, shp, shp),
        grid=(S // tm,),
        in_specs=[pl.BlockSpec((tm, 1), lambda i: (i, 0)), row, row, row, row],
        out_specs=(tab, tab, tab, tab),
        compiler_params=_cparams(("parallel",)),
        name="rope_tables",
    )(positions.reshape(S, 1), f1, sign, f2, sign)


def _proj_kernel(x_ref, w_ref, o_ref):
    o_ref[...] = jnp.dot(x_ref[...], w_ref[...], preferred_element_type=F32).astype(o_ref.dtype)


def _proj(x, w, *, tm=1024, tn, out_dtype=F32):
    S, K = x.shape
    N = w.shape[1]
    tm = min(tm, S)
    return pl.pallas_call(
        _proj_kernel,
        out_shape=jax.ShapeDtypeStruct((S, N), out_dtype),
        grid=(N // tn, S // tm),
        in_specs=[pl.BlockSpec((tm, K), lambda n, i: (i, 0)),
                  pl.BlockSpec((K, tn), lambda n, i: (0, n))],
        out_specs=pl.BlockSpec((tm, tn), lambda n, i: (i, n)),
        compiler_params=_cparams(("parallel", "parallel")),
        name="proj",
    )(x, w)


def _mla_qkv_kernel(cq_ref, ckv_ref, kr_ref, qn_ref, kvn_ref, wq_ref, wkv_ref, hgq_ref, gn_ref, gr_ref,
                    cos_ref, sin_ref, q_ref, k_ref, v_ref):
    cos = cos_ref[...]
    sin = sin_ref[...]
    xq = _rms(cq_ref[...], qn_ref[...]).astype(BF16)
    xkv = _rms(ckv_ref[...], kvn_ref[...]).astype(BF16)
    kr = kr_ref[...]
    kr_ss = jnp.sum(kr * kr, axis=-1, keepdims=True)
    inv_d = 1.0 / MLA_QK
    for h in range(MLA_HEADS):
        lo = h * MLA_QK_PAD
        a = jnp.dot(xq, wq_ref[:, lo:lo + MLA_QK_PAD], preferred_element_type=F32)
        y = a * lax.rsqrt(jnp.sum(a * a, axis=-1, keepdims=True) * inv_d + NORM_EPS) * hgq_ref[...]
        q_ref[:, lo:lo + LANES] = y[:, :LANES].astype(q_ref.dtype)
        q_ref[:, lo + LANES:lo + MLA_QK_PAD] = _rope_apply(y[:, LANES:], cos, sin).astype(q_ref.dtype)
        kv = jnp.dot(xkv, wkv_ref[:, lo:lo + MLA_QK_PAD], preferred_element_type=F32)
        kn = kv[:, :LANES]
        rs = lax.rsqrt((jnp.sum(kn * kn, axis=-1, keepdims=True) + kr_ss) * inv_d + NORM_EPS)
        k_ref[:, lo:lo + LANES] = (kn * rs * gn_ref[...]).astype(k_ref.dtype)
        k_ref[:, lo + LANES:lo + MLA_QK_PAD] = _rope_apply(kr * rs * gr_ref[...], cos, sin).astype(k_ref.dtype)
        v_ref[:, h * MLA_V:(h + 1) * MLA_V] = kv[:, LANES:].astype(v_ref.dtype)


def _mla_qkv(p, qn, kvn, wq, wkv, hgq, gn, gr, cos, sin, *, tm=512):
    S = p.shape[0]
    tm = min(tm, S)
    cq_block = RWKV_SEC // MLA_Q_LORA
    ckv_block = (RWKV_SEC + MLA_Q_LORA) // MLA_KV_LORA
    kr_block = (RWKV_SEC + MLA_Q_LORA + MLA_KV_LORA) // LANES
    NQ = MLA_HEADS * MLA_QK_PAD
    tab = pl.BlockSpec((tm, LANES), lambda i: (i, 0))
    full = lambda a: pl.BlockSpec(a.shape, lambda i: (0, 0))
    return pl.pallas_call(
        _mla_qkv_kernel,
        out_shape=(jax.ShapeDtypeStruct((S, NQ), BF16), jax.ShapeDtypeStruct((S, NQ), BF16),
                   jax.ShapeDtypeStruct((S, MLA_HEADS * MLA_V), BF16)),
        grid=(S // tm,),
        in_specs=[pl.BlockSpec((tm, MLA_Q_LORA), lambda i: (i, cq_block)),
                  pl.BlockSpec((tm, MLA_KV_LORA), lambda i: (i, ckv_block)),
                  pl.BlockSpec((tm, LANES), lambda i: (i, kr_block)),
                  full(qn), full(kvn), full(wq), full(wkv), full(hgq), full(gn), full(gr), tab, tab],
        out_specs=(pl.BlockSpec((tm, NQ), lambda i: (i, 0)), pl.BlockSpec((tm, NQ), lambda i: (i, 0)),
                   pl.BlockSpec((tm, MLA_HEADS * MLA_V), lambda i: (i, 0))),
        compiler_params=_cparams(("parallel",)),
        name="mla_qkv",
    )(p, p, p, qn, kvn, wq, wkv, hgq, gn, gr, cos, sin)


def _flash_rows(s, v1, rows, m_sc, acc_sc):
    m_prev = m_sc[rows, :]
    m_new = jnp.maximum(m_prev, jnp.max(s, axis=-1, keepdims=True))
    alpha = jnp.exp2(m_prev - m_new)
    pr = jnp.exp2((s - jnp.concatenate([m_new] * (s.shape[1] // LANES), axis=1)).astype(BF16))
    acc_sc[rows, :] = (jnp.concatenate([alpha, alpha], axis=1) * acc_sc[rows, :]
                       + jnp.dot(pr, v1, preferred_element_type=F32))
    m_sc[rows, :] = m_new


def _flash_init(m_sc, acc_sc):
    m_sc[...] = jnp.full(m_sc.shape, -jnp.inf, F32)
    acc_sc[...] = jnp.zeros(acc_sc.shape, F32)


def _with_ones(v):
    return jnp.concatenate([v, jnp.ones(v.shape, v.dtype)], axis=1)


def _flash_finish(acc_sc, o_ref):
    acc = acc_sc[...]
    d = acc.shape[1] // 2
    o_ref[...] = (acc[:, :d] / acc[:, d:]).astype(o_ref.dtype)


def _mla_attn_kernel(qi_ref, ki_ref, q_ref, k_ref, v_ref, o_ref, m_sc, acc_sc, *, sb):
    p = pl.program_id(1)
    qi = qi_ref[p]
    ki = ki_ref[p]
    t = q_ref.shape[0]

    @pl.when(ki == 0)
    def _():
        _flash_init(m_sc, acc_sc)

    @pl.when(ki < qi)
    def _():
        v1 = _with_ones(v_ref[...])
        for rb in range(t // sb):
            rows = slice(rb * sb, (rb + 1) * sb)
            s = _dot_nt(q_ref[rows, :], k_ref[...])
            _flash_rows(s, v1, rows, m_sc, acc_sc)

    @pl.when(ki == qi)
    def _():
        v1 = _with_ones(v_ref[...])
        for rb in range(t // sb):
            rows = slice(rb * sb, (rb + 1) * sb)
            nc = (rb + 1) * sb
            s = _dot_nt(q_ref[rows, :], k_ref[0:nc, :])
            row = lax.broadcasted_iota(jnp.int32, s.shape, 0) + rb * sb
            col = lax.broadcasted_iota(jnp.int32, s.shape, 1)
            s = jnp.where(col <= row, s, -jnp.inf)
            _flash_rows(s, v1[0:nc, :], rows, m_sc, acc_sc)
        _flash_finish(acc_sc, o_ref)


def _mla_attention(q, k, v, *, t=2048, sb=256):
    S = q.shape[0]
    t = min(t, S)
    sb = min(sb, t)
    nb = S // t
    pairs = [(a, b) for a in range(nb) for b in range(a + 1)]
    qi_tab = jnp.asarray(np.array([a for a, _ in pairs], np.int32))
    ki_tab = jnp.asarray(np.array([b for _, b in pairs], np.int32))
    stat = pltpu.VMEM((t, LANES), F32)
    grid_spec = pltpu.PrefetchScalarGridSpec(
        num_scalar_prefetch=2,
        grid=(MLA_HEADS, len(pairs)),
        in_specs=[pl.BlockSpec((t, MLA_QK_PAD), lambda h, p, qt, kt: (qt[p], h)),
                  pl.BlockSpec((t, MLA_QK_PAD), lambda h, p, qt, kt: (kt[p], h)),
                  pl.BlockSpec((t, MLA_V), lambda h, p, qt, kt: (kt[p], h))],
        out_specs=pl.BlockSpec((t, MLA_V), lambda h, p, qt, kt: (qt[p], h)),
        scratch_shapes=[stat, pltpu.VMEM((t, 2 * MLA_V), F32)],
    )
    return pl.pallas_call(
        functools.partial(_mla_attn_kernel, sb=sb),
        out_shape=jax.ShapeDtypeStruct((S, MLA_HEADS * MLA_V), BF16),
        grid_spec=grid_spec,
        compiler_params=_cparams(("parallel", "arbitrary")),
        name="mla_attention",
    )(qi_tab, ki_tab, q, k, v)


def _seg_sum(x, e_ref):
    return _split_dot(x, e_ref[...])


def _bmm(a, b):
    return jnp.einsum('gik,gkj->gij', a.astype(BF16), b.astype(BF16), preferred_element_type=F32)


def _bmm_nt(a, b):
    return jnp.einsum('gik,gjk->gij', a.astype(BF16), b.astype(BF16), preferred_element_type=F32)


def _bsplit_cumsum(tri, x):
    G = x.shape[0]
    tb = jnp.broadcast_to(tri.astype(BF16)[None], (G,) + tri.shape)
    x0 = x.astype(BF16)
    r1 = x - x0.astype(F32)
    x1 = r1.astype(BF16)
    x2 = (r1 - x1.astype(F32)).astype(BF16)
    f = lambda t: jnp.einsum('gts,gsl->gtl', tb, t, preferred_element_type=F32)
    return f(x0) + f(x1) + f(x2)


def _rwkv_kernel(pr_ref, pk_ref, pv_ref, px_ref, qr_ref, qk_ref, qv_ref, qx_ref, mur_ref, muk_ref, muv_ref, mux_ref,
                 w0_ref, w2_ref, a0_ref, a2_ref, g2_ref, kk_ref, ka_ref, lnw_ref, lnb_ref, rk_ref, tri_ref, e_ref,
                 o_ref, st_sc, sall_sc, kc_sc, nc_sc, pc_sc, *, hg, G):
    C = RWKV_CHUNK
    LW = hg * RWKV_HEAD
    RW = hg * C
    c_idx = pl.program_id(1)

    @pl.when(c_idx == 0)
    def _():
        st_sc[...] = jnp.zeros(st_sc.shape, F32)

    row = lax.broadcasted_iota(jnp.int32, (G * C, 1), 0)

    def shift_mix(x_ref, prev_ref, mu_ref):
        x = x_ref[...]
        last = jnp.where(c_idx == 0, 0.0, prev_ref[7:8, :])
        prev = jnp.where(row == 0, last, pltpu.roll(x, 1, 0))
        return x + (prev - x) * mu_ref[...]

    r = shift_mix(pr_ref, qr_ref, mur_ref)
    kb = shift_mix(pk_ref, qk_ref, muk_ref)
    v = shift_mix(pv_ref, qv_ref, muv_ref)
    xx = shift_mix(px_ref, qx_ref, mux_ref)
    lw = -EXP_NEG_HALF * _sigmoid(w0_ref[...] + _dot(jnp.tanh(xx[:, :LANES]), w2_ref[...]))
    a = _sigmoid(a0_ref[...] + _dot(xx[:, LANES:2 * LANES], a2_ref[...]))
    g = _dot(_sigmoid(xx[:, 2 * LANES:]), g2_ref[...])
    kk = kb * kk_ref[...]
    kkn = kk * lax.rsqrt(jnp.maximum(_seg_sum(kk * kk, e_ref), 1e-24))
    k = kb * (1.0 + (a - 1.0) * ka_ref[...])
    b = kkn * a

    lw3 = lw.reshape(G, C, LW)
    cum3 = _bsplit_cumsum(tri_ref[...], lw3)
    clast = cum3[:, C - 1:C, :]
    r3 = r.reshape(G, C, LW)
    k3 = k.reshape(G, C, LW)
    v3 = v.reshape(G, C, LW)
    kk3 = kkn.reshape(G, C, LW)
    b3 = b.reshape(G, C, LW)
    einv = jnp.exp(-cum3)
    pc = jnp.exp(clast)
    etail = pc * einv
    rt = r3 * jnp.exp(cum3)
    at = -kk3 * jnp.exp(cum3 - lw3)
    bt = b3 * einv
    kt = k3 * einv
    bh = b3 * etail
    kh = k3 * etail

    lane_head = lax.broadcasted_iota(jnp.int32, (1, 1, LW), 2) // RWKV_HEAD

    def stack(x):
        return jnp.concatenate([jnp.where(lane_head == h, x, 0.0) for h in range(hg)], axis=1)

    a2, b2, k2, r2, v2, bh2, kh2 = (stack(t).astype(BF16) for t in (at, bt, kt, rt, v3, bh, kh))
    ri = lax.broadcasted_iota(jnp.int32, (1, RW, RW), 1)
    ci = lax.broadcasted_iota(jnp.int32, (1, RW, RW), 2)
    same = (ri // C) == (ci // C)
    strict = same & (ci < ri)
    incl = same & (ci <= ri)
    eye = (ri == ci).astype(F32)

    bk2 = jnp.concatenate([b2, k2], axis=1)
    a_bk = _bmm_nt(a2, bk2)
    r_bk = _bmm_nt(r2, bk2)
    lmat = jnp.where(strict, a_bk[:, :, :RW], 0.0)
    akm = jnp.where(strict, a_bk[:, :, RW:], 0.0)
    rbm = jnp.where(incl, r_bk[:, :, :RW], 0.0)
    rkm = jnp.where(incl, r_bk[:, :, RW:], 0.0)
    tinv = eye + lmat
    pw = lmat
    for _ in range(int(np.log2(C)) - 1):
        pw = _bmm(pw, pw)
        tinv = tinv + _bmm(tinv, pw)
    wm = _bmm(tinv, a2)
    z = _bmm(tinv, _bmm(akm, v2))
    qm = r2.astype(F32) + _bmm(rbm, wm)
    y0 = _bmm(rkm, v2) + _bmm(rbm, z)
    kc_sc[...] = _bmm(jnp.swapaxes(wm, 1, 2), bh2)
    nc_sc[...] = _bmm(jnp.swapaxes(z, 1, 2), bh2) + _bmm(jnp.swapaxes(v2, 1, 2), kh2)
    pc_sc[...] = pc

    def chunk(c, carry):
        st = st_sc[...]
        sall_sc[c] = st
        st_sc[...] = st * pc_sc[c] + _dot(st, kc_sc[c]) + nc_sc[c]
        return carry

    lax.fori_loop(0, G, chunk, 0, unroll=True)

    y2 = _bmm_nt(qm, sall_sc[...]) + y0
    y = y2[:, 0:C]
    for h in range(1, hg):
        y = y + y2[:, h * C:(h + 1) * C]

    y = y.reshape(G * C, LW)
    inv_n = 1.0 / RWKV_HEAD
    mu = _seg_sum(y, e_ref) * inv_n
    d = y - mu
    var = _seg_sum(d * d, e_ref) * inv_n
    yn = d * lax.rsqrt(var + RWKV_GN_EPS) * lnw_ref[...] + lnb_ref[...]
    bonus = _seg_sum(r * k * rk_ref[...], e_ref) * v
    o_ref[...] = ((yn + bonus) * g).astype(o_ref.dtype)


def _rwkv_mix(p, mu, w0, w2, a0, a2, g2, k_k, k_a, ln_w, ln_b, r_k, *, hg=2, G=16):
    S = p.shape[0]
    C = RWKV_CHUNK
    G = min(G, S // C)
    TB = G * C
    LW = hg * RWKV_HEAD
    W = RWKV_WIDTH
    ngrp = W // LW
    XW = RWKV_SEC - 3 * W
    t = np.arange(C)
    tri = jnp.asarray((t[None, :] <= t[:, None]).astype(np.float32), BF16)
    l = np.arange(LW)
    e = jnp.asarray((l[:, None] // RWKV_HEAD == l[None, :] // RWKV_HEAD).astype(np.float32), BF16)

    def cols(width, first):
        blk = pl.BlockSpec((TB, width), lambda gi, c: (c, first(gi)))
        prev = pl.BlockSpec((8, width), lambda gi, c: (jnp.maximum(c * (TB // 8) - 1, 0), first(gi)))
        par = pl.BlockSpec((1, width), lambda gi, c: (0, first(gi)))
        return blk, prev, par

    rb, rp, rm = cols(LW, lambda gi: gi)
    kb, kp, km = cols(LW, lambda gi: ngrp + gi)
    vb, vp, vm = cols(LW, lambda gi: 2 * ngrp + gi)
    xb, xp, xm = cols(XW, lambda gi: 3 * W // XW)
    row = pl.BlockSpec((1, LW), lambda gi, c: (0, gi))
    lora = lambda n: pl.BlockSpec((n, LW), lambda gi, c: (0, gi))
    sq = lambda: pltpu.VMEM((G, LW, LW), F32)
    return pl.pallas_call(
        functools.partial(_rwkv_kernel, hg=hg, G=G),
        out_shape=jax.ShapeDtypeStruct((S, W), BF16),
        grid=(ngrp, S // TB),
        in_specs=[rb, kb, vb, xb, rp, kp, vp, xp, rm, km, vm, xm,
                  row, lora(LANES), row, lora(LANES), lora(2 * LANES), row, row, row, row, row,
                  pl.BlockSpec((C, C), lambda gi, c: (0, 0)), pl.BlockSpec((LW, LW), lambda gi, c: (0, 0))],
        out_specs=pl.BlockSpec((TB, LW), lambda gi, c: (c, gi)),
        scratch_shapes=[pltpu.VMEM((LW, LW), F32), sq(), sq(), sq(), pltpu.VMEM((G, 1, LW), F32)],
        compiler_params=_cparams(("parallel", "arbitrary")),
        name="rwkv_mix",
    )(p, p, p, p, p, p, p, p, mu, mu, mu, mu, w0, w2, a0, a2, g2, k_k, k_a, ln_w, ln_b, r_k, tri, e)


def _pad_cols(w, width):
    return jnp.pad(w, ((0, 0), (0, width - w.shape[1])))


def _rope_gap_layout(w):
    z = jnp.zeros(w.shape[:-1] + (32,), w.dtype)
    return jnp.concatenate([w[..., :32], z, w[..., 32:], z], axis=-1)


def _even_mixer(hn, tabs, w_in, q_norm, w_uq, kv_norm, w_ukv, q_head_norm, k_head_norm,
                mu, w0, w2, a0, a2, g2, k_k, k_a, r_k, ln_w, ln_b, w_out):
    cos1, sin1 = tabs[0], tabs[1]
    W = RWKV_WIDTH
    o_cq = 0
    o_ckv = MLA_Q_LORA
    o_kr = o_ckv + MLA_KV_LORA
    o_rw = o_kr + MLA_ROPE
    w_in = w_in.astype(BF16)
    rw = w_in[:, o_rw:]

    def rwkv_layout(t):
        return jnp.concatenate([t[:, :3 * W], _pad_cols(t[:, 3 * W:3 * W + RWKV_W_LORA], LANES),
                                _pad_cols(t[:, 3 * W + RWKV_W_LORA:3 * W + RWKV_W_LORA + RWKV_A_LORA], LANES),
                                _pad_cols(t[:, 3 * W + RWKV_W_LORA + RWKV_A_LORA:], 2 * LANES)], axis=1)

    w_in_p = jnp.concatenate([rwkv_layout(rw), w_in[:, o_cq:o_ckv], w_in[:, o_ckv:o_kr],
                              _rope_gap_layout(w_in[:, o_kr:o_rw]), jnp.zeros((w_in.shape[0], LANES), BF16)], axis=1)
    p = _proj(hn, w_in_p, tn=1536)

    wq = w_uq.astype(BF16).reshape(MLA_Q_LORA, MLA_HEADS, MLA_QK)
    wq = jnp.concatenate([wq[..., :MLA_NOPE], _rope_gap_layout(wq[..., MLA_NOPE:])], axis=-1)
    wq = wq.reshape(MLA_Q_LORA, MLA_HEADS * MLA_QK_PAD)
    scale = MLA_QK ** -0.5 * LOG2E
    hg_q = jnp.concatenate([q_head_norm[:MLA_NOPE], _rope_gap_layout(q_head_norm[MLA_NOPE:])])[None, :] * scale
    gn = k_head_norm[None, :MLA_NOPE]
    gr = _rope_gap_layout(k_head_norm[MLA_NOPE:])[None, :]
    q, kmat, vmat = _mla_qkv(p, q_norm[None, :], kv_norm[None, :], wq, w_ukv.astype(BF16), hg_q, gn, gr, cos1, sin1)
    o_a = _mla_attention(q, kmat, vmat)

    w2p = jnp.pad(w2, ((0, LANES - RWKV_W_LORA), (0, 0))).astype(BF16)
    a2p = jnp.pad(a2, ((0, LANES - RWKV_A_LORA), (0, 0))).astype(BF16)
    g2p = jnp.pad(g2, ((0, 2 * LANES - RWKV_G_LORA), (0, 0))).astype(BF16)
    o_b = _rwkv_mix(p, rwkv_layout(mu[None, :]), w0[None, :], w2p, a0[None, :], a2p, g2p, k_k[None, :], k_a[None, :],
                    ln_w[None, :], ln_b[None, :], r_k.reshape(1, W))
    return o_a, o_b, w_out.astype(BF16)


def _swa_qkv_kernel(x_ref, w_ref, hg_ref, cos_ref, sin_ref, o_ref):
    n = pl.program_id(0)
    half = x_ref.shape[0] // 2

    @pl.when(n < 2)
    def _():
        for r0 in (0, half):
            rows = slice(r0, r0 + half)
            acc = jnp.dot(x_ref[rows, :], w_ref[...], preferred_element_type=F32)
            cos = cos_ref[rows, :]
            sin = sin_ref[rows, :]
            for c in range(acc.shape[1] // LANES):
                sl = slice(c * LANES, (c + 1) * LANES)
                o_ref[rows, sl] = _rope_apply(_rms(acc[:, sl], hg_ref[:, sl]), cos, sin).astype(o_ref.dtype)

    @pl.when(n == 2)
    def _():
        o_ref[...] = jnp.dot(x_ref[...], w_ref[...], preferred_element_type=F32).astype(o_ref.dtype)


def _swa_qkv(x, w, hg, cos, sin, *, tm=1024):
    S, K = x.shape
    tn = SWA_HEADS * SWA_DIM
    tm = min(tm, S)
    tab = pl.BlockSpec((tm, LANES), lambda n, i: (i, 0))
    return pl.pallas_call(
        _swa_qkv_kernel,
        out_shape=jax.ShapeDtypeStruct((S, 3 * tn), BF16),
        grid=(3, S // tm),
        in_specs=[pl.BlockSpec((tm, K), lambda n, i: (i, 0)),
                  pl.BlockSpec((K, tn), lambda n, i: (0, n)),
                  pl.BlockSpec((1, tn), lambda n, i: (0, jnp.minimum(n, 1))),
                  tab, tab],
        out_specs=pl.BlockSpec((tm, tn), lambda n, i: (i, n)),
        compiler_params=_cparams(("parallel", "parallel")),
        name="swa_qkv",
    )(x, w, hg, cos, sin)


def _dilated_bias(t, nrel):
    row = np.arange(t)[:, None]
    col = np.arange(t)[None, :]
    out = np.empty((nrel, t, t), np.float32)
    for r in range(nrel):
        delta = r * t + row - col
        cnt = np.zeros((t, t), np.float64)
        for window, dilation in SWA_PATTERNS:
            cnt += (delta >= 0) & (delta <= window) & (delta % dilation == 0)
        with np.errstate(divide="ignore"):
            out[r] = np.log2(cnt)
    return out


def _dilated_kernel(q_ref, k_ref, v_ref, bias_ref, o_ref, m_sc, acc_sc, *, sb, nrel, live):
    qi = pl.program_id(1)
    r = pl.program_id(2)
    t = q_ref.shape[0]

    @pl.when(r == 0)
    def _():
        _flash_init(m_sc, acc_sc)

    for rr in range(nrel):
        @pl.when((r == rr) & (qi - r >= 0))
        def _(rr=rr):
            v1 = _with_ones(v_ref[...])
            for rb in range(t // sb):
                rows = slice(rb * sb, (rb + 1) * sb)
                lo, hi = live[rr][rb]
                s = _dot_nt(q_ref[rows, :], k_ref[lo:hi, :]) + bias_ref[rr, rows, lo:hi]
                _flash_rows(s, v1[lo:hi, :], rows, m_sc, acc_sc)

    @pl.when(r == nrel - 1)
    def _():
        _flash_finish(acc_sc, o_ref)


def _dilated_attention(qkv, *, t=1024, sb=256):
    S = qkv.shape[0]
    t = min(t, S)
    sb = min(sb, t)
    max_window = max(w for w, _ in SWA_PATTERNS)
    nrel = min(-(-max_window // t) + 1, S // t)
    H = SWA_HEADS
    bias = _dilated_bias(t, nrel)
    live = []
    for r in range(nrel):
        per_rb = []
        for rb in range(t // sb):
            cols = np.nonzero(np.isfinite(bias[r, rb * sb:(rb + 1) * sb]).any(axis=0))[0]
            lo = int(cols.min()) // LANES * LANES
            hi = -(-(int(cols.max()) + 1) // LANES) * LANES
            per_rb.append((lo, hi))
        live.append(per_rb)
    stat = pltpu.VMEM((t, LANES), F32)
    return pl.pallas_call(
        functools.partial(_dilated_kernel, sb=sb, nrel=nrel, live=live),
        out_shape=jax.ShapeDtypeStruct((S, H * SWA_DIM), BF16),
        grid=(H, S // t, nrel),
        in_specs=[pl.BlockSpec((t, SWA_DIM), lambda h, i, r: (i, h)),
                  pl.BlockSpec((t, SWA_DIM), lambda h, i, r: (jnp.maximum(i - r, 0), H + h)),
                  pl.BlockSpec((t, SWA_DIM), lambda h, i, r: (jnp.maximum(i - r, 0), 2 * H + h)),
                  pl.BlockSpec((nrel, t, t), lambda h, i, r: (0, 0, 0))],
        out_specs=pl.BlockSpec((t, SWA_DIM), lambda h, i, r: (i, h)),
        scratch_shapes=[stat, pltpu.VMEM((t, 2 * SWA_DIM), F32)],
        compiler_params=_cparams(("parallel", "parallel", "arbitrary")),
        name="dilated_attention",
    )(qkv, qkv, qkv, jnp.asarray(bias))


def _gla_kernel(q_ref, k_ref, v_ref, glr_ref, wgu_ref, bg_ref, rd_ref, gn_ref, tri_ref, o_ref,
                st_sc, sall_sc, oi_sc, *, G):
    C = GLA_CHUNK
    SB = GLA_SUB
    DK = GLA_DK
    DV = GLA_DV
    c_idx = pl.program_id(1)

    @pl.when(c_idx == 0)
    def _():
        st_sc[...] = jnp.zeros(st_sc.shape, F32)

    z = _dot(glr_ref[...], wgu_ref[...]) + bg_ref[...]
    lg = -_softplus(-z) * (1.0 / GLA_NORMALIZER)
    bc = _bsplit_cumsum(tri_ref[...], lg.reshape(G, C, DK))
    q = (q_ref[...] * (DK ** -0.5)).reshape(G, C, DK)
    k = k_ref[...].reshape(G, C, DK)
    v = v_ref[...].reshape(G, C, DV)
    blast = bc[:, C - 1:C, :]
    khat = k * jnp.exp(blast - bc)
    gkv = _bmm(jnp.swapaxes(v, 1, 2), khat)
    eb = jnp.exp(blast)

    st = st_sc[...]
    for c in range(G):
        sall_sc[c] = st
        st = st * eb[c] + gkv[c]
    st_sc[...] = st

    ti = lax.broadcasted_iota(jnp.int32, (1, SB, SB, 1), 1)
    si = lax.broadcasted_iota(jnp.int32, (1, SB, SB, 1), 2)
    causal = si <= ti
    for sb in range(C // SB):
        lo = sb * SB
        qs = q[:, lo:lo + SB]
        ks = k[:, lo:lo + SB]
        bs = bc[:, lo:lo + SB]
        rel = bs[:, :, None, :] - bs[:, None, :, :]
        dec = jnp.exp(jnp.where(causal, rel, -jnp.inf))
        att = jnp.sum(qs[:, :, None, :] * ks[:, None, :, :] * dec, axis=-1)
        o_sb = _bmm(att, v[:, lo:lo + SB])
        if sb > 0:
            bm = bc[:, lo - 1:lo]
            qsc = qs * jnp.exp(bs - bm)
            ksc = k[:, :lo] * jnp.exp(bm - bc[:, :lo])
            o_sb = o_sb + _bmm(_bmm_nt(qsc, ksc), v[:, :lo])
        oi_sc[:, lo:lo + SB, :] = o_sb

    o = (oi_sc[...] + _bmm_nt(q * jnp.exp(bc), sall_sc[...])).reshape(G * C, DV)
    rd = rd_ref[...]
    o_ref[...] = (_rms(o, gn_ref[...]) * (rd * _sigmoid(rd))).astype(o_ref.dtype)


def _gla(pb, wgu, bg, gn, *, G=32):
    S = pb.shape[0]
    C = GLA_CHUNK
    G = min(G, S // C)
    TB = G * C
    H, DK, DV = GLA_HEADS, GLA_DK, GLA_DV
    t = np.arange(C)
    tri = jnp.asarray((t[None, :] <= t[:, None]).astype(np.float32), BF16)
    glr_block = (2 * H * DK + 2 * H * DV) // LANES
    return pl.pallas_call(
        functools.partial(_gla_kernel, G=G),
        out_shape=jax.ShapeDtypeStruct((S, H * DV), BF16),
        grid=(H, S // TB),
        in_specs=[pl.BlockSpec((TB, DK), lambda h, c: (c, h)),
                  pl.BlockSpec((TB, DK), lambda h, c: (c, H + h)),
                  pl.BlockSpec((TB, DV), lambda h, c: (c, (2 * H * DK) // DV + h)),
                  pl.BlockSpec((TB, LANES), lambda h, c: (c, glr_block)),
                  pl.BlockSpec((LANES, DK), lambda h, c: (0, h)),
                  pl.BlockSpec((1, DK), lambda h, c: (0, h)),
                  pl.BlockSpec((TB, DV), lambda h, c: (c, (2 * H * DK + H * DV) // DV + h)),
                  pl.BlockSpec((1, DV), lambda h, c: (0, h)),
                  pl.BlockSpec((C, C), lambda h, c: (0, 0))],
        out_specs=pl.BlockSpec((TB, DV), lambda h, c: (c, h)),
        scratch_shapes=[pltpu.VMEM((DV, DK), F32), pltpu.VMEM((G, DV, DK), F32), pltpu.VMEM((G, C, DV), F32)],
        compiler_params=_cparams(("parallel", "arbitrary")),
        name="gla",
    )(pb, pb, pb, pb, wgu, bg, pb, gn, tri)


def _odd_mixer(hn, tabs, w_in, q_head_norm, k_head_norm, w_gate_up, b_gate, gla_norm, w_out):
    cos2, sin2 = tabs[2], tabs[3]
    nq = SWA_HEADS * SWA_DIM
    hg = jnp.concatenate([jnp.tile(q_head_norm * (SWA_DIM ** -0.5 * LOG2E), SWA_HEADS), jnp.tile(k_head_norm, SWA_HEADS)])[None, :]
    w_in = w_in.astype(BF16)
    qkv = _swa_qkv(hn, w_in, hg, cos2, sin2)
    o_c = _dilated_attention(qkv)
    o = 3 * nq
    dk, dv = GLA_HEADS * GLA_DK, GLA_HEADS * GLA_DV
    wb = jnp.concatenate([w_in[:, o:o + 2 * dk + dv], w_in[:, o + 2 * dk + dv + GLA_LORA:],
                          _pad_cols(w_in[:, o + 2 * dk + dv:o + 2 * dk + dv + GLA_LORA], LANES)], axis=1)
    pb = _proj(hn, wb, tm=512, tn=wb.shape[1])
    wgu = jnp.pad(w_gate_up, ((0, LANES - GLA_LORA), (0, 0))).astype(BF16)
    o_d = _gla(pb, wgu, b_gate[None, :], gla_norm.reshape(1, dv))
    return o_c, o_d, w_out.astype(BF16)


def kernel(x, positions, ffn_norm, ffn_w_gate, ffn_w_up, ffn_w_down, mix_norm, mla_rwkv_w_in, mla_q_norm, mla_w_uq,
           mla_kv_norm, mla_w_ukv, mla_q_head_norm, mla_k_head_norm, rwkv_mu, rwkv_w0, rwkv_w2, rwkv_a0, rwkv_a2,
           rwkv_g2, rwkv_k_k, rwkv_k_a, rwkv_r_k, rwkv_ln_w, rwkv_ln_b, mla_rwkv_w_out, swa_gla_w_in,
           swa_q_head_norm, swa_k_head_norm, gla_w_gate_up, gla_b_gate, gla_norm, swa_gla_w_out):
    B, S, D = x.shape
    assert B == 1
    h = x.reshape(S, D)
    tabs = _rope_tables(positions.reshape(S))
    wg = ffn_w_gate.astype(BF16)
    wu = ffn_w_up.astype(BF16)
    wd = ffn_w_down.astype(BF16)
    depth = ffn_norm.shape[0]
    for layer in range(depth):
        i = layer // 2
        h, hn = _ffn(h, ffn_norm, wg, wu, wd, layer, 0, next_norm=mix_norm[layer])
        if layer % 2 == 0:
            mix = _even_mixer(hn, tabs, mla_rwkv_w_in[i], mla_q_norm[i], mla_w_uq[i], mla_kv_norm[i],
                            mla_w_ukv[i], mla_q_head_norm[i], mla_k_head_norm[i], rwkv_mu[i], rwkv_w0[i],
                            rwkv_w2[i], rwkv_a0[i], rwkv_a2[i], rwkv_g2[i], rwkv_k_k[i], rwkv_k_a[i], rwkv_r_k[i],
                            rwkv_ln_w[i], rwkv_ln_b[i], mla_rwkv_w_out[i])
        else:
            mix = _odd_mixer(hn, tabs, swa_gla_w_in[i], swa_q_head_norm[i], swa_k_head_norm[i],
                           gla_w_gate_up[i], gla_b_gate[i], gla_norm[i], swa_gla_w_out[i])
        h = _ffn(h, ffn_norm, wg, wu, wd, layer, 1, mix=mix)
    return h.reshape(B, S, D)
```

```python
import functools

import numpy as np
import jax
import jax.numpy as jnp
from jax import lax
from jax.experimental import pallas as pl
from jax.experimental.pallas import tpu as pltpu

F32 = jnp.float32
BF16 = jnp.bfloat16

LANES = 128
VMEM_LIMIT = 56 * 1024 * 1024

D_MODEL = 2048
D_FF = 5632
MACARON_WEIGHT = 0.5
NORM_EPS = 1e-6
ROPE_THETA = 10000.0
LOG2E = float(np.log2(np.e))
EXP_NEG_HALF = float(np.exp(-0.5))

MLA_HEADS = 8
MLA_NOPE = 128
MLA_ROPE = 64
MLA_QK = MLA_NOPE + MLA_ROPE
MLA_QK_PAD = 256
MLA_V = 128
MLA_Q_LORA = 512
MLA_KV_LORA = 256

RWKV_HEAD = 64
RWKV_WIDTH = 1024
RWKV_HEADS = 16
RWKV_W_LORA = 64
RWKV_A_LORA = 64
RWKV_G_LORA = 160
RWKV_GN_EPS = 64e-5
RWKV_SEC = 3584
RWKV_CHUNK = 64

SWA_HEADS = 8
SWA_DIM = 128
SWA_PATTERNS = ((128, 1), (512, 4), (2048, 16))

GLA_HEADS = 4
GLA_DK = 128
GLA_DV = 256
GLA_LORA = 16
GLA_NORMALIZER = 16.0
GLA_CHUNK = 64
GLA_SUB = 8


def _cparams(sem):
    return pltpu.CompilerParams(dimension_semantics=sem, vmem_limit_bytes=VMEM_LIMIT)


def _rms(x, g):
    return x * lax.rsqrt(jnp.mean(x * x, axis=-1, keepdims=True) + NORM_EPS) * g


def _dot(a, b):
    return jnp.dot(a.astype(BF16), b.astype(BF16), preferred_element_type=F32)


def _dot_nt(a, b):
    return lax.dot_general(a.astype(BF16), b.astype(BF16), (((1,), (1,)), ((), ())),
                           preferred_element_type=F32)


def _sigmoid(x):
    return 0.5 * jnp.tanh(0.5 * x) + 0.5


def _softplus(x):
    return jnp.maximum(x, 0.0) + jnp.log(1.0 + jnp.exp(-jnp.abs(x)))


def _rope_apply(x, cos, sin_signed):
    return x * cos + pltpu.roll(x, 64, 1) * sin_signed


def _ffn_kernel(*refs, emit_norm, mix):
    it = iter(refs)
    h_ref, g_ref, wg_ref, wu_ref, wd_ref = (next(it) for _ in range(5))
    if mix:
        a_ref, b_ref, wa_ref, wb_ref = (next(it) for _ in range(4))
    if emit_norm:
        g2_ref = next(it)
    o_ref = next(it)
    if emit_norm:
        hn_ref = next(it)
    xn_ref = next(it)
    j = pl.program_id(1)

    @pl.when(j == 0)
    def _():
        x = h_ref[...]
        if mix:
            x = (x + jnp.dot(a_ref[...], wa_ref[...], preferred_element_type=F32)
                 + jnp.dot(b_ref[...], wb_ref[...], preferred_element_type=F32))
        xn_ref[...] = _rms(x, g_ref[...]).astype(BF16)
        o_ref[...] = x

    xn = xn_ref[...]
    gate = jnp.dot(xn, wg_ref[...], preferred_element_type=F32)
    up = jnp.dot(xn, wu_ref[...], preferred_element_type=F32)
    act = (gate * _sigmoid(gate) * up * MACARON_WEIGHT).astype(BF16)
    o_ref[...] += jnp.dot(act, wd_ref[...], preferred_element_type=F32)

    if emit_norm:
        @pl.when(j == pl.num_programs(1) - 1)
        def _():
            hn_ref[...] = _rms(o_ref[...], g2_ref[...]).astype(hn_ref.dtype)


def _ffn(h, g, wg, wu, wd, layer, k, next_norm=None, mix=None, *, tm=512, tf=512):
    S, D = h.shape
    F = wg.shape[-1]
    tm = min(tm, S)
    emit_norm = next_norm is not None
    wspec = pl.BlockSpec((None, None, D, tf), lambda i, j: (layer, k, 0, j))
    row = pl.BlockSpec((1, D), lambda i, j: (0, 0))
    tile = pl.BlockSpec((tm, D), lambda i, j: (i, 0))
    in_specs = [tile, row, wspec, wspec, pl.BlockSpec((None, None, tf, D), lambda i, j: (layer, k, j, 0))]
    args = [h, g[layer, k][None, :], wg, wu, wd]
    if mix is not None:
        a, b, w_out = mix
        ka = a.shape[1]
        assert b.shape[1] == ka and w_out.shape == (2 * ka, D)
        half = pl.BlockSpec((tm, ka), lambda i, j: (i, 0))
        in_specs += [half, half, pl.BlockSpec((ka, D), lambda i, j: (0, 0)), pl.BlockSpec((ka, D), lambda i, j: (1, 0))]
        args += [a, b, w_out, w_out]
    out_shape = jax.ShapeDtypeStruct((S, D), F32)
    out_specs = tile
    if emit_norm:
        in_specs.append(row)
        args.append(next_norm[None, :])
        out_shape = (out_shape, jax.ShapeDtypeStruct((S, D), BF16))
        out_specs = (tile, tile)
    return pl.pallas_call(
        functools.partial(_ffn_kernel, emit_norm=emit_norm, mix=mix is not None),
        out_shape=out_shape,
        grid=(S // tm, F // tf),
        in_specs=in_specs,
        out_specs=out_specs,
        scratch_shapes=[pltpu.VMEM((tm, D), BF16)],
        compiler_params=_cparams(("parallel", "arbitrary")),
        name="ffn",
    )(*args)


def _rope_tab_kernel(pos_ref, f1_ref, s1_ref, f2_ref, s2_ref, c1_ref, n1_ref, c2_ref, n2_ref):
    pos = pos_ref[...].astype(F32)
    a1 = pos * f1_ref[...]
    c1_ref[...] = jnp.cos(a1)
    n1_ref[...] = jnp.sin(a1) * s1_ref[...]
    a2 = pos * f2_ref[...]
    c2_ref[...] = jnp.cos(a2)
    n2_ref[...] = jnp.sin(a2) * s2_ref[...]


def _rope_tables(positions, *, tm=1024):
    S = positions.shape[0]
    tm = min(tm, S)
    f32half = ROPE_THETA ** (-jnp.arange(MLA_ROPE // 2, dtype=F32) / (MLA_ROPE // 2))
    z = jnp.zeros((32,), F32)
    f1 = jnp.concatenate([f32half, z, f32half, z])[None]
    f64half = ROPE_THETA ** (-jnp.arange(SWA_DIM // 2, dtype=F32) / (SWA_DIM // 2))
    f2 = jnp.concatenate([f64half, f64half])[None]
    sign = jnp.concatenate([-jnp.ones((64,), F32), jnp.ones((64,), F32)])[None]
    row = pl.BlockSpec((1, LANES), lambda i: (0, 0))
    tab = pl.BlockSpec((tm, LANES), lambda i: (i, 0))
    shp = jax.ShapeDtypeStruct((S, LANES), F32)
    return pl.pallas_call(
        _rope_tab_kernel,
        out_shape=(shp, shp, shp, shp),
        grid=(S // tm,),
        in_specs=[pl.BlockSpec((tm, 1), lambda i: (i, 0)), row, row, row, row],
        out_specs=(tab, tab, tab, tab),
        compiler_params=_cparams(("parallel",)),
        name="rope_tables",
    )(positions.reshape(S, 1), f1, sign, f2, sign)


def _proj_kernel(x_ref, w_ref, o_ref):
    o_ref[...] = jnp.dot(x_ref[...], w_ref[...], preferred_element_type=F32).astype(o_ref.dtype)


def _proj(x, w, *, tm=1024, tn, out_dtype=F32):
    S, K = x.shape
    N = w.shape[1]
    tm = min(tm, S)
    return pl.pallas_call(
        _proj_kernel,
        out_shape=jax.ShapeDtypeStruct((S, N), out_dtype),
        grid=(N // tn, S // tm),
        in_specs=[pl.BlockSpec((tm, K), lambda n, i: (i, 0)),
                  pl.BlockSpec((K, tn), lambda n, i: (0, n))],
        out_specs=pl.BlockSpec((tm, tn), lambda n, i: (i, n)),
        compiler_params=_cparams(("parallel", "parallel")),
        name="proj",
    )(x, w)


def _mla_qkv_kernel(cq_ref, ckv_ref, kr_ref, qn_ref, kvn_ref, wq_ref, wkv_ref, hgq_ref, gn_ref, gr_ref,
                    cos_ref, sin_ref, q_ref, k_ref, v_ref):
    cos = cos_ref[...]
    sin = sin_ref[...]
    xq = _rms(cq_ref[...], qn_ref[...]).astype(BF16)
    xkv = _rms(ckv_ref[...], kvn_ref[...]).astype(BF16)
    kr = kr_ref[...]
    kr_ss = jnp.sum(kr * kr, axis=-1, keepdims=True)
    inv_d = 1.0 / MLA_QK
    for h in range(MLA_HEADS):
        lo = h * MLA_QK_PAD
        a = jnp.dot(xq, wq_ref[:, lo:lo + MLA_QK_PAD], preferred_element_type=F32)
        y = a * lax.rsqrt(jnp.sum(a * a, axis=-1, keepdims=True) * inv_d + NORM_EPS) * hgq_ref[...]
        q_ref[:, lo:lo + LANES] = y[:, :LANES].astype(q_ref.dtype)
        q_ref[:, lo + LANES:lo + MLA_QK_PAD] = _rope_apply(y[:, LANES:], cos, sin).astype(q_ref.dtype)
        kv = jnp.dot(xkv, wkv_ref[:, lo:lo + MLA_QK_PAD], preferred_element_type=F32)
        kn = kv[:, :LANES]
        rs = lax.rsqrt((jnp.sum(kn * kn, axis=-1, keepdims=True) + kr_ss) * inv_d + NORM_EPS)
        k_ref[:, lo:lo + LANES] = (kn * rs * gn_ref[...]).astype(k_ref.dtype)
        k_ref[:, lo + LANES:lo + MLA_QK_PAD] = _rope_apply(kr * rs * gr_ref[...], cos, sin).astype(k_ref.dtype)
        v_ref[:, h * MLA_V:(h + 1) * MLA_V] = kv[:, LANES:].astype(v_ref.dtype)


def _mla_qkv(p, qn, kvn, wq, wkv, hgq, gn, gr, cos, sin, *, tm=512):
    S = p.shape[0]
    tm = min(tm, S)
    cq_block = RWKV_SEC // MLA_Q_LORA
    ckv_block = (RWKV_SEC + MLA_Q_LORA) // MLA_KV_LORA
    kr_block = (RWKV_SEC + MLA_Q_LORA + MLA_KV_LORA) // LANES
    NQ = MLA_HEADS * MLA_QK_PAD
    tab = pl.BlockSpec((tm, LANES), lambda i: (i, 0))
    full = lambda a: pl.BlockSpec(a.shape, lambda i: (0, 0))
    return pl.pallas_call(
        _mla_qkv_kernel,
        out_shape=(jax.ShapeDtypeStruct((S, NQ), BF16), jax.ShapeDtypeStruct((S, NQ), BF16),
                   jax.ShapeDtypeStruct((S, MLA_HEADS * MLA_V), BF16)),
        grid=(S // tm,),
        in_specs=[pl.BlockSpec((tm, MLA_Q_LORA), lambda i: (i, cq_block)),
                  pl.BlockSpec((tm, MLA_KV_LORA), lambda i: (i, ckv_block)),
                  pl.BlockSpec((tm, LANES), lambda i: (i, kr_block)),
                  full(qn), full(kvn), full(wq), full(wkv), full(hgq), full(gn), full(gr), tab, tab],
        out_specs=(pl.BlockSpec((tm, NQ), lambda i: (i, 0)), pl.BlockSpec((tm, NQ), lambda i: (i, 0)),
                   pl.BlockSpec((tm, MLA_HEADS * MLA_V), lambda i: (i, 0))),
        compiler_params=_cparams(("parallel",)),
        name="mla_qkv",
    )(p, p, p, qn, kvn, wq, wkv, hgq, gn, gr, cos, sin)


def _flash_rows(s, v1, rows, m_sc, acc_sc):
    m_prev = m_sc[rows, :]
    m_new = jnp.maximum(m_prev, jnp.max(s, axis=-1, keepdims=True))
    alpha = jnp.exp2(m_prev - m_new)
    pr = jnp.exp2((s - jnp.concatenate([m_new] * (s.shape[1] // LANES), axis=1)).astype(BF16))
    acc_sc[rows, :] = (jnp.concatenate([alpha, alpha], axis=1) * acc_sc[rows, :]
                       + jnp.dot(pr, v1, preferred_element_type=F32))
    m_sc[rows, :] = m_new


def _flash_init(m_sc, acc_sc):
    m_sc[...] = jnp.full(m_sc.shape, -jnp.inf, F32)
    acc_sc[...] = jnp.zeros(acc_sc.shape, F32)


def _with_ones(v):
    return jnp.concatenate([v, jnp.ones(v.shape, v.dtype)], axis=1)


def _flash_finish(acc_sc, o_ref):
    acc = acc_sc[...]
    d = acc.shape[1] // 2
    o_ref[...] = (acc[:, :d] / acc[:, d:]).astype(o_ref.dtype)


def _mla_attn_kernel(qi_ref, ki_ref, q_ref, k_ref, v_ref, o_ref, m_sc, acc_sc, *, sb):
    p = pl.program_id(1)
    qi = qi_ref[p]
    ki = ki_ref[p]
    t = q_ref.shape[0]

    @pl.when(ki == 0)
    def _():
        _flash_init(m_sc, acc_sc)

    @pl.when(ki < qi)
    def _():
        v1 = _with_ones(v_ref[...])
        for rb in range(t // sb):
            rows = slice(rb * sb, (rb + 1) * sb)
            s = _dot_nt(q_ref[rows, :], k_ref[...])
            _flash_rows(s, v1, rows, m_sc, acc_sc)

    @pl.when(ki == qi)
    def _():
        v1 = _with_ones(v_ref[...])
        for rb in range(t // sb):
            rows = slice(rb * sb, (rb + 1) * sb)
            nc = (rb + 1) * sb
            s = _dot_nt(q_ref[rows, :], k_ref[0:nc, :])
            row = lax.broadcasted_iota(jnp.int32, s.shape, 0) + rb * sb
            col = lax.broadcasted_iota(jnp.int32, s.shape, 1)
            s = jnp.where(col <= row, s, -jnp.inf)
            _flash_rows(s, v1[0:nc, :], rows, m_sc, acc_sc)
        _flash_finish(acc_sc, o_ref)


def _mla_attention(q, k, v, *, t=2048, sb=256):
    S = q.shape[0]
    t = min(t, S)
    sb = min(sb, t)
    nb = S // t
    pairs = [(a, b) for a in range(nb) for b in range(a + 1)]
    qi_tab = jnp.asarray(np.array([a for a, _ in pairs], np.int32))
    ki_tab = jnp.asarray(np.array([b for _, b in pairs], np.int32))
    stat = pltpu.VMEM((t, LANES), F32)
    grid_spec = pltpu.PrefetchScalarGridSpec(
        num_scalar_prefetch=2,
        grid=(MLA_HEADS, len(pairs)),
        in_specs=[pl.BlockSpec((t, MLA_QK_PAD), lambda h, p, qt, kt: (qt[p], h)),
                  pl.BlockSpec((t, MLA_QK_PAD), lambda h, p, qt, kt: (kt[p], h)),
                  pl.BlockSpec((t, MLA_V), lambda h, p, qt, kt: (kt[p], h))],
        out_specs=pl.BlockSpec((t, MLA_V), lambda h, p, qt, kt: (qt[p], h)),
        scratch_shapes=[stat, pltpu.VMEM((t, 2 * MLA_V), F32)],
    )
    return pl.pallas_call(
        functools.partial(_mla_attn_kernel, sb=sb),
        out_shape=jax.ShapeDtypeStruct((S, MLA_HEADS * MLA_V), BF16),
        grid_spec=grid_spec,
        compiler_params=_cparams(("parallel", "arbitrary")),
        name="mla_attention",
    )(qi_tab, ki_tab, q, k, v)


def _bmm(a, b):
    return jnp.einsum('gik,gkj->gij', a.astype(BF16), b.astype(BF16), preferred_element_type=F32)


def _bmm_nt(a, b):
    return jnp.einsum('gik,gjk->gij', a.astype(BF16), b.astype(BF16), preferred_element_type=F32)


def _bsplit_cumsum(tri, x):
    G = x.shape[0]
    tb = jnp.broadcast_to(tri.astype(BF16)[None], (G,) + tri.shape)
    x0 = x.astype(BF16)
    r1 = x - x0.astype(F32)
    x1 = r1.astype(BF16)
    x2 = (r1 - x1.astype(F32)).astype(BF16)
    f = lambda t: jnp.einsum('gts,gsl->gtl', tb, t, preferred_element_type=F32)
    return f(x0) + f(x1) + f(x2)


def _rwkv_kernel(p_ref, q_ref, mu_ref, w0_ref, w2_ref, a0_ref, a2_ref, g2_ref, kk_ref, ka_ref, lnw_ref, lnb_ref, rk_ref,
                 tri_ref, e_ref, o_ref, st_sc, *, hg, cps):
    C = RWKV_CHUNK
    W = RWKV_WIDTH
    LW = hg * RWKV_HEAD
    RW = hg * C
    NP = W // LW
    NB = cps * NP
    TB = cps * C
    c_idx = pl.program_id(0)

    @pl.when(c_idx == 0)
    def _():
        st_sc[...] = jnp.zeros(st_sc.shape, F32)

    x = p_ref[...]
    row = lax.broadcasted_iota(jnp.int32, (TB, 1), 0)
    last = jnp.where(c_idx == 0, 0.0, q_ref[7:8, :])
    prev = jnp.where(row == 0, last, pltpu.roll(x, 1, 0))
    xs = x + (prev - x) * mu_ref[...]
    r = xs[:, :W]
    kb = xs[:, W:2 * W]
    v = xs[:, 2 * W:3 * W]
    xw = xs[:, 3 * W:3 * W + LANES]
    xa = xs[:, 3 * W + LANES:3 * W + 2 * LANES]
    xg = xs[:, 3 * W + 2 * LANES:]
    lw = -EXP_NEG_HALF * _sigmoid(w0_ref[...] + _dot(jnp.tanh(xw), w2_ref[...]))
    a = _sigmoid(a0_ref[...] + _dot(xa, a2_ref[...]))
    g = _dot(_sigmoid(xg), g2_ref[...])
    k = kb * (1.0 + (a - 1.0) * ka_ref[...])

    def to_b(t):
        t3 = t.reshape(cps, C, W)
        return jnp.concatenate([t3[c:c + 1, :, LW * p:LW * (p + 1)] for c in range(cps) for p in range(NP)], axis=0)

    def from_b(t):
        rows = [jnp.concatenate([t[c * NP + p] for p in range(NP)], axis=-1) for c in range(cps)]
        return jnp.concatenate(rows, axis=0)

    e_b = jnp.broadcast_to(e_ref[...][None], (NB, LW, LW))

    def seg_sum(t):
        t0 = t.astype(BF16)
        t1 = (t - t0.astype(F32)).astype(BF16)
        return _bmm(t0, e_b) + _bmm(t1, e_b)

    lw3, r3, k3, v3, a3 = to_b(lw), to_b(r), to_b(k), to_b(v), to_b(a)
    kkb = to_b(kb * kk_ref[...])
    kk3 = kkb * lax.rsqrt(jnp.maximum(seg_sum(kkb * kkb), 1e-24))
    b3 = kk3 * a3

    cum3 = _bsplit_cumsum(tri_ref[...], lw3)
    clast = cum3[:, C - 1:C, :]
    einv = jnp.exp(-cum3)
    pc = jnp.exp(clast)
    etail = pc * einv
    rt = r3 * jnp.exp(cum3)
    at = -kk3 * jnp.exp(cum3 - lw3)
    bt = b3 * einv
    kt = k3 * einv
    bh = b3 * etail
    kh = k3 * etail

    lane_head = lax.broadcasted_iota(jnp.int32, (1, 1, LW), 2) // RWKV_HEAD

    def stack(t):
        return jnp.concatenate([jnp.where(lane_head == h, t, 0.0) for h in range(hg)], axis=1)

    a2, b2, k2, r2, v2, bh2, kh2 = (stack(t).astype(BF16) for t in (at, bt, kt, rt, v3, bh, kh))
    ri = lax.broadcasted_iota(jnp.int32, (1, RW, RW), 1)
    ci = lax.broadcasted_iota(jnp.int32, (1, RW, RW), 2)
    same = (ri // C) == (ci // C)
    strict = same & (ci < ri)
    incl = same & (ci <= ri)
    eye = (ri == ci).astype(F32)

    bk2 = jnp.concatenate([b2, k2], axis=1)
    a_bk = _bmm_nt(a2, bk2)
    r_bk = _bmm_nt(r2, bk2)
    lmat = jnp.where(strict, a_bk[:, :, :RW], 0.0)
    akm = jnp.where(strict, a_bk[:, :, RW:], 0.0)
    rbm = jnp.where(incl, r_bk[:, :, :RW], 0.0)
    rkm = jnp.where(incl, r_bk[:, :, RW:], 0.0)
    tinv = eye + lmat
    pw = lmat
    for _ in range(int(np.log2(C)) - 1):
        pw = _bmm(pw, pw)
        tinv = tinv + _bmm(tinv, pw)
    wm = _bmm(tinv, a2)
    z = _bmm(tinv, _bmm(akm, v2))
    qm = r2.astype(F32) + _bmm(rbm, wm)
    y0 = _bmm(rkm, v2) + _bmm(rbm, z)
    kc = _bmm(jnp.swapaxes(wm, 1, 2), bh2)
    nc = _bmm(jnp.swapaxes(z, 1, 2), bh2) + _bmm(jnp.swapaxes(v2, 1, 2), kh2)

    st = st_sc[...]
    y2 = []
    for c in range(cps):
        sl = slice(c * NP, (c + 1) * NP)
        y2.append(_bmm_nt(qm[sl], st) + y0[sl])
        st = st * pc[sl] + _bmm(st, kc[sl]) + nc[sl]
    st_sc[...] = st
    y2 = jnp.concatenate(y2, axis=0)
    y = y2[:, 0:C]
    for h in range(1, hg):
        y = y + y2[:, h * C:(h + 1) * C]

    inv_n = 1.0 / RWKV_HEAD
    d = y - seg_sum(y) * inv_n
    yhat = d * lax.rsqrt(seg_sum(d * d) * inv_n + RWKV_GN_EPS)
    bonus = seg_sum(to_b(r * k * rk_ref[...])) * v3
    out = (from_b(yhat) * lnw_ref[...] + lnb_ref[...] + from_b(bonus)) * g
    o_ref[...] = out.astype(o_ref.dtype)


def _rwkv_mix(p, mu, w0, w2, a0, a2, g2, k_k, k_a, ln_w, ln_b, r_k, *, hg=2, cps=2):
    S = p.shape[0]
    C = RWKV_CHUNK
    cps = min(cps, S // C)
    TB = cps * C
    LW = hg * RWKV_HEAD
    W = RWKV_WIDTH
    t = np.arange(C)
    tri = jnp.asarray((t[None, :] <= t[:, None]).astype(np.float32), BF16)
    l = np.arange(LW)
    e = jnp.asarray((l[:, None] // RWKV_HEAD == l[None, :] // RWKV_HEAD).astype(np.float32), BF16)
    full = lambda arr: pl.BlockSpec(arr.shape, lambda c: (0, 0))
    return pl.pallas_call(
        functools.partial(_rwkv_kernel, hg=hg, cps=cps),
        out_shape=jax.ShapeDtypeStruct((S, W), BF16),
        grid=(S // TB,),
        in_specs=[pl.BlockSpec((TB, RWKV_SEC), lambda c: (c, 0)),
                  pl.BlockSpec((8, RWKV_SEC), lambda c: (jnp.maximum(c * (TB // 8) - 1, 0), 0)),
                  full(mu), full(w0), full(w2), full(a0), full(a2), full(g2), full(k_k), full(k_a),
                  full(ln_w), full(ln_b), full(r_k), full(tri), full(e)],
        out_specs=pl.BlockSpec((TB, W), lambda c: (c, 0)),
        scratch_shapes=[pltpu.VMEM((W // LW, LW, LW), F32)],
        compiler_params=_cparams(("arbitrary",)),
        name="rwkv_mix",
    )(p, p, mu, w0, w2, a0, a2, g2, k_k, k_a, ln_w, ln_b, r_k, tri, e)


def _pad_cols(w, width):
    return jnp.pad(w, ((0, 0), (0, width - w.shape[1])))


def _rope_gap_layout(w):
    z = jnp.zeros(w.shape[:-1] + (32,), w.dtype)
    return jnp.concatenate([w[..., :32], z, w[..., 32:], z], axis=-1)


def _even_mixer(hn, tabs, w_in, q_norm, w_uq, kv_norm, w_ukv, q_head_norm, k_head_norm,
                mu, w0, w2, a0, a2, g2, k_k, k_a, r_k, ln_w, ln_b, w_out):
    cos1, sin1 = tabs[0], tabs[1]
    W = RWKV_WIDTH
    o_cq = 0
    o_ckv = MLA_Q_LORA
    o_kr = o_ckv + MLA_KV_LORA
    o_rw = o_kr + MLA_ROPE
    w_in = w_in.astype(BF16)
    rw = w_in[:, o_rw:]

    def rwkv_layout(t):
        return jnp.concatenate([t[:, :3 * W], _pad_cols(t[:, 3 * W:3 * W + RWKV_W_LORA], LANES),
                                _pad_cols(t[:, 3 * W + RWKV_W_LORA:3 * W + RWKV_W_LORA + RWKV_A_LORA], LANES),
                                _pad_cols(t[:, 3 * W + RWKV_W_LORA + RWKV_A_LORA:], 2 * LANES)], axis=1)

    w_in_p = jnp.concatenate([rwkv_layout(rw), w_in[:, o_cq:o_ckv], w_in[:, o_ckv:o_kr],
                              _rope_gap_layout(w_in[:, o_kr:o_rw]), jnp.zeros((w_in.shape[0], LANES), BF16)], axis=1)
    p = _proj(hn, w_in_p, tn=1536)

    wq = w_uq.astype(BF16).reshape(MLA_Q_LORA, MLA_HEADS, MLA_QK)
    wq = jnp.concatenate([wq[..., :MLA_NOPE], _rope_gap_layout(wq[..., MLA_NOPE:])], axis=-1)
    wq = wq.reshape(MLA_Q_LORA, MLA_HEADS * MLA_QK_PAD)
    scale = MLA_QK ** -0.5 * LOG2E
    hg_q = jnp.concatenate([q_head_norm[:MLA_NOPE], _rope_gap_layout(q_head_norm[MLA_NOPE:])])[None, :] * scale
    gn = k_head_norm[None, :MLA_NOPE]
    gr = _rope_gap_layout(k_head_norm[MLA_NOPE:])[None, :]
    q, kmat, vmat = _mla_qkv(p, q_norm[None, :], kv_norm[None, :], wq, w_ukv.astype(BF16), hg_q, gn, gr, cos1, sin1)
    o_a = _mla_attention(q, kmat, vmat)

    w2p = jnp.pad(w2, ((0, LANES - RWKV_W_LORA), (0, 0))).astype(BF16)
    a2p = jnp.pad(a2, ((0, LANES - RWKV_A_LORA), (0, 0))).astype(BF16)
    g2p = jnp.pad(g2, ((0, 2 * LANES - RWKV_G_LORA), (0, 0))).astype(BF16)
    o_b = _rwkv_mix(p, rwkv_layout(mu[None, :]), w0[None, :], w2p, a0[None, :], a2p, g2p, k_k[None, :], k_a[None, :],
                    ln_w[None, :], ln_b[None, :], r_k.reshape(1, W))
    return o_a, o_b, w_out.astype(BF16)


def _swa_qkv_kernel(x_ref, w_ref, hg_ref, cos_ref, sin_ref, o_ref):
    n = pl.program_id(0)
    half = x_ref.shape[0] // 2

    @pl.when(n < 2)
    def _():
        for r0 in (0, half):
            rows = slice(r0, r0 + half)
            acc = jnp.dot(x_ref[rows, :], w_ref[...], preferred_element_type=F32)
            cos = cos_ref[rows, :]
            sin = sin_ref[rows, :]
            for c in range(acc.shape[1] // LANES):
                sl = slice(c * LANES, (c + 1) * LANES)
                o_ref[rows, sl] = _rope_apply(_rms(acc[:, sl], hg_ref[:, sl]), cos, sin).astype(o_ref.dtype)

    @pl.when(n == 2)
    def _():
        o_ref[...] = jnp.dot(x_ref[...], w_ref[...], preferred_element_type=F32).astype(o_ref.dtype)


def _swa_qkv(x, w, hg, cos, sin, *, tm=1024):
    S, K = x.shape
    tn = SWA_HEADS * SWA_DIM
    tm = min(tm, S)
    tab = pl.BlockSpec((tm, LANES), lambda n, i: (i, 0))
    return pl.pallas_call(
        _swa_qkv_kernel,
        out_shape=jax.ShapeDtypeStruct((S, 3 * tn), BF16),
        grid=(3, S // tm),
        in_specs=[pl.BlockSpec((tm, K), lambda n, i: (i, 0)),
                  pl.BlockSpec((K, tn), lambda n, i: (0, n)),
                  pl.BlockSpec((1, tn), lambda n, i: (0, jnp.minimum(n, 1))),
                  tab, tab],
        out_specs=pl.BlockSpec((tm, tn), lambda n, i: (i, n)),
        compiler_params=_cparams(("parallel", "parallel")),
        name="swa_qkv",
    )(x, w, hg, cos, sin)


def _dilated_bias(t, nrel):
    row = np.arange(t)[:, None]
    col = np.arange(t)[None, :]
    out = np.empty((nrel, t, t), np.float32)
    for r in range(nrel):
        delta = r * t + row - col
        cnt = np.zeros((t, t), np.float64)
        for window, dilation in SWA_PATTERNS:
            cnt += (delta >= 0) & (delta <= window) & (delta % dilation == 0)
        with np.errstate(divide="ignore"):
            out[r] = np.log2(cnt)
    return out


def _dilated_kernel(q_ref, k_ref, v_ref, bias_ref, o_ref, m_sc, acc_sc, *, sb, nrel, live):
    qi = pl.program_id(1)
    r = pl.program_id(2)
    t = q_ref.shape[0]

    @pl.when(r == 0)
    def _():
        _flash_init(m_sc, acc_sc)

    for rr in range(nrel):
        @pl.when((r == rr) & (qi - r >= 0))
        def _(rr=rr):
            v1 = _with_ones(v_ref[...])
            for rb in range(t // sb):
                rows = slice(rb * sb, (rb + 1) * sb)
                lo, hi = live[rr][rb]
                s = _dot_nt(q_ref[rows, :], k_ref[lo:hi, :]) + bias_ref[rr, rows, lo:hi]
                _flash_rows(s, v1[lo:hi, :], rows, m_sc, acc_sc)

    @pl.when(r == nrel - 1)
    def _():
        _flash_finish(acc_sc, o_ref)


def _dilated_attention(qkv, *, t=1024, sb=256):
    S = qkv.shape[0]
    t = min(t, S)
    sb = min(sb, t)
    max_window = max(w for w, _ in SWA_PATTERNS)
    nrel = min(-(-max_window // t) + 1, S // t)
    H = SWA_HEADS
    bias = _dilated_bias(t, nrel)
    live = []
    for r in range(nrel):
        per_rb = []
        for rb in range(t // sb):
            cols = np.nonzero(np.isfinite(bias[r, rb * sb:(rb + 1) * sb]).any(axis=0))[0]
            lo = int(cols.min()) // LANES * LANES
            hi = -(-(int(cols.max()) + 1) // LANES) * LANES
            per_rb.append((lo, hi))
        live.append(per_rb)
    stat = pltpu.VMEM((t, LANES), F32)
    return pl.pallas_call(
        functools.partial(_dilated_kernel, sb=sb, nrel=nrel, live=live),
        out_shape=jax.ShapeDtypeStruct((S, H * SWA_DIM), BF16),
        grid=(H, S // t, nrel),
        in_specs=[pl.BlockSpec((t, SWA_DIM), lambda h, i, r: (i, h)),
                  pl.BlockSpec((t, SWA_DIM), lambda h, i, r: (jnp.maximum(i - r, 0), H + h)),
                  pl.BlockSpec((t, SWA_DIM), lambda h, i, r: (jnp.maximum(i - r, 0), 2 * H + h)),
                  pl.BlockSpec((nrel, t, t), lambda h, i, r: (0, 0, 0))],
        out_specs=pl.BlockSpec((t, SWA_DIM), lambda h, i, r: (i, h)),
        scratch_shapes=[stat, pltpu.VMEM((t, 2 * SWA_DIM), F32)],
        compiler_params=_cparams(("parallel", "parallel", "arbitrary")),
        name="dilated_attention",
    )(qkv, qkv, qkv, jnp.asarray(bias))


def _gla_kernel(q_ref, k_ref, v_ref, glr_ref, wgu_ref, bg_ref, rd_ref, gn_ref, tri_ref, o_ref,
                st_sc, sall_sc, oi_sc, *, G):
    C = GLA_CHUNK
    SB = GLA_SUB
    DK = GLA_DK
    DV = GLA_DV
    c_idx = pl.program_id(1)

    @pl.when(c_idx == 0)
    def _():
        st_sc[...] = jnp.zeros(st_sc.shape, F32)

    z = _dot(glr_ref[...], wgu_ref[...]) + bg_ref[...]
    lg = -_softplus(-z) * (1.0 / GLA_NORMALIZER)
    bc = _bsplit_cumsum(tri_ref[...], lg.reshape(G, C, DK))
    q = (q_ref[...] * (DK ** -0.5)).reshape(G, C, DK)
    k = k_ref[...].reshape(G, C, DK)
    v = v_ref[...].reshape(G, C, DV)
    blast = bc[:, C - 1:C, :]
    khat = k * jnp.exp(blast - bc)
    gkv = _bmm(jnp.swapaxes(v, 1, 2), khat)
    eb = jnp.exp(blast)

    st = st_sc[...]
    for c in range(G):
        sall_sc[c] = st
        st = st * eb[c] + gkv[c]
    st_sc[...] = st

    ti = lax.broadcasted_iota(jnp.int32, (1, SB, SB, 1), 1)
    si = lax.broadcasted_iota(jnp.int32, (1, SB, SB, 1), 2)
    causal = si <= ti
    for sb in range(C // SB):
        lo = sb * SB
        qs = q[:, lo:lo + SB]
        ks = k[:, lo:lo + SB]
        bs = bc[:, lo:lo + SB]
        rel = bs[:, :, None, :] - bs[:, None, :, :]
        dec = jnp.exp(jnp.where(causal, rel, -jnp.inf))
        att = jnp.sum(qs[:, :, None, :] * ks[:, None, :, :] * dec, axis=-1)
        o_sb = _bmm(att, v[:, lo:lo + SB])
        if sb > 0:
            bm = bc[:, lo - 1:lo]
            qsc = qs * jnp.exp(bs - bm)
            ksc = k[:, :lo] * jnp.exp(bm - bc[:, :lo])
            o_sb = o_sb + _bmm(_bmm_nt(qsc, ksc), v[:, :lo])
        oi_sc[:, lo:lo + SB, :] = o_sb

    o = (oi_sc[...] + _bmm_nt(q * jnp.exp(bc), sall_sc[...])).reshape(G * C, DV)
    rd = rd_ref[...]
    o_ref[...] = (_rms(o, gn_ref[...]) * (rd * _sigmoid(rd))).astype(o_ref.dtype)


def _gla(pb, wgu, bg, gn, *, G=32):
    S = pb.shape[0]
    C = GLA_CHUNK
    G = min(G, S // C)
    TB = G * C
    H, DK, DV = GLA_HEADS, GLA_DK, GLA_DV
    t = np.arange(C)
    tri = jnp.asarray((t[None, :] <= t[:, None]).astype(np.float32), BF16)
    glr_block = (2 * H * DK + 2 * H * DV) // LANES
    return pl.pallas_call(
        functools.partial(_gla_kernel, G=G),
        out_shape=jax.ShapeDtypeStruct((S, H * DV), BF16),
        grid=(H, S // TB),
        in_specs=[pl.BlockSpec((TB, DK), lambda h, c: (c, h)),
                  pl.BlockSpec((TB, DK), lambda h, c: (c, H + h)),
                  pl.BlockSpec((TB, DV), lambda h, c: (c, (2 * H * DK) // DV + h)),
                  pl.BlockSpec((TB, LANES), lambda h, c: (c, glr_block)),
                  pl.BlockSpec((LANES, DK), lambda h, c: (0, h)),
                  pl.BlockSpec((1, DK), lambda h, c: (0, h)),
                  pl.BlockSpec((TB, DV), lambda h, c: (c, (2 * H * DK + H * DV) // DV + h)),
                  pl.BlockSpec((1, DV), lambda h, c: (0, h)),
                  pl.BlockSpec((C, C), lambda h, c: (0, 0))],
        out_specs=pl.BlockSpec((TB, DV), lambda h, c: (c, h)),
        scratch_shapes=[pltpu.VMEM((DV, DK), F32), pltpu.VMEM((G, DV, DK), F32), pltpu.VMEM((G, C, DV), F32)],
        compiler_params=_cparams(("parallel", "arbitrary")),
        name="gla",
    )(pb, pb, pb, pb, wgu, bg, pb, gn, tri)


def _odd_mixer(hn, tabs, w_in, q_head_norm, k_head_norm, w_gate_up, b_gate, gla_norm, w_out):
    cos2, sin2 = tabs[2], tabs[3]
    nq = SWA_HEADS * SWA_DIM
    hg = jnp.concatenate([jnp.tile(q_head_norm * (SWA_DIM ** -0.5 * LOG2E), SWA_HEADS), jnp.tile(k_head_norm, SWA_HEADS)])[None, :]
    w_in = w_in.astype(BF16)
    qkv = _swa_qkv(hn, w_in, hg, cos2, sin2)
    o_c = _dilated_attention(qkv)
    o = 3 * nq
    dk, dv = GLA_HEADS * GLA_DK, GLA_HEADS * GLA_DV
    wb = jnp.concatenate([w_in[:, o:o + 2 * dk + dv], w_in[:, o + 2 * dk + dv + GLA_LORA:],
                          _pad_cols(w_in[:, o + 2 * dk + dv:o + 2 * dk + dv + GLA_LORA], LANES)], axis=1)
    pb = _proj(hn, wb, tm=512, tn=wb.shape[1])
    wgu = jnp.pad(w_gate_up, ((0, LANES - GLA_LORA), (0, 0))).astype(BF16)
    o_d = _gla(pb, wgu, b_gate[None, :], gla_norm.reshape(1, dv))
    return o_c, o_d, w_out.astype(BF16)


def kernel(x, positions, ffn_norm, ffn_w_gate, ffn_w_up, ffn_w_down, mix_norm, mla_rwkv_w_in, mla_q_norm, mla_w_uq,
           mla_kv_norm, mla_w_ukv, mla_q_head_norm, mla_k_head_norm, rwkv_mu, rwkv_w0, rwkv_w2, rwkv_a0, rwkv_a2,
           rwkv_g2, rwkv_k_k, rwkv_k_a, rwkv_r_k, rwkv_ln_w, rwkv_ln_b, mla_rwkv_w_out, swa_gla_w_in,
           swa_q_head_norm, swa_k_head_norm, gla_w_gate_up, gla_b_gate, gla_norm, swa_gla_w_out):
    B, S, D = x.shape
    assert B == 1
    h = x.reshape(S, D)
    tabs = _rope_tables(positions.reshape(S))
    wg = ffn_w_gate.astype(BF16)
    wu = ffn_w_up.astype(BF16)
    wd = ffn_w_down.astype(BF16)
    depth = ffn_norm.shape[0]
    for layer in range(depth):
        i = layer // 2
        h, hn = _ffn(h, ffn_norm, wg, wu, wd, layer, 0, next_norm=mix_norm[layer])
        if layer % 2 == 0:
            mix = _even_mixer(hn, tabs, mla_rwkv_w_in[i], mla_q_norm[i], mla_w_uq[i], mla_kv_norm[i],
                            mla_w_ukv[i], mla_q_head_norm[i], mla_k_head_norm[i], rwkv_mu[i], rwkv_w0[i],
                            rwkv_w2[i], rwkv_a0[i], rwkv_a2[i], rwkv_g2[i], rwkv_k_k[i], rwkv_k_a[i], rwkv_r_k[i],
                            rwkv_ln_w[i], rwkv_ln_b[i], mla_rwkv_w_out[i])
        else:
            mix = _odd_mixer(hn, tabs, swa_gla_w_in[i], swa_q_head_norm[i], swa_k_head_norm[i],
                           gla_w_gate_up[i], gla_b_gate[i], gla_norm[i], swa_gla_w_out[i])
        h = _ffn(h, ffn_norm, wg, wu, wd, layer, 1, mix=mix)
    return h.reshape(B, S, D)
```

```python
import functools

import numpy as np
import jax
import jax.numpy as jnp
from jax import lax
from jax.experimental import pallas as pl
from jax.experimental.pallas import tpu as pltpu

F32 = jnp.float32
BF16 = jnp.bfloat16

LANES = 128
VMEM_LIMIT = 56 * 1024 * 1024

D_MODEL = 2048
D_FF = 5632
MACARON_WEIGHT = 0.5
NORM_EPS = 1e-6
ROPE_THETA = 10000.0
LOG2E = float(np.log2(np.e))
EXP_NEG_HALF = float(np.exp(-0.5))

MLA_HEADS = 8
MLA_NOPE = 128
MLA_ROPE = 64
MLA_QK = MLA_NOPE + MLA_ROPE
MLA_QK_PAD = 256
MLA_V = 128
MLA_Q_LORA = 512
MLA_KV_LORA = 256

RWKV_HEAD = 64
RWKV_WIDTH = 1024
RWKV_HEADS = 16
RWKV_W_LORA = 64
RWKV_A_LORA = 64
RWKV_G_LORA = 160
RWKV_GN_EPS = 64e-5
RWKV_SEC = 3584
RWKV_CHUNK = 64

SWA_HEADS = 8
SWA_DIM = 128
SWA_PATTERNS = ((128, 1), (512, 4), (2048, 16))

GLA_HEADS = 4
GLA_DK = 128
GLA_DV = 256
GLA_LORA = 16
GLA_NORMALIZER = 16.0
GLA_CHUNK = 64
GLA_SUB = 8


def _cparams(sem):
    return pltpu.CompilerParams(dimension_semantics=sem, vmem_limit_bytes=VMEM_LIMIT)


def _rms(x, g):
    return x * lax.rsqrt(jnp.mean(x * x, axis=-1, keepdims=True) + NORM_EPS) * g


def _dot(a, b):
    return jnp.dot(a.astype(BF16), b.astype(BF16), preferred_element_type=F32)


def _dot_nt(a, b):
    return lax.dot_general(a.astype(BF16), b.astype(BF16), (((1,), (1,)), ((), ())),
                           preferred_element_type=F32)


def _sigmoid(x):
    return 0.5 * jnp.tanh(0.5 * x) + 0.5


def _softplus(x):
    return jnp.maximum(x, 0.0) + jnp.log(1.0 + jnp.exp(-jnp.abs(x)))


def _rope_apply(x, cos, sin_signed):
    return x * cos + pltpu.roll(x, 64, 1) * sin_signed


def _ffn_kernel(*refs, emit_norm, mix):
    it = iter(refs)
    h_ref, g_ref, wg_ref, wu_ref, wd_ref = (next(it) for _ in range(5))
    if mix:
        a_ref, b_ref, wa_ref, wb_ref = (next(it) for _ in range(4))
    if emit_norm:
        g2_ref = next(it)
    o_ref = next(it)
    if emit_norm:
        hn_ref = next(it)
    xn_ref = next(it)
    j = pl.program_id(1)

    @pl.when(j == 0)
    def _():
        x = h_ref[...]
        if mix:
            x = (x + jnp.dot(a_ref[...], wa_ref[...], preferred_element_type=F32)
                 + jnp.dot(b_ref[...], wb_ref[...], preferred_element_type=F32))
        xn_ref[...] = _rms(x, g_ref[...]).astype(BF16)
        o_ref[...] = x

    xn = xn_ref[...]
    gate = jnp.dot(xn, wg_ref[...], preferred_element_type=F32)
    up = jnp.dot(xn, wu_ref[...], preferred_element_type=F32)
    act = (gate * _sigmoid(gate) * up * MACARON_WEIGHT).astype(BF16)
    o_ref[...] += jnp.dot(act, wd_ref[...], preferred_element_type=F32)

    if emit_norm:
        @pl.when(j == pl.num_programs(1) - 1)
        def _():
            hn_ref[...] = _rms(o_ref[...], g2_ref[...]).astype(hn_ref.dtype)


def _ffn(h, g, wg, wu, wd, layer, k, next_norm=None, mix=None, *, tm=512, tf=512):
    S, D = h.shape
    F = wg.shape[-1]
    tm = min(tm, S)
    emit_norm = next_norm is not None
    wspec = pl.BlockSpec((None, None, D, tf), lambda i, j: (layer, k, 0, j))
    row = pl.BlockSpec((1, D), lambda i, j: (0, 0))
    tile = pl.BlockSpec((tm, D), lambda i, j: (i, 0))
    in_specs = [tile, row, wspec, wspec, pl.BlockSpec((None, None, tf, D), lambda i, j: (layer, k, j, 0))]
    args = [h, g[layer, k][None, :], wg, wu, wd]
    if mix is not None:
        a, b, w_out = mix
        ka = a.shape[1]
        assert b.shape[1] == ka and w_out.shape == (2 * ka, D)
        half = pl.BlockSpec((tm, ka), lambda i, j: (i, 0))
        in_specs += [half, half, pl.BlockSpec((ka, D), lambda i, j: (0, 0)), pl.BlockSpec((ka, D), lambda i, j: (1, 0))]
        args += [a, b, w_out, w_out]
    out_shape = jax.ShapeDtypeStruct((S, D), F32)
    out_specs = tile
    if emit_norm:
        in_specs.append(row)
        args.append(next_norm[None, :])
        out_shape = (out_shape, jax.ShapeDtypeStruct((S, D), BF16))
        out_specs = (tile, tile)
    return pl.pallas_call(
        functools.partial(_ffn_kernel, emit_norm=emit_norm, mix=mix is not None),
        out_shape=out_shape,
        grid=(S // tm, F // tf),
        in_specs=in_specs,
        out_specs=out_specs,
        scratch_shapes=[pltpu.VMEM((tm, D), BF16)],
        compiler_params=_cparams(("parallel", "arbitrary")),
        name="ffn",
    )(*args)


def _rope_tab_kernel(pos_ref, f1_ref, s1_ref, f2_ref, s2_ref, c1_ref, n1_ref, c2_ref, n2_ref):
    pos = pos_ref[...].astype(F32)
    a1 = pos * f1_ref[...]
    c1_ref[...] = jnp.cos(a1)
    n1_ref[...] = jnp.sin(a1) * s1_ref[...]
    a2 = pos * f2_ref[...]
    c2_ref[...] = jnp.cos(a2)
    n2_ref[...] = jnp.sin(a2) * s2_ref[...]


def _rope_tables(positions, *, tm=1024):
    S = positions.shape[0]
    tm = min(tm, S)
    f32half = ROPE_THETA ** (-jnp.arange(MLA_ROPE // 2, dtype=F32) / (MLA_ROPE // 2))
    z = jnp.zeros((32,), F32)
    f1 = jnp.concatenate([f32half, z, f32half, z])[None]
    f64half = ROPE_THETA ** (-jnp.arange(SWA_DIM // 2, dtype=F32) / (SWA_DIM // 2))
    f2 = jnp.concatenate([f64half, f64half])[None]
    sign = jnp.concatenate([-jnp.ones((64,), F32), jnp.ones((64,), F32)])[None]
    row = pl.BlockSpec((1, LANES), lambda i: (0, 0))
    tab = pl.BlockSpec((tm, LANES), lambda i: (i, 0))
    shp = jax.ShapeDtypeStruct((S, LANES), F32)
    return pl.pallas_call(
        _rope_tab_kernel,
        out_shape=(shp, shp, shp, shp),
        grid=(S // tm,),
        in_specs=[pl.BlockSpec((tm, 1), lambda i: (i, 0)), row, row, row, row],
        out_specs=(tab, tab, tab, tab),
        compiler_params=_cparams(("parallel",)),
        name="rope_tables",
    )(positions.reshape(S, 1), f1, sign, f2, sign)


def _proj_kernel(x_ref, w_ref, o_ref):
    o_ref[...] = jnp.dot(x_ref[...], w_ref[...], preferred_element_type=F32).astype(o_ref.dtype)


def _proj(x, w, *, tm=1024, tn, out_dtype=F32):
    S, K = x.shape
    N = w.shape[1]
    tm = min(tm, S)
    return pl.pallas_call(
        _proj_kernel,
        out_shape=jax.ShapeDtypeStruct((S, N), out_dtype),
        grid=(N // tn, S // tm),
        in_specs=[pl.BlockSpec((tm, K), lambda n, i: (i, 0)),
                  pl.BlockSpec((K, tn), lambda n, i: (0, n))],
        out_specs=pl.BlockSpec((tm, tn), lambda n, i: (i, n)),
        compiler_params=_cparams(("parallel", "parallel")),
        name="proj",
    )(x, w)


def _mla_qkv_kernel(cq_ref, ckv_ref, kr_ref, qn_ref, kvn_ref, wq_ref, wkv_ref, hgq_ref, gn_ref, gr_ref,
                    cos_ref, sin_ref, q_ref, k_ref, v_ref):
    cos = cos_ref[...]
    sin = sin_ref[...]
    xq = _rms(cq_ref[...], qn_ref[...]).astype(BF16)
    xkv = _rms(ckv_ref[...], kvn_ref[...]).astype(BF16)
    kr = kr_ref[...]
    kr_ss = jnp.sum(kr * kr, axis=-1, keepdims=True)
    inv_d = 1.0 / MLA_QK
    for h in range(MLA_HEADS):
        lo = h * MLA_QK_PAD
        a = jnp.dot(xq, wq_ref[:, lo:lo + MLA_QK_PAD], preferred_element_type=F32)
        y = a * lax.rsqrt(jnp.sum(a * a, axis=-1, keepdims=True) * inv_d + NORM_EPS) * hgq_ref[...]
        q_ref[:, lo:lo + LANES] = y[:, :LANES].astype(q_ref.dtype)
        q_ref[:, lo + LANES:lo + MLA_QK_PAD] = _rope_apply(y[:, LANES:], cos, sin).astype(q_ref.dtype)
        kv = jnp.dot(xkv, wkv_ref[:, lo:lo + MLA_QK_PAD], preferred_element_type=F32)
        kn = kv[:, :LANES]
        rs = lax.rsqrt((jnp.sum(kn * kn, axis=-1, keepdims=True) + kr_ss) * inv_d + NORM_EPS)
        k_ref[:, lo:lo + LANES] = (kn * rs * gn_ref[...]).astype(k_ref.dtype)
        k_ref[:, lo + LANES:lo + MLA_QK_PAD] = _rope_apply(kr * rs * gr_ref[...], cos, sin).astype(k_ref.dtype)
        v_ref[:, h * MLA_V:(h + 1) * MLA_V] = kv[:, LANES:].astype(v_ref.dtype)


def _mla_qkv(p, qn, kvn, wq, wkv, hgq, gn, gr, cos, sin, *, tm=512):
    S = p.shape[0]
    tm = min(tm, S)
    cq_block = RWKV_SEC // MLA_Q_LORA
    ckv_block = (RWKV_SEC + MLA_Q_LORA) // MLA_KV_LORA
    kr_block = (RWKV_SEC + MLA_Q_LORA + MLA_KV_LORA) // LANES
    NQ = MLA_HEADS * MLA_QK_PAD
    tab = pl.BlockSpec((tm, LANES), lambda i: (i, 0))
    full = lambda a: pl.BlockSpec(a.shape, lambda i: (0, 0))
    return pl.pallas_call(
        _mla_qkv_kernel,
        out_shape=(jax.ShapeDtypeStruct((S, NQ), BF16), jax.ShapeDtypeStruct((S, NQ), BF16),
                   jax.ShapeDtypeStruct((S, MLA_HEADS * MLA_V), BF16)),
        grid=(S // tm,),
        in_specs=[pl.BlockSpec((tm, MLA_Q_LORA), lambda i: (i, cq_block)),
                  pl.BlockSpec((tm, MLA_KV_LORA), lambda i: (i, ckv_block)),
                  pl.BlockSpec((tm, LANES), lambda i: (i, kr_block)),
                  full(qn), full(kvn), full(wq), full(wkv), full(hgq), full(gn), full(gr), tab, tab],
        out_specs=(pl.BlockSpec((tm, NQ), lambda i: (i, 0)), pl.BlockSpec((tm, NQ), lambda i: (i, 0)),
                   pl.BlockSpec((tm, MLA_HEADS * MLA_V), lambda i: (i, 0))),
        compiler_params=_cparams(("parallel",)),
        name="mla_qkv",
    )(p, p, p, qn, kvn, wq, wkv, hgq, gn, gr, cos, sin)


def _flash_rows(s, v1, rows, m_sc, acc_sc):
    m_prev = m_sc[rows, :]
    m_new = jnp.maximum(m_prev, jnp.max(s, axis=-1, keepdims=True))
    alpha = jnp.exp2(m_prev - m_new)
    pr = jnp.exp2((s - jnp.concatenate([m_new] * (s.shape[1] // LANES), axis=1)).astype(BF16))
    acc_sc[rows, :] = (jnp.concatenate([alpha, alpha], axis=1) * acc_sc[rows, :]
                       + jnp.dot(pr, v1, preferred_element_type=F32))
    m_sc[rows, :] = m_new


def _flash_init(m_sc, acc_sc):
    m_sc[...] = jnp.full(m_sc.shape, -jnp.inf, F32)
    acc_sc[...] = jnp.zeros(acc_sc.shape, F32)


def _with_ones(v):
    return jnp.concatenate([v, jnp.ones(v.shape, v.dtype)], axis=1)


def _flash_finish(acc_sc, o_ref):
    acc = acc_sc[...]
    d = acc.shape[1] // 2
    o_ref[...] = (acc[:, :d] / acc[:, d:]).astype(o_ref.dtype)


def _mla_attn_kernel(qi_ref, ki_ref, q_ref, k_ref, v_ref, o_ref, m_sc, acc_sc, *, sb):
    p = pl.program_id(1)
    qi = qi_ref[p]
    ki = ki_ref[p]
    t = q_ref.shape[0]

    @pl.when(ki == 0)
    def _():
        _flash_init(m_sc, acc_sc)

    @pl.when(ki < qi)
    def _():
        v1 = _with_ones(v_ref[...])
        for rb in range(t // sb):
            rows = slice(rb * sb, (rb + 1) * sb)
            s = _dot_nt(q_ref[rows, :], k_ref[...])
            _flash_rows(s, v1, rows, m_sc, acc_sc)

    @pl.when(ki == qi)
    def _():
        v1 = _with_ones(v_ref[...])
        for rb in range(t // sb):
            rows = slice(rb * sb, (rb + 1) * sb)
            nc = (rb + 1) * sb
            s = _dot_nt(q_ref[rows, :], k_ref[0:nc, :])
            row = lax.broadcasted_iota(jnp.int32, s.shape, 0) + rb * sb
            col = lax.broadcasted_iota(jnp.int32, s.shape, 1)
            s = jnp.where(col <= row, s, -jnp.inf)
            _flash_rows(s, v1[0:nc, :], rows, m_sc, acc_sc)
        _flash_finish(acc_sc, o_ref)


def _mla_attention(q, k, v, *, t=2048, sb=256):
    S = q.shape[0]
    t = min(t, S)
    sb = min(sb, t)
    nb = S // t
    pairs = [(a, b) for a in range(nb) for b in range(a + 1)]
    qi_tab = jnp.asarray(np.array([a for a, _ in pairs], np.int32))
    ki_tab = jnp.asarray(np.array([b for _, b in pairs], np.int32))
    stat = pltpu.VMEM((t, LANES), F32)
    grid_spec = pltpu.PrefetchScalarGridSpec(
        num_scalar_prefetch=2,
        grid=(MLA_HEADS, len(pairs)),
        in_specs=[pl.BlockSpec((t, MLA_QK_PAD), lambda h, p, qt, kt: (qt[p], h)),
                  pl.BlockSpec((t, MLA_QK_PAD), lambda h, p, qt, kt: (kt[p], h)),
                  pl.BlockSpec((t, MLA_V), lambda h, p, qt, kt: (kt[p], h))],
        out_specs=pl.BlockSpec((t, MLA_V), lambda h, p, qt, kt: (qt[p], h)),
        scratch_shapes=[stat, pltpu.VMEM((t, 2 * MLA_V), F32)],
    )
    return pl.pallas_call(
        functools.partial(_mla_attn_kernel, sb=sb),
        out_shape=jax.ShapeDtypeStruct((S, MLA_HEADS * MLA_V), BF16),
        grid_spec=grid_spec,
        compiler_params=_cparams(("parallel", "arbitrary")),
        name="mla_attention",
    )(qi_tab, ki_tab, q, k, v)


def _bmm(a, b):
    return jnp.einsum('gik,gkj->gij', a.astype(BF16), b.astype(BF16), preferred_element_type=F32)


def _bmm_nt(a, b):
    return jnp.einsum('gik,gjk->gij', a.astype(BF16), b.astype(BF16), preferred_element_type=F32)


def _bsplit_cumsum(tri, x):
    G = x.shape[0]
    tb = jnp.broadcast_to(tri.astype(BF16)[None], (G,) + tri.shape)
    x0 = x.astype(BF16)
    r1 = x - x0.astype(F32)
    x1 = r1.astype(BF16)
    x2 = (r1 - x1.astype(F32)).astype(BF16)
    f = lambda t: jnp.einsum('gts,gsl->gtl', tb, t, preferred_element_type=F32)
    return f(x0) + f(x1) + f(x2)


def _rwkv_kernel(p_ref, q_ref, mu_ref, w0_ref, w2_ref, a0_ref, a2_ref, g2_ref, kk_ref, ka_ref, lnw_ref, lnb_ref, rk_ref,
                 tri_ref, e_ref, o_ref, st_sc, *, hg, cps):
    C = RWKV_CHUNK
    W = RWKV_WIDTH
    LW = hg * RWKV_HEAD
    RW = hg * C
    NP = W // LW
    NB = cps * NP
    TB = cps * C
    c_idx = pl.program_id(0)

    @pl.when(c_idx == 0)
    def _():
        st_sc[...] = jnp.zeros(st_sc.shape, F32)

    x = p_ref[...]
    row = lax.broadcasted_iota(jnp.int32, (TB, 1), 0)
    last = jnp.where(c_idx == 0, 0.0, q_ref[7:8, :])
    prev = jnp.where(row == 0, last, pltpu.roll(x, 1, 0))
    xs = x + (prev - x) * mu_ref[...]
    r = xs[:, :W]
    kb = xs[:, W:2 * W]
    v = xs[:, 2 * W:3 * W]
    xw = xs[:, 3 * W:3 * W + LANES]
    xa = xs[:, 3 * W + LANES:3 * W + 2 * LANES]
    xg = xs[:, 3 * W + 2 * LANES:]
    lw = -EXP_NEG_HALF * _sigmoid(w0_ref[...] + _dot(jnp.tanh(xw), w2_ref[...]))
    a = _sigmoid(a0_ref[...] + _dot(xa, a2_ref[...]))
    g = _dot(_sigmoid(xg), g2_ref[...])
    k = kb * (1.0 + (a - 1.0) * ka_ref[...])

    def to_b(t):
        t3 = t.reshape(cps, C, W)
        return jnp.concatenate([t3[c:c + 1, :, LW * p:LW * (p + 1)] for c in range(cps) for p in range(NP)], axis=0)

    def from_b(t):
        rows = [jnp.concatenate([t[c * NP + p] for p in range(NP)], axis=-1) for c in range(cps)]
        return jnp.concatenate(rows, axis=0)

    e_b = jnp.broadcast_to(e_ref[...][None], (NB, LW, LW))

    def seg_sum(t):
        t0 = t.astype(BF16)
        t1 = (t - t0.astype(F32)).astype(BF16)
        return _bmm(t0, e_b) + _bmm(t1, e_b)

    lw3, r3, k3, v3, a3 = to_b(lw), to_b(r), to_b(k), to_b(v), to_b(a)
    kkb = to_b(kb * kk_ref[...])
    kk3 = kkb * lax.rsqrt(jnp.maximum(seg_sum(kkb * kkb), 1e-24))
    b3 = kk3 * a3

    cum3 = _bsplit_cumsum(tri_ref[...], lw3)
    clast = cum3[:, C - 1:C, :]
    einv = jnp.exp(-cum3)
    pc = jnp.exp(clast)
    etail = pc * einv
    rt = r3 * jnp.exp(cum3)
    at = -kk3 * jnp.exp(cum3 - lw3)
    bt = b3 * einv
    kt = k3 * einv
    bh = b3 * etail
    kh = k3 * etail

    lane_head = lax.broadcasted_iota(jnp.int32, (1, 1, LW), 2) // RWKV_HEAD

    def stack(t):
        return jnp.concatenate([jnp.where(lane_head == h, t, 0.0) for h in range(hg)], axis=1)

    a2, b2, k2, r2, v2, bh2, kh2 = (stack(t).astype(BF16) for t in (at, bt, kt, rt, v3, bh, kh))
    ri = lax.broadcasted_iota(jnp.int32, (1, RW, RW), 1)
    ci = lax.broadcasted_iota(jnp.int32, (1, RW, RW), 2)
    same = (ri // C) == (ci // C)
    strict = same & (ci < ri)
    incl = same & (ci <= ri)
    eye = (ri == ci).astype(F32)

    bk2 = jnp.concatenate([b2, k2], axis=1)
    a_bk = _bmm_nt(a2, bk2)
    r_bk = _bmm_nt(r2, bk2)
    lmat = jnp.where(strict, a_bk[:, :, :RW], 0.0)
    akm = jnp.where(strict, a_bk[:, :, RW:], 0.0)
    rbm = jnp.where(incl, r_bk[:, :, :RW], 0.0)
    rkm = jnp.where(incl, r_bk[:, :, RW:], 0.0)
    tinv = eye + lmat
    pw = lmat
    for _ in range(int(np.log2(C)) - 1):
        pw = _bmm(pw, pw)
        tinv = tinv + _bmm(tinv, pw)
    wm = _bmm(tinv, a2)
    z = _bmm(tinv, _bmm(akm, v2))
    qm = r2.astype(F32) + _bmm(rbm, wm)
    y0 = _bmm(rkm, v2) + _bmm(rbm, z)
    kc = _bmm(jnp.swapaxes(wm, 1, 2), bh2)
    nc = _bmm(jnp.swapaxes(z, 1, 2), bh2) + _bmm(jnp.swapaxes(v2, 1, 2), kh2)

    st = st_sc[...]
    y2 = []
    for c in range(cps):
        sl = slice(c * NP, (c + 1) * NP)
        y2.append(_bmm_nt(qm[sl], st) + y0[sl])
        st = st * pc[sl] + _bmm(st, kc[sl]) + nc[sl]
    st_sc[...] = st
    y2 = jnp.concatenate(y2, axis=0)
    y = y2[:, 0:C]
    for h in range(1, hg):
        y = y + y2[:, h * C:(h + 1) * C]

    inv_n = 1.0 / RWKV_HEAD
    d = y - seg_sum(y) * inv_n
    yhat = d * lax.rsqrt(seg_sum(d * d) * inv_n + RWKV_GN_EPS)
    bonus = seg_sum(to_b(r * k * rk_ref[...])) * v3
    out = (from_b(yhat) * lnw_ref[...] + lnb_ref[...] + from_b(bonus)) * g
    o_ref[...] = out.astype(o_ref.dtype)


def _rwkv_mix(p, mu, w0, w2, a0, a2, g2, k_k, k_a, ln_w, ln_b, r_k, *, hg=2, cps=2):
    S = p.shape[0]
    C = RWKV_CHUNK
    cps = min(cps, S // C)
    TB = cps * C
    LW = hg * RWKV_HEAD
    W = RWKV_WIDTH
    t = np.arange(C)
    tri = jnp.asarray((t[None, :] <= t[:, None]).astype(np.float32), BF16)
    l = np.arange(LW)
    e = jnp.asarray((l[:, None] // RWKV_HEAD == l[None, :] // RWKV_HEAD).astype(np.float32), BF16)
    full = lambda arr: pl.BlockSpec(arr.shape, lambda c: (0, 0))
    return pl.pallas_call(
        functools.partial(_rwkv_kernel, hg=hg, cps=cps),
        out_shape=jax.ShapeDtypeStruct((S, W), BF16),
        grid=(S // TB,),
        in_specs=[pl.BlockSpec((TB, RWKV_SEC), lambda c: (c, 0)),
                  pl.BlockSpec((8, RWKV_SEC), lambda c: (jnp.maximum(c * (TB // 8) - 1, 0), 0)),
                  full(mu), full(w0), full(w2), full(a0), full(a2), full(g2), full(k_k), full(k_a),
                  full(ln_w), full(ln_b), full(r_k), full(tri), full(e)],
        out_specs=pl.BlockSpec((TB, W), lambda c: (c, 0)),
        scratch_shapes=[pltpu.VMEM((W // LW, LW, LW), F32)],
        compiler_params=_cparams(("arbitrary",)),
        name="rwkv_mix",
    )(p, p, mu, w0, w2, a0, a2, g2, k_k, k_a, ln_w, ln_b, r_k, tri, e)


def _pad_cols(w, width):
    return jnp.pad(w, ((0, 0), (0, width - w.shape[1])))


def _rope_gap_layout(w):
    z = jnp.zeros(w.shape[:-1] + (32,), w.dtype)
    return jnp.concatenate([w[..., :32], z, w[..., 32:], z], axis=-1)


def _even_mixer(hn, tabs, w_in, q_norm, w_uq, kv_norm, w_ukv, q_head_norm, k_head_norm,
                mu, w0, w2, a0, a2, g2, k_k, k_a, r_k, ln_w, ln_b, w_out):
    cos1, sin1 = tabs[0], tabs[1]
    W = RWKV_WIDTH
    o_cq = 0
    o_ckv = MLA_Q_LORA
    o_kr = o_ckv + MLA_KV_LORA
    o_rw = o_kr + MLA_ROPE
    w_in = w_in.astype(BF16)
    rw = w_in[:, o_rw:]

    def rwkv_layout(t):
        return jnp.concatenate([t[:, :3 * W], _pad_cols(t[:, 3 * W:3 * W + RWKV_W_LORA], LANES),
                                _pad_cols(t[:, 3 * W + RWKV_W_LORA:3 * W + RWKV_W_LORA + RWKV_A_LORA], LANES),
                                _pad_cols(t[:, 3 * W + RWKV_W_LORA + RWKV_A_LORA:], 2 * LANES)], axis=1)

    w_in_p = jnp.concatenate([rwkv_layout(rw), w_in[:, o_cq:o_ckv], w_in[:, o_ckv:o_kr],
                              _rope_gap_layout(w_in[:, o_kr:o_rw]), jnp.zeros((w_in.shape[0], LANES), BF16)], axis=1)
    p = _proj(hn, w_in_p, tn=1536)

    wq = w_uq.astype(BF16).reshape(MLA_Q_LORA, MLA_HEADS, MLA_QK)
    wq = jnp.concatenate([wq[..., :MLA_NOPE], _rope_gap_layout(wq[..., MLA_NOPE:])], axis=-1)
    wq = wq.reshape(MLA_Q_LORA, MLA_HEADS * MLA_QK_PAD)
    scale = MLA_QK ** -0.5 * LOG2E
    hg_q = jnp.concatenate([q_head_norm[:MLA_NOPE], _rope_gap_layout(q_head_norm[MLA_NOPE:])])[None, :] * scale
    gn = k_head_norm[None, :MLA_NOPE]
    gr = _rope_gap_layout(k_head_norm[MLA_NOPE:])[None, :]
    q, kmat, vmat = _mla_qkv(p, q_norm[None, :], kv_norm[None, :], wq, w_ukv.astype(BF16), hg_q, gn, gr, cos1, sin1)
    o_a = _mla_attention(q, kmat, vmat)

    w2p = jnp.pad(w2, ((0, LANES - RWKV_W_LORA), (0, 0))).astype(BF16)
    a2p = jnp.pad(a2, ((0, LANES - RWKV_A_LORA), (0, 0))).astype(BF16)
    g2p = jnp.pad(g2, ((0, 2 * LANES - RWKV_G_LORA), (0, 0))).astype(BF16)
    o_b = _rwkv_mix(p, rwkv_layout(mu[None, :]), w0[None, :], w2p, a0[None, :], a2p, g2p, k_k[None, :], k_a[None, :],
                    ln_w[None, :], ln_b[None, :], r_k.reshape(1, W))
    return o_a, o_b, w_out.astype(BF16)


def _swa_qkv_kernel(x_ref, w_ref, hg_ref, cos_ref, sin_ref, o_ref):
    n = pl.program_id(0)
    half = x_ref.shape[0] // 2

    @pl.when(n < 2)
    def _():
        for r0 in (0, half):
            rows = slice(r0, r0 + half)
            acc = jnp.dot(x_ref[rows, :], w_ref[...], preferred_element_type=F32)
            cos = cos_ref[rows, :]
            sin = sin_ref[rows, :]
            for c in range(acc.shape[1] // LANES):
                sl = slice(c * LANES, (c + 1) * LANES)
                o_ref[rows, sl] = _rope_apply(_rms(acc[:, sl], hg_ref[:, sl]), cos, sin).astype(o_ref.dtype)

    @pl.when(n == 2)
    def _():
        o_ref[...] = jnp.dot(x_ref[...], w_ref[...], preferred_element_type=F32).astype(o_ref.dtype)


def _swa_qkv(x, w, hg, cos, sin, *, tm=1024):
    S, K = x.shape
    tn = SWA_HEADS * SWA_DIM
    tm = min(tm, S)
    tab = pl.BlockSpec((tm, LANES), lambda n, i: (i, 0))
    return pl.pallas_call(
        _swa_qkv_kernel,
        out_shape=jax.ShapeDtypeStruct((S, 3 * tn), BF16),
        grid=(3, S // tm),
        in_specs=[pl.BlockSpec((tm, K), lambda n, i: (i, 0)),
                  pl.BlockSpec((K, tn), lambda n, i: (0, n)),
                  pl.BlockSpec((1, tn), lambda n, i: (0, jnp.minimum(n, 1))),
                  tab, tab],
        out_specs=pl.BlockSpec((tm, tn), lambda n, i: (i, n)),
        compiler_params=_cparams(("parallel", "parallel")),
        name="swa_qkv",
    )(x, w, hg, cos, sin)


def _dilated_bias(t, nrel):
    row = np.arange(t)[:, None]
    col = np.arange(t)[None, :]
    out = np.empty((nrel, t, t), np.float32)
    for r in range(nrel):
        delta = r * t + row - col
        cnt = np.zeros((t, t), np.float64)
        for window, dilation in SWA_PATTERNS:
            cnt += (delta >= 0) & (delta <= window) & (delta % dilation == 0)
        with np.errstate(divide="ignore"):
            out[r] = np.log2(cnt)
    return out


def _dilated_kernel(q_ref, *refs, sb, nrel, live):
    kv = refs[:2 * nrel]
    bias_ref, o_ref, m_sc, acc_sc = refs[2 * nrel:]
    qi = pl.program_id(1)
    t = q_ref.shape[0]
    _flash_init(m_sc, acc_sc)

    def block(rr):
        k_ref, v_ref = kv[2 * rr], kv[2 * rr + 1]
        v1 = _with_ones(v_ref[...])
        for rb in range(t // sb):
            rows = slice(rb * sb, (rb + 1) * sb)
            lo, hi = live[rr][rb]
            s = _dot_nt(q_ref[rows, :], k_ref[lo:hi, :]) + bias_ref[rr, rows, lo:hi]
            _flash_rows(s, v1[lo:hi, :], rows, m_sc, acc_sc)

    @pl.when(qi >= nrel - 1)
    def _():
        for rr in range(nrel):
            block(rr)

    @pl.when(qi < nrel - 1)
    def _():
        block(0)
        for rr in range(1, nrel - 1):
            @pl.when(qi >= rr)
            def _(rr=rr):
                block(rr)

    _flash_finish(acc_sc, o_ref)


def _dilated_attention(qkv, *, t=1024, sb=256):
    S = qkv.shape[0]
    t = min(t, S)
    sb = min(sb, t)
    max_window = max(w for w, _ in SWA_PATTERNS)
    nrel = min(-(-max_window // t) + 1, S // t)
    H = SWA_HEADS
    bias = _dilated_bias(t, nrel)
    live = []
    for r in range(nrel):
        per_rb = []
        for rb in range(t // sb):
            cols = np.nonzero(np.isfinite(bias[r, rb * sb:(rb + 1) * sb]).any(axis=0))[0]
            lo = int(cols.min()) // LANES * LANES
            hi = -(-(int(cols.max()) + 1) // LANES) * LANES
            per_rb.append((lo, hi))
        live.append(per_rb)
    kv_specs = []
    for r in range(nrel):
        kv_specs.append(pl.BlockSpec((t, SWA_DIM), lambda h, i, r=r: (jnp.maximum(i - r, 0), H + h)))
        kv_specs.append(pl.BlockSpec((t, SWA_DIM), lambda h, i, r=r: (jnp.maximum(i - r, 0), 2 * H + h)))
    stat = pltpu.VMEM((t, LANES), F32)
    return pl.pallas_call(
        functools.partial(_dilated_kernel, sb=sb, nrel=nrel, live=live),
        out_shape=jax.ShapeDtypeStruct((S, H * SWA_DIM), BF16),
        grid=(H, S // t),
        in_specs=[pl.BlockSpec((t, SWA_DIM), lambda h, i: (i, h))] + kv_specs
                 + [pl.BlockSpec((nrel, t, t), lambda h, i: (0, 0, 0))],
        out_specs=pl.BlockSpec((t, SWA_DIM), lambda h, i: (i, h)),
        scratch_shapes=[stat, pltpu.VMEM((t, 2 * SWA_DIM), F32)],
        compiler_params=_cparams(("parallel", "parallel")),
        name="dilated_attention",
    )(qkv, *([qkv] * (2 * nrel)), jnp.asarray(bias))


def _gla_kernel(q_ref, k_ref, v_ref, glr_ref, wgu_ref, bg_ref, rd_ref, gn_ref, tri_ref, o_ref,
                st_sc, sall_sc, oi_sc, *, G):
    C = GLA_CHUNK
    SB = GLA_SUB
    DK = GLA_DK
    DV = GLA_DV
    c_idx = pl.program_id(1)

    @pl.when(c_idx == 0)
    def _():
        st_sc[...] = jnp.zeros(st_sc.shape, F32)

    z = _dot(glr_ref[...], wgu_ref[...]) + bg_ref[...]
    lg = -_softplus(-z) * (1.0 / GLA_NORMALIZER)
    bc = _bsplit_cumsum(tri_ref[...], lg.reshape(G, C, DK))
    q = (q_ref[...] * (DK ** -0.5)).reshape(G, C, DK)
    k = k_ref[...].reshape(G, C, DK)
    v = v_ref[...].reshape(G, C, DV)
    blast = bc[:, C - 1:C, :]
    khat = k * jnp.exp(blast - bc)
    gkv = _bmm(jnp.swapaxes(v, 1, 2), khat)
    eb = jnp.exp(blast)

    st = st_sc[...]
    for c in range(G):
        sall_sc[c] = st
        st = st * eb[c] + gkv[c]
    st_sc[...] = st

    ti = lax.broadcasted_iota(jnp.int32, (1, SB, SB, 1), 1)
    si = lax.broadcasted_iota(jnp.int32, (1, SB, SB, 1), 2)
    causal = si <= ti
    for sb in range(C // SB):
        lo = sb * SB
        qs = q[:, lo:lo + SB]
        ks = k[:, lo:lo + SB]
        bs = bc[:, lo:lo + SB]
        rel = bs[:, :, None, :] - bs[:, None, :, :]
        dec = jnp.exp(jnp.where(causal, rel, -jnp.inf))
        att = jnp.sum(qs[:, :, None, :] * ks[:, None, :, :] * dec, axis=-1)
        o_sb = _bmm(att, v[:, lo:lo + SB])
        if sb > 0:
            bm = bc[:, lo - 1:lo]
            qsc = qs * jnp.exp(bs - bm)
            ksc = k[:, :lo] * jnp.exp(bm - bc[:, :lo])
            o_sb = o_sb + _bmm(_bmm_nt(qsc, ksc), v[:, :lo])
        oi_sc[:, lo:lo + SB, :] = o_sb

    o = (oi_sc[...] + _bmm_nt(q * jnp.exp(bc), sall_sc[...])).reshape(G * C, DV)
    rd = rd_ref[...]
    o_ref[...] = (_rms(o, gn_ref[...]) * (rd * _sigmoid(rd))).astype(o_ref.dtype)


def _gla(pb, wgu, bg, gn, *, G=32):
    S = pb.shape[0]
    C = GLA_CHUNK
    G = min(G, S // C)
    TB = G * C
    H, DK, DV = GLA_HEADS, GLA_DK, GLA_DV
    t = np.arange(C)
    tri = jnp.asarray((t[None, :] <= t[:, None]).astype(np.float32), BF16)
    glr_block = (2 * H * DK + 2 * H * DV) // LANES
    return pl.pallas_call(
        functools.partial(_gla_kernel, G=G),
        out_shape=jax.ShapeDtypeStruct((S, H * DV), BF16),
        grid=(H, S // TB),
        in_specs=[pl.BlockSpec((TB, DK), lambda h, c: (c, h)),
                  pl.BlockSpec((TB, DK), lambda h, c: (c, H + h)),
                  pl.BlockSpec((TB, DV), lambda h, c: (c, (2 * H * DK) // DV + h)),
                  pl.BlockSpec((TB, LANES), lambda h, c: (c, glr_block)),
                  pl.BlockSpec((LANES, DK), lambda h, c: (0, h)),
                  pl.BlockSpec((1, DK), lambda h, c: (0, h)),
                  pl.BlockSpec((TB, DV), lambda h, c: (c, (2 * H * DK + H * DV) // DV + h)),
                  pl.BlockSpec((1, DV), lambda h, c: (0, h)),
                  pl.BlockSpec((C, C), lambda h, c: (0, 0))],
        out_specs=pl.BlockSpec((TB, DV), lambda h, c: (c, h)),
        scratch_shapes=[pltpu.VMEM((DV, DK), F32), pltpu.VMEM((G, DV, DK), F32), pltpu.VMEM((G, C, DV), F32)],
        compiler_params=_cparams(("parallel", "arbitrary")),
        name="gla",
    )(pb, pb, pb, pb, wgu, bg, pb, gn, tri)


def _odd_mixer(hn, tabs, w_in, q_head_norm, k_head_norm, w_gate_up, b_gate, gla_norm, w_out):
    cos2, sin2 = tabs[2], tabs[3]
    nq = SWA_HEADS * SWA_DIM
    hg = jnp.concatenate([jnp.tile(q_head_norm * (SWA_DIM ** -0.5 * LOG2E), SWA_HEADS), jnp.tile(k_head_norm, SWA_HEADS)])[None, :]
    w_in = w_in.astype(BF16)
    qkv = _swa_qkv(hn, w_in, hg, cos2, sin2)
    o_c = _dilated_attention(qkv)
    o = 3 * nq
    dk, dv = GLA_HEADS * GLA_DK, GLA_HEADS * GLA_DV
    wb = jnp.concatenate([w_in[:, o:o + 2 * dk + dv], w_in[:, o + 2 * dk + dv + GLA_LORA:],
                          _pad_cols(w_in[:, o + 2 * dk + dv:o + 2 * dk + dv + GLA_LORA], LANES)], axis=1)
    pb = _proj(hn, wb, tm=512, tn=wb.shape[1])
    wgu = jnp.pad(w_gate_up, ((0, LANES - GLA_LORA), (0, 0))).astype(BF16)
    o_d = _gla(pb, wgu, b_gate[None, :], gla_norm.reshape(1, dv))
    return o_c, o_d, w_out.astype(BF16)


def kernel(x, positions, ffn_norm, ffn_w_gate, ffn_w_up, ffn_w_down, mix_norm, mla_rwkv_w_in, mla_q_norm, mla_w_uq,
           mla_kv_norm, mla_w_ukv, mla_q_head_norm, mla_k_head_norm, rwkv_mu, rwkv_w0, rwkv_w2, rwkv_a0, rwkv_a2,
           rwkv_g2, rwkv_k_k, rwkv_k_a, rwkv_r_k, rwkv_ln_w, rwkv_ln_b, mla_rwkv_w_out, swa_gla_w_in,
           swa_q_head_norm, swa_k_head_norm, gla_w_gate_up, gla_b_gate, gla_norm, swa_gla_w_out):
    B, S, D = x.shape
    assert B == 1
    h = x.reshape(S, D)
    tabs = _rope_tables(positions.reshape(S))
    wg = ffn_w_gate.astype(BF16)
    wu = ffn_w_up.astype(BF16)
    wd = ffn_w_down.astype(BF16)
    depth = ffn_norm.shape[0]
    for layer in range(depth):
        i = layer // 2
        h, hn = _ffn(h, ffn_norm, wg, wu, wd, layer, 0, next_norm=mix_norm[layer])
        if layer % 2 == 0:
            mix = _even_mixer(hn, tabs, mla_rwkv_w_in[i], mla_q_norm[i], mla_w_uq[i], mla_kv_norm[i],
                            mla_w_ukv[i], mla_q_head_norm[i], mla_k_head_norm[i], rwkv_mu[i], rwkv_w0[i],
                            rwkv_w2[i], rwkv_a0[i], rwkv_a2[i], rwkv_g2[i], rwkv_k_k[i], rwkv_k_a[i], rwkv_r_k[i],
                            rwkv_ln_w[i], rwkv_ln_b[i], mla_rwkv_w_out[i])
        else:
            mix = _odd_mixer(hn, tabs, swa_gla_w_in[i], swa_q_head_norm[i], swa_k_head_norm[i],
                           gla_w_gate_up[i], gla_b_gate[i], gla_norm[i], swa_gla_w_out[i])
        h = _ffn(h, ffn_norm, wg, wu, wd, layer, 1, mix=mix)
    return h.reshape(B, S, D)
```

```python
import functools

import numpy as np
import jax
import jax.numpy as jnp
from jax import lax
from jax.experimental import pallas as pl
from jax.experimental.pallas import tpu as pltpu

F32 = jnp.float32
BF16 = jnp.bfloat16

LANES = 128
VMEM_LIMIT = 56 * 1024 * 1024

D_MODEL = 2048
D_FF = 5632
MACARON_WEIGHT = 0.5
NORM_EPS = 1e-6
ROPE_THETA = 10000.0
LOG2E = float(np.log2(np.e))
EXP_NEG_HALF = float(np.exp(-0.5))

MLA_HEADS = 8
MLA_NOPE = 128
MLA_ROPE = 64
MLA_QK = MLA_NOPE + MLA_ROPE
MLA_QK_PAD = 256
MLA_V = 128
MLA_Q_LORA = 512
MLA_KV_LORA = 256

RWKV_HEAD = 64
RWKV_WIDTH = 1024
RWKV_HEADS = 16
RWKV_W_LORA = 64
RWKV_A_LORA = 64
RWKV_G_LORA = 160
RWKV_GN_EPS = 64e-5
RWKV_SEC = 3584
RWKV_CHUNK = 64

SWA_HEADS = 8
SWA_DIM = 128
SWA_PATTERNS = ((128, 1), (512, 4), (2048, 16))

GLA_HEADS = 4
GLA_DK = 128
GLA_DV = 256
GLA_LORA = 16
GLA_NORMALIZER = 16.0
GLA_CHUNK = 64
GLA_SUB = 8


def _cparams(sem):
    return pltpu.CompilerParams(dimension_semantics=sem, vmem_limit_bytes=VMEM_LIMIT)


def _rms(x, g):
    return x * lax.rsqrt(jnp.mean(x * x, axis=-1, keepdims=True) + NORM_EPS) * g


def _dot(a, b):
    return jnp.dot(a.astype(BF16), b.astype(BF16), preferred_element_type=F32)


def _dot_nt(a, b):
    return lax.dot_general(a.astype(BF16), b.astype(BF16), (((1,), (1,)), ((), ())),
                           preferred_element_type=F32)


def _sigmoid(x):
    return 0.5 * jnp.tanh(0.5 * x) + 0.5


def _softplus(x):
    return jnp.maximum(x, 0.0) + jnp.log(1.0 + jnp.exp(-jnp.abs(x)))


def _rope_apply(x, cos, sin_signed):
    return x * cos + pltpu.roll(x, 64, 1) * sin_signed


def _ffn_kernel(*refs, emit_norm, mix):
    it = iter(refs)
    h_ref, g_ref, wg_ref, wu_ref, wd_ref = (next(it) for _ in range(5))
    if mix:
        a_ref, b_ref, wa_ref, wb_ref = (next(it) for _ in range(4))
    if emit_norm:
        g2_ref = next(it)
    o_ref = next(it)
    if emit_norm:
        hn_ref = next(it)
    xn_ref = next(it)
    j = pl.program_id(1)

    @pl.when(j == 0)
    def _():
        x = h_ref[...]
        if mix:
            x = (x + jnp.dot(a_ref[...], wa_ref[...], preferred_element_type=F32)
                 + jnp.dot(b_ref[...], wb_ref[...], preferred_element_type=F32))
        xn_ref[...] = _rms(x, g_ref[...]).astype(BF16)
        o_ref[...] = x

    xn = xn_ref[...]
    gate = jnp.dot(xn, wg_ref[...], preferred_element_type=F32)
    up = jnp.dot(xn, wu_ref[...], preferred_element_type=F32)
    act = (gate * _sigmoid(gate) * up * MACARON_WEIGHT).astype(BF16)
    o_ref[...] += jnp.dot(act, wd_ref[...], preferred_element_type=F32)

    if emit_norm:
        @pl.when(j == pl.num_programs(1) - 1)
        def _():
            hn_ref[...] = _rms(o_ref[...], g2_ref[...]).astype(hn_ref.dtype)


def _ffn(h, g, wg, wu, wd, layer, k, next_norm=None, mix=None, *, tm=512, tf=512):
    S, D = h.shape
    F = wg.shape[-1]
    tm = min(tm, S)
    emit_norm = next_norm is not None
    wspec = pl.BlockSpec((None, None, D, tf), lambda i, j: (layer, k, 0, j))
    row = pl.BlockSpec((1, D), lambda i, j: (0, 0))
    tile = pl.BlockSpec((tm, D), lambda i, j: (i, 0))
    in_specs = [tile, row, wspec, wspec, pl.BlockSpec((None, None, tf, D), lambda i, j: (layer, k, j, 0))]
    args = [h, g[layer, k][None, :], wg, wu, wd]
    if mix is not None:
        a, b, w_out = mix
        ka = a.shape[1]
        assert b.shape[1] == ka and w_out.shape == (2 * ka, D)
        half = pl.BlockSpec((tm, ka), lambda i, j: (i, 0))
        in_specs += [half, half, pl.BlockSpec((ka, D), lambda i, j: (0, 0)), pl.BlockSpec((ka, D), lambda i, j: (1, 0))]
        args += [a, b, w_out, w_out]
    out_shape = jax.ShapeDtypeStruct((S, D), F32)
    out_specs = tile
    if emit_norm:
        in_specs.append(row)
        args.append(next_norm[None, :])
        out_shape = (out_shape, jax.ShapeDtypeStruct((S, D), BF16))
        out_specs = (tile, tile)
    return pl.pallas_call(
        functools.partial(_ffn_kernel, emit_norm=emit_norm, mix=mix is not None),
        out_shape=out_shape,
        grid=(S // tm, F // tf),
        in_specs=in_specs,
        out_specs=out_specs,
        scratch_shapes=[pltpu.VMEM((tm, D), BF16)],
        compiler_params=_cparams(("parallel", "arbitrary")),
        name="ffn",
    )(*args)


def _rope_tab_kernel(pos_ref, f1_ref, s1_ref, f2_ref, s2_ref, c1_ref, n1_ref, c2_ref, n2_ref):
    pos = pos_ref[...].astype(F32)
    a1 = pos * f1_ref[...]
    c1_ref[...] = jnp.cos(a1)
    n1_ref[...] = jnp.sin(a1) * s1_ref[...]
    a2 = pos * f2_ref[...]
    c2_ref[...] = jnp.cos(a2)
    n2_ref[...] = jnp.sin(a2) * s2_ref[...]


def _rope_tables(positions, *, tm=1024):
    S = positions.shape[0]
    tm = min(tm, S)
    f32half = ROPE_THETA ** (-jnp.arange(MLA_ROPE // 2, dtype=F32) / (MLA_ROPE // 2))
    z = jnp.zeros((32,), F32)
    f1 = jnp.concatenate([f32half, z, f32half, z])[None]
    f64half = ROPE_THETA ** (-jnp.arange(SWA_DIM // 2, dtype=F32) / (SWA_DIM // 2))
    f2 = jnp.concatenate([f64half, f64half])[None]
    sign = jnp.concatenate([-jnp.ones((64,), F32), jnp.ones((64,), F32)])[None]
    row = pl.BlockSpec((1, LANES), lambda i: (0, 0))
    tab = pl.BlockSpec((tm, LANES), lambda i: (i, 0))
    shp = jax.ShapeDtypeStruct((S, LANES), F32)
    return pl.pallas_call(
        _rope_tab_kernel,
        out_shape=(shp, shp, shp, shp),
        grid=(S // tm,),
        in_specs=[pl.BlockSpec((tm, 1), lambda i: (i, 0)), row, row, row, row],
        out_specs=(tab, tab, tab, tab),
        compiler_params=_cparams(("parallel",)),
        name="rope_tables",
    )(positions.reshape(S, 1), f1, sign, f2, sign)


def _proj_kernel(x_ref, w_ref, o_ref):
    o_ref[...] = jnp.dot(x_ref[...], w_ref[...], preferred_element_type=F32).astype(o_ref.dtype)


def _proj(x, w, *, tm=1024, tn, out_dtype=F32):
    S, K = x.shape
    N = w.shape[1]
    tm = min(tm, S)
    return pl.pallas_call(
        _proj_kernel,
        out_shape=jax.ShapeDtypeStruct((S, N), out_dtype),
        grid=(N // tn, S // tm),
        in_specs=[pl.BlockSpec((tm, K), lambda n, i: (i, 0)),
                  pl.BlockSpec((K, tn), lambda n, i: (0, n))],
        out_specs=pl.BlockSpec((tm, tn), lambda n, i: (i, n)),
        compiler_params=_cparams(("parallel", "parallel")),
        name="proj",
    )(x, w)


def _mla_qkv_kernel(cq_ref, ckv_ref, kr_ref, qn_ref, kvn_ref, wq_ref, wkv_ref, hgq_ref, gn_ref, gr_ref,
                    cos_ref, sin_ref, q_ref, k_ref, v_ref):
    cos = cos_ref[...]
    sin = sin_ref[...]
    xq = _rms(cq_ref[...], qn_ref[...]).astype(BF16)
    xkv = _rms(ckv_ref[...], kvn_ref[...]).astype(BF16)
    kr = kr_ref[...]
    kr_ss = jnp.sum(kr * kr, axis=-1, keepdims=True)
    inv_d = 1.0 / MLA_QK
    for h in range(MLA_HEADS):
        lo = h * MLA_QK_PAD
        a = jnp.dot(xq, wq_ref[:, lo:lo + MLA_QK_PAD], preferred_element_type=F32)
        y = a * lax.rsqrt(jnp.sum(a * a, axis=-1, keepdims=True) * inv_d + NORM_EPS) * hgq_ref[...]
        q_ref[:, lo:lo + LANES] = y[:, :LANES].astype(q_ref.dtype)
        q_ref[:, lo + LANES:lo + MLA_QK_PAD] = _rope_apply(y[:, LANES:], cos, sin).astype(q_ref.dtype)
        kv = jnp.dot(xkv, wkv_ref[:, lo:lo + MLA_QK_PAD], preferred_element_type=F32)
        kn = kv[:, :LANES]
        rs = lax.rsqrt((jnp.sum(kn * kn, axis=-1, keepdims=True) + kr_ss) * inv_d + NORM_EPS)
        k_ref[:, lo:lo + LANES] = (kn * rs * gn_ref[...]).astype(k_ref.dtype)
        k_ref[:, lo + LANES:lo + MLA_QK_PAD] = _rope_apply(kr * rs * gr_ref[...], cos, sin).astype(k_ref.dtype)
        v_ref[:, h * MLA_V:(h + 1) * MLA_V] = kv[:, LANES:].astype(v_ref.dtype)


def _mla_qkv(p, qn, kvn, wq, wkv, hgq, gn, gr, cos, sin, *, tm=1024):
    S = p.shape[0]
    tm = min(tm, S)
    cq_block = RWKV_SEC // MLA_Q_LORA
    ckv_block = (RWKV_SEC + MLA_Q_LORA) // MLA_KV_LORA
    kr_block = (RWKV_SEC + MLA_Q_LORA + MLA_KV_LORA) // LANES
    NQ = MLA_HEADS * MLA_QK_PAD
    tab = pl.BlockSpec((tm, LANES), lambda i: (i, 0))
    full = lambda a: pl.BlockSpec(a.shape, lambda i: (0, 0))
    return pl.pallas_call(
        _mla_qkv_kernel,
        out_shape=(jax.ShapeDtypeStruct((S, NQ), BF16), jax.ShapeDtypeStruct((S, NQ), BF16),
                   jax.ShapeDtypeStruct((S, MLA_HEADS * MLA_V), BF16)),
        grid=(S // tm,),
        in_specs=[pl.BlockSpec((tm, MLA_Q_LORA), lambda i: (i, cq_block)),
                  pl.BlockSpec((tm, MLA_KV_LORA), lambda i: (i, ckv_block)),
                  pl.BlockSpec((tm, LANES), lambda i: (i, kr_block)),
                  full(qn), full(kvn), full(wq), full(wkv), full(hgq), full(gn), full(gr), tab, tab],
        out_specs=(pl.BlockSpec((tm, NQ), lambda i: (i, 0)), pl.BlockSpec((tm, NQ), lambda i: (i, 0)),
                   pl.BlockSpec((tm, MLA_HEADS * MLA_V), lambda i: (i, 0))),
        compiler_params=_cparams(("parallel",)),
        name="mla_qkv",
    )(p, p, p, qn, kvn, wq, wkv, hgq, gn, gr, cos, sin)


def _flash_rows(s, v1, rows, m_sc, acc_sc):
    m_prev = m_sc[rows, :]
    m_new = jnp.maximum(m_prev, jnp.max(s, axis=-1, keepdims=True))
    alpha = jnp.exp2(m_prev - m_new)
    pr = jnp.exp2((s - jnp.concatenate([m_new] * (s.shape[1] // LANES), axis=1)).astype(BF16))
    acc_sc[rows, :] = (jnp.concatenate([alpha, alpha], axis=1) * acc_sc[rows, :]
                       + jnp.dot(pr, v1, preferred_element_type=F32))
    m_sc[rows, :] = m_new


def _flash_init(m_sc, acc_sc):
    m_sc[...] = jnp.full(m_sc.shape, -jnp.inf, F32)
    acc_sc[...] = jnp.zeros(acc_sc.shape, F32)


def _with_ones(v):
    return jnp.concatenate([v, jnp.ones(v.shape, v.dtype)], axis=1)


def _flash_finish(acc_sc, o_ref):
    acc = acc_sc[...]
    d = acc.shape[1] // 2
    o_ref[...] = (acc[:, :d] / acc[:, d:]).astype(o_ref.dtype)


def _mla_attn_kernel(qi_ref, ki_ref, q_ref, k_ref, v_ref, o_ref, m_sc, acc_sc, *, sb):
    p = pl.program_id(1)
    qi = qi_ref[p]
    ki = ki_ref[p]
    t = q_ref.shape[0]

    @pl.when(ki == 0)
    def _():
        _flash_init(m_sc, acc_sc)

    @pl.when(ki < qi)
    def _():
        v1 = _with_ones(v_ref[...])
        for rb in range(t // sb):
            rows = slice(rb * sb, (rb + 1) * sb)
            s = _dot_nt(q_ref[rows, :], k_ref[...])
            _flash_rows(s, v1, rows, m_sc, acc_sc)

    @pl.when(ki == qi)
    def _():
        v1 = _with_ones(v_ref[...])
        for rb in range(t // sb):
            rows = slice(rb * sb, (rb + 1) * sb)
            nc = (rb + 1) * sb
            s = _dot_nt(q_ref[rows, :], k_ref[0:nc, :])
            row = lax.broadcasted_iota(jnp.int32, s.shape, 0) + rb * sb
            col = lax.broadcasted_iota(jnp.int32, s.shape, 1)
            s = jnp.where(col <= row, s, -jnp.inf)
            _flash_rows(s, v1[0:nc, :], rows, m_sc, acc_sc)
        _flash_finish(acc_sc, o_ref)


def _mla_attention(q, k, v, *, t=2048, sb=256):
    S = q.shape[0]
    t = min(t, S)
    sb = min(sb, t)
    nb = S // t
    pairs = [(a, b) for a in range(nb) for b in range(a + 1)]
    qi_tab = jnp.asarray(np.array([a for a, _ in pairs], np.int32))
    ki_tab = jnp.asarray(np.array([b for _, b in pairs], np.int32))
    stat = pltpu.VMEM((t, LANES), F32)
    grid_spec = pltpu.PrefetchScalarGridSpec(
        num_scalar_prefetch=2,
        grid=(MLA_HEADS, len(pairs)),
        in_specs=[pl.BlockSpec((t, MLA_QK_PAD), lambda h, p, qt, kt: (qt[p], h)),
                  pl.BlockSpec((t, MLA_QK_PAD), lambda h, p, qt, kt: (kt[p], h)),
                  pl.BlockSpec((t, MLA_V), lambda h, p, qt, kt: (kt[p], h))],
        out_specs=pl.BlockSpec((t, MLA_V), lambda h, p, qt, kt: (qt[p], h)),
        scratch_shapes=[stat, pltpu.VMEM((t, 2 * MLA_V), F32)],
    )
    return pl.pallas_call(
        functools.partial(_mla_attn_kernel, sb=sb),
        out_shape=jax.ShapeDtypeStruct((S, MLA_HEADS * MLA_V), BF16),
        grid_spec=grid_spec,
        compiler_params=_cparams(("parallel", "arbitrary")),
        name="mla_attention",
    )(qi_tab, ki_tab, q, k, v)


def _bmm(a, b):
    return jnp.einsum('gik,gkj->gij', a.astype(BF16), b.astype(BF16), preferred_element_type=F32)


def _bmm_nt(a, b):
    return jnp.einsum('gik,gjk->gij', a.astype(BF16), b.astype(BF16), preferred_element_type=F32)


def _bsplit_cumsum(tri, x):
    G = x.shape[0]
    tb = jnp.broadcast_to(tri.astype(BF16)[None], (G,) + tri.shape)
    x0 = x.astype(BF16)
    r1 = x - x0.astype(F32)
    x1 = r1.astype(BF16)
    x2 = (r1 - x1.astype(F32)).astype(BF16)
    f = lambda t: jnp.einsum('gts,gsl->gtl', tb, t, preferred_element_type=F32)
    return f(x0) + f(x1) + f(x2)


def _rwkv_kernel(p_ref, q_ref, mu_ref, w0_ref, w2_ref, a0_ref, a2_ref, g2_ref, kk_ref, ka_ref, lnw_ref, lnb_ref, rk_ref,
                 tri_ref, e_ref, o_ref, st_sc, *, hg, cps):
    C = RWKV_CHUNK
    W = RWKV_WIDTH
    LW = hg * RWKV_HEAD
    RW = hg * C
    NP = W // LW
    NB = cps * NP
    TB = cps * C
    c_idx = pl.program_id(0)

    @pl.when(c_idx == 0)
    def _():
        st_sc[...] = jnp.zeros(st_sc.shape, F32)

    x = p_ref[...]
    row = lax.broadcasted_iota(jnp.int32, (TB, 1), 0)
    last = jnp.where(c_idx == 0, 0.0, q_ref[7:8, :])
    prev = jnp.where(row == 0, last, pltpu.roll(x, 1, 0))
    xs = x + (prev - x) * mu_ref[...]
    r = xs[:, :W]
    kb = xs[:, W:2 * W]
    v = xs[:, 2 * W:3 * W]
    xw = xs[:, 3 * W:3 * W + LANES]
    xa = xs[:, 3 * W + LANES:3 * W + 2 * LANES]
    xg = xs[:, 3 * W + 2 * LANES:]
    lw = -EXP_NEG_HALF * _sigmoid(w0_ref[...] + _dot(jnp.tanh(xw), w2_ref[...]))
    a = _sigmoid(a0_ref[...] + _dot(xa, a2_ref[...]))
    g = _dot(_sigmoid(xg), g2_ref[...])
    k = kb * (1.0 + (a - 1.0) * ka_ref[...])

    def to_b(t):
        t3 = t.reshape(cps, C, W)
        return jnp.concatenate([t3[c:c + 1, :, LW * p:LW * (p + 1)] for c in range(cps) for p in range(NP)], axis=0)

    def from_b(t):
        rows = [jnp.concatenate([t[c * NP + p] for p in range(NP)], axis=-1) for c in range(cps)]
        return jnp.concatenate(rows, axis=0)

    e_b = jnp.broadcast_to(e_ref[...][None], (NB, LW, LW))

    def seg_sum(t):
        t0 = t.astype(BF16)
        t1 = (t - t0.astype(F32)).astype(BF16)
        return _bmm(t0, e_b) + _bmm(t1, e_b)

    lw3, r3, k3, v3, a3 = to_b(lw), to_b(r), to_b(k), to_b(v), to_b(a)
    kkb = to_b(kb * kk_ref[...])
    kk3 = kkb * lax.rsqrt(jnp.maximum(seg_sum(kkb * kkb), 1e-24))
    b3 = kk3 * a3

    cum3 = _bsplit_cumsum(tri_ref[...], lw3)
    clast = cum3[:, C - 1:C, :]
    einv = jnp.exp(-cum3)
    pc = jnp.exp(clast)
    etail = pc * einv
    rt = r3 * jnp.exp(cum3)
    at = -kk3 * jnp.exp(cum3 - lw3)
    bt = b3 * einv
    kt = k3 * einv
    bh = b3 * etail
    kh = k3 * etail

    lane_head = lax.broadcasted_iota(jnp.int32, (1, 1, LW), 2) // RWKV_HEAD

    def stack(t):
        return jnp.concatenate([jnp.where(lane_head == h, t, 0.0) for h in range(hg)], axis=1)

    a2, b2, k2, r2, v2, bh2, kh2 = (stack(t).astype(BF16) for t in (at, bt, kt, rt, v3, bh, kh))
    ri = lax.broadcasted_iota(jnp.int32, (1, RW, RW), 1)
    ci = lax.broadcasted_iota(jnp.int32, (1, RW, RW), 2)
    same = (ri // C) == (ci // C)
    strict = same & (ci < ri)
    incl = same & (ci <= ri)
    eye = (ri == ci).astype(F32)

    bk2 = jnp.concatenate([b2, k2], axis=1)
    a_bk = _bmm_nt(a2, bk2)
    r_bk = _bmm_nt(r2, bk2)
    lmat = jnp.where(strict, a_bk[:, :, :RW], 0.0)
    akm = jnp.where(strict, a_bk[:, :, RW:], 0.0)
    rbm = jnp.where(incl, r_bk[:, :, :RW], 0.0)
    rkm = jnp.where(incl, r_bk[:, :, RW:], 0.0)
    tinv = eye + lmat
    pw = lmat
    for _ in range(int(np.log2(C)) - 1):
        pw = _bmm(pw, pw)
        tinv = tinv + _bmm(tinv, pw)
    wm = _bmm(tinv, a2)
    z = _bmm(tinv, _bmm(akm, v2))
    qm = r2.astype(F32) + _bmm(rbm, wm)
    y0 = _bmm(rkm, v2) + _bmm(rbm, z)
    kc = _bmm(jnp.swapaxes(wm, 1, 2), bh2)
    nc = _bmm(jnp.swapaxes(z, 1, 2), bh2) + _bmm(jnp.swapaxes(v2, 1, 2), kh2)

    st = st_sc[...]
    y2 = []
    for c in range(cps):
        sl = slice(c * NP, (c + 1) * NP)
        y2.append(_bmm_nt(qm[sl], st) + y0[sl])
        st = st * pc[sl] + _bmm(st, kc[sl]) + nc[sl]
    st_sc[...] = st
    y2 = jnp.concatenate(y2, axis=0)
    y = y2[:, 0:C]
    for h in range(1, hg):
        y = y + y2[:, h * C:(h + 1) * C]

    inv_n = 1.0 / RWKV_HEAD
    d = y - seg_sum(y) * inv_n
    yhat = d * lax.rsqrt(seg_sum(d * d) * inv_n + RWKV_GN_EPS)
    bonus = seg_sum(to_b(r * k * rk_ref[...])) * v3
    out = (from_b(yhat) * lnw_ref[...] + lnb_ref[...] + from_b(bonus)) * g
    o_ref[...] = out.astype(o_ref.dtype)


def _rwkv_mix(p, mu, w0, w2, a0, a2, g2, k_k, k_a, ln_w, ln_b, r_k, *, hg=2, cps=2):
    S = p.shape[0]
    C = RWKV_CHUNK
    cps = min(cps, S // C)
    TB = cps * C
    LW = hg * RWKV_HEAD
    W = RWKV_WIDTH
    t = np.arange(C)
    tri = jnp.asarray((t[None, :] <= t[:, None]).astype(np.float32), BF16)
    l = np.arange(LW)
    e = jnp.asarray((l[:, None] // RWKV_HEAD == l[None, :] // RWKV_HEAD).astype(np.float32), BF16)
    full = lambda arr: pl.BlockSpec(arr.shape, lambda c: (0, 0))
    return pl.pallas_call(
        functools.partial(_rwkv_kernel, hg=hg, cps=cps),
        out_shape=jax.ShapeDtypeStruct((S, W), BF16),
        grid=(S // TB,),
        in_specs=[pl.BlockSpec((TB, RWKV_SEC), lambda c: (c, 0)),
                  pl.BlockSpec((8, RWKV_SEC), lambda c: (jnp.maximum(c * (TB // 8) - 1, 0), 0)),
                  full(mu), full(w0), full(w2), full(a0), full(a2), full(g2), full(k_k), full(k_a),
                  full(ln_w), full(ln_b), full(r_k), full(tri), full(e)],
        out_specs=pl.BlockSpec((TB, W), lambda c: (c, 0)),
        scratch_shapes=[pltpu.VMEM((W // LW, LW, LW), F32)],
        compiler_params=_cparams(("arbitrary",)),
        name="rwkv_mix",
    )(p, p, mu, w0, w2, a0, a2, g2, k_k, k_a, ln_w, ln_b, r_k, tri, e)


def _pad_cols(w, width):
    return jnp.pad(w, ((0, 0), (0, width - w.shape[1])))


def _rope_gap_layout(w):
    z = jnp.zeros(w.shape[:-1] + (32,), w.dtype)
    return jnp.concatenate([w[..., :32], z, w[..., 32:], z], axis=-1)


def _even_mixer(hn, tabs, w_in, q_norm, w_uq, kv_norm, w_ukv, q_head_norm, k_head_norm,
                mu, w0, w2, a0, a2, g2, k_k, k_a, r_k, ln_w, ln_b, w_out):
    cos1, sin1 = tabs[0], tabs[1]
    W = RWKV_WIDTH
    o_cq = 0
    o_ckv = MLA_Q_LORA
    o_kr = o_ckv + MLA_KV_LORA
    o_rw = o_kr + MLA_ROPE
    w_in = w_in.astype(BF16)
    rw = w_in[:, o_rw:]

    def rwkv_layout(t):
        return jnp.concatenate([t[:, :3 * W], _pad_cols(t[:, 3 * W:3 * W + RWKV_W_LORA], LANES),
                                _pad_cols(t[:, 3 * W + RWKV_W_LORA:3 * W + RWKV_W_LORA + RWKV_A_LORA], LANES),
                                _pad_cols(t[:, 3 * W + RWKV_W_LORA + RWKV_A_LORA:], 2 * LANES)], axis=1)

    w_in_p = jnp.concatenate([rwkv_layout(rw), w_in[:, o_cq:o_ckv], w_in[:, o_ckv:o_kr],
                              _rope_gap_layout(w_in[:, o_kr:o_rw]), jnp.zeros((w_in.shape[0], LANES), BF16)], axis=1)
    p = _proj(hn, w_in_p, tn=1536)

    wq = w_uq.astype(BF16).reshape(MLA_Q_LORA, MLA_HEADS, MLA_QK)
    wq = jnp.concatenate([wq[..., :MLA_NOPE], _rope_gap_layout(wq[..., MLA_NOPE:])], axis=-1)
    wq = wq.reshape(MLA_Q_LORA, MLA_HEADS * MLA_QK_PAD)
    scale = MLA_QK ** -0.5 * LOG2E
    hg_q = jnp.concatenate([q_head_norm[:MLA_NOPE], _rope_gap_layout(q_head_norm[MLA_NOPE:])])[None, :] * scale
    gn = k_head_norm[None, :MLA_NOPE]
    gr = _rope_gap_layout(k_head_norm[MLA_NOPE:])[None, :]
    q, kmat, vmat = _mla_qkv(p, q_norm[None, :], kv_norm[None, :], wq, w_ukv.astype(BF16), hg_q, gn, gr, cos1, sin1)
    o_a = _mla_attention(q, kmat, vmat)

    w2p = jnp.pad(w2, ((0, LANES - RWKV_W_LORA), (0, 0))).astype(BF16)
    a2p = jnp.pad(a2, ((0, LANES - RWKV_A_LORA), (0, 0))).astype(BF16)
    g2p = jnp.pad(g2, ((0, 2 * LANES - RWKV_G_LORA), (0, 0))).astype(BF16)
    o_b = _rwkv_mix(p, rwkv_layout(mu[None, :]), w0[None, :], w2p, a0[None, :], a2p, g2p, k_k[None, :], k_a[None, :],
                    ln_w[None, :], ln_b[None, :], r_k.reshape(1, W))
    return o_a, o_b, w_out.astype(BF16)


def _swa_qkv_kernel(x_ref, w_ref, hg_ref, cos_ref, sin_ref, o_ref):
    n = pl.program_id(0)
    half = x_ref.shape[0] // 2

    @pl.when(n < 2)
    def _():
        for r0 in (0, half):
            rows = slice(r0, r0 + half)
            acc = jnp.dot(x_ref[rows, :], w_ref[...], preferred_element_type=F32)
            cos = cos_ref[rows, :]
            sin = sin_ref[rows, :]
            for c in range(acc.shape[1] // LANES):
                sl = slice(c * LANES, (c + 1) * LANES)
                o_ref[rows, sl] = _rope_apply(_rms(acc[:, sl], hg_ref[:, sl]), cos, sin).astype(o_ref.dtype)

    @pl.when(n == 2)
    def _():
        o_ref[...] = jnp.dot(x_ref[...], w_ref[...], preferred_element_type=F32).astype(o_ref.dtype)


def _swa_qkv(x, w, hg, cos, sin, *, tm=1024):
    S, K = x.shape
    tn = SWA_HEADS * SWA_DIM
    tm = min(tm, S)
    tab = pl.BlockSpec((tm, LANES), lambda n, i: (i, 0))
    return pl.pallas_call(
        _swa_qkv_kernel,
        out_shape=jax.ShapeDtypeStruct((S, 3 * tn), BF16),
        grid=(3, S // tm),
        in_specs=[pl.BlockSpec((tm, K), lambda n, i: (i, 0)),
                  pl.BlockSpec((K, tn), lambda n, i: (0, n)),
                  pl.BlockSpec((1, tn), lambda n, i: (0, jnp.minimum(n, 1))),
                  tab, tab],
        out_specs=pl.BlockSpec((tm, tn), lambda n, i: (i, n)),
        compiler_params=_cparams(("parallel", "parallel")),
        name="swa_qkv",
    )(x, w, hg, cos, sin)


def _dilated_bias(t, nrel):
    row = np.arange(t)[:, None]
    col = np.arange(t)[None, :]
    out = np.empty((nrel, t, t), np.float32)
    for r in range(nrel):
        delta = r * t + row - col
        cnt = np.zeros((t, t), np.float64)
        for window, dilation in SWA_PATTERNS:
            cnt += (delta >= 0) & (delta <= window) & (delta % dilation == 0)
        with np.errstate(divide="ignore"):
            out[r] = np.log2(cnt)
    return out


def _dilated_kernel(q_ref, *refs, sb, nrel, live):
    kv = refs[:2 * nrel]
    bias_ref, o_ref, m_sc, acc_sc = refs[2 * nrel:]
    qi = pl.program_id(1)
    t = q_ref.shape[0]
    _flash_init(m_sc, acc_sc)

    def block(rr):
        k_ref, v_ref = kv[2 * rr], kv[2 * rr + 1]
        v1 = _with_ones(v_ref[...])
        for rb in range(t // sb):
            rows = slice(rb * sb, (rb + 1) * sb)
            lo, hi = live[rr][rb]
            s = _dot_nt(q_ref[rows, :], k_ref[lo:hi, :]) + bias_ref[rr, rows, lo:hi]
            _flash_rows(s, v1[lo:hi, :], rows, m_sc, acc_sc)

    @pl.when(qi >= nrel - 1)
    def _():
        for rr in range(nrel):
            block(rr)

    @pl.when(qi < nrel - 1)
    def _():
        block(0)
        for rr in range(1, nrel - 1):
            @pl.when(qi >= rr)
            def _(rr=rr):
                block(rr)

    _flash_finish(acc_sc, o_ref)


def _dilated_attention(qkv, *, t=1024, sb=256):
    S = qkv.shape[0]
    t = min(t, S)
    sb = min(sb, t)
    max_window = max(w for w, _ in SWA_PATTERNS)
    nrel = min(-(-max_window // t) + 1, S // t)
    H = SWA_HEADS
    bias = _dilated_bias(t, nrel)
    live = []
    for r in range(nrel):
        per_rb = []
        for rb in range(t // sb):
            cols = np.nonzero(np.isfinite(bias[r, rb * sb:(rb + 1) * sb]).any(axis=0))[0]
            lo = int(cols.min()) // LANES * LANES
            hi = -(-(int(cols.max()) + 1) // LANES) * LANES
            per_rb.append((lo, hi))
        live.append(per_rb)
    kv_specs = []
    for r in range(nrel):
        kv_specs.append(pl.BlockSpec((t, SWA_DIM), lambda h, i, r=r: (jnp.maximum(i - r, 0), H + h)))
        kv_specs.append(pl.BlockSpec((t, SWA_DIM), lambda h, i, r=r: (jnp.maximum(i - r, 0), 2 * H + h)))
    stat = pltpu.VMEM((t, LANES), F32)
    return pl.pallas_call(
        functools.partial(_dilated_kernel, sb=sb, nrel=nrel, live=live),
        out_shape=jax.ShapeDtypeStruct((S, H * SWA_DIM), BF16),
        grid=(H, S // t),
        in_specs=[pl.BlockSpec((t, SWA_DIM), lambda h, i: (i, h))] + kv_specs
                 + [pl.BlockSpec((nrel, t, t), lambda h, i: (0, 0, 0))],
        out_specs=pl.BlockSpec((t, SWA_DIM), lambda h, i: (i, h)),
        scratch_shapes=[stat, pltpu.VMEM((t, 2 * SWA_DIM), F32)],
        compiler_params=_cparams(("parallel", "parallel")),
        name="dilated_attention",
    )(qkv, *([qkv] * (2 * nrel)), jnp.asarray(bias))


def _gla_kernel(q_ref, k_ref, v_ref, glr_ref, wgu_ref, bg_ref, rd_ref, gn_ref, tri_ref, o_ref,
                st_sc, sall_sc, oi_sc, *, G):
    C = GLA_CHUNK
    SB = GLA_SUB
    DK = GLA_DK
    DV = GLA_DV
    c_idx = pl.program_id(1)

    @pl.when(c_idx == 0)
    def _():
        st_sc[...] = jnp.zeros(st_sc.shape, F32)

    z = _dot(glr_ref[...], wgu_ref[...]) + bg_ref[...]
    lg = -_softplus(-z) * (1.0 / GLA_NORMALIZER)
    bc = _bsplit_cumsum(tri_ref[...], lg.reshape(G, C, DK))
    q = (q_ref[...] * (DK ** -0.5)).reshape(G, C, DK)
    k = k_ref[...].reshape(G, C, DK)
    v = v_ref[...].reshape(G, C, DV)
    blast = bc[:, C - 1:C, :]
    khat = k * jnp.exp(blast - bc)
    gkv = _bmm(jnp.swapaxes(v, 1, 2), khat)
    eb = jnp.exp(blast)

    st = st_sc[...]
    for c in range(G):
        sall_sc[c] = st
        st = st * eb[c] + gkv[c]
    st_sc[...] = st

    ti = lax.broadcasted_iota(jnp.int32, (1, SB, SB, 1), 1)
    si = lax.broadcasted_iota(jnp.int32, (1, SB, SB, 1), 2)
    causal = si <= ti
    for sb in range(C // SB):
        lo = sb * SB
        qs = q[:, lo:lo + SB]
        ks = k[:, lo:lo + SB]
        bs = bc[:, lo:lo + SB]
        rel = bs[:, :, None, :] - bs[:, None, :, :]
        dec = jnp.exp(jnp.where(causal, rel, -jnp.inf))
        att = jnp.sum(qs[:, :, None, :] * ks[:, None, :, :] * dec, axis=-1)
        o_sb = _bmm(att, v[:, lo:lo + SB])
        if sb > 0:
            bm = bc[:, lo - 1:lo]
            qsc = qs * jnp.exp(bs - bm)
            ksc = k[:, :lo] * jnp.exp(bm - bc[:, :lo])
            o_sb = o_sb + _bmm(_bmm_nt(qsc, ksc), v[:, :lo])
        oi_sc[:, lo:lo + SB, :] = o_sb

    o = (oi_sc[...] + _bmm_nt(q * jnp.exp(bc), sall_sc[...])).reshape(G * C, DV)
    rd = rd_ref[...]
    o_ref[...] = (_rms(o, gn_ref[...]) * (rd * _sigmoid(rd))).astype(o_ref.dtype)


def _gla(pb, wgu, bg, gn, *, G=32):
    S = pb.shape[0]
    C = GLA_CHUNK
    G = min(G, S // C)
    TB = G * C
    H, DK, DV = GLA_HEADS, GLA_DK, GLA_DV
    t = np.arange(C)
    tri = jnp.asarray((t[None, :] <= t[:, None]).astype(np.float32), BF16)
    glr_block = (2 * H * DK + 2 * H * DV) // LANES
    return pl.pallas_call(
        functools.partial(_gla_kernel, G=G),
        out_shape=jax.ShapeDtypeStruct((S, H * DV), BF16),
        grid=(H, S // TB),
        in_specs=[pl.BlockSpec((TB, DK), lambda h, c: (c, h)),
                  pl.BlockSpec((TB, DK), lambda h, c: (c, H + h)),
                  pl.BlockSpec((TB, DV), lambda h, c: (c, (2 * H * DK) // DV + h)),
                  pl.BlockSpec((TB, LANES), lambda h, c: (c, glr_block)),
                  pl.BlockSpec((LANES, DK), lambda h, c: (0, h)),
                  pl.BlockSpec((1, DK), lambda h, c: (0, h)),
                  pl.BlockSpec((TB, DV), lambda h, c: (c, (2 * H * DK + H * DV) // DV + h)),
                  pl.BlockSpec((1, DV), lambda h, c: (0, h)),
                  pl.BlockSpec((C, C), lambda h, c: (0, 0))],
        out_specs=pl.BlockSpec((TB, DV), lambda h, c: (c, h)),
        scratch_shapes=[pltpu.VMEM((DV, DK), F32), pltpu.VMEM((G, DV, DK), F32), pltpu.VMEM((G, C, DV), F32)],
        compiler_params=_cparams(("parallel", "arbitrary")),
        name="gla",
    )(pb, pb, pb, pb, wgu, bg, pb, gn, tri)


def _odd_mixer(hn, tabs, w_in, q_head_norm, k_head_norm, w_gate_up, b_gate, gla_norm, w_out):
    cos2, sin2 = tabs[2], tabs[3]
    nq = SWA_HEADS * SWA_DIM
    hg = jnp.concatenate([jnp.tile(q_head_norm * (SWA_DIM ** -0.5 * LOG2E), SWA_HEADS), jnp.tile(k_head_norm, SWA_HEADS)])[None, :]
    w_in = w_in.astype(BF16)
    qkv = _swa_qkv(hn, w_in, hg, cos2, sin2)
    o_c = _dilated_attention(qkv)
    o = 3 * nq
    dk, dv = GLA_HEADS * GLA_DK, GLA_HEADS * GLA_DV
    wb = jnp.concatenate([w_in[:, o:o + 2 * dk + dv], w_in[:, o + 2 * dk + dv + GLA_LORA:],
                          _pad_cols(w_in[:, o + 2 * dk + dv:o + 2 * dk + dv + GLA_LORA], LANES)], axis=1)
    pb = _proj(hn, wb, tm=512, tn=wb.shape[1])
    wgu = jnp.pad(w_gate_up, ((0, LANES - GLA_LORA), (0, 0))).astype(BF16)
    o_d = _gla(pb, wgu, b_gate[None, :], gla_norm.reshape(1, dv))
    return o_c, o_d, w_out.astype(BF16)


def kernel(x, positions, ffn_norm, ffn_w_gate, ffn_w_up, ffn_w_down, mix_norm, mla_rwkv_w_in, mla_q_norm, mla_w_uq,
           mla_kv_norm, mla_w_ukv, mla_q_head_norm, mla_k_head_norm, rwkv_mu, rwkv_w0, rwkv_w2, rwkv_a0, rwkv_a2,
           rwkv_g2, rwkv_k_k, rwkv_k_a, rwkv_r_k, rwkv_ln_w, rwkv_ln_b, mla_rwkv_w_out, swa_gla_w_in,
           swa_q_head_norm, swa_k_head_norm, gla_w_gate_up, gla_b_gate, gla_norm, swa_gla_w_out):
    B, S, D = x.shape
    assert B == 1
    h = x.reshape(S, D)
    tabs = _rope_tables(positions.reshape(S))
    wg = ffn_w_gate.astype(BF16)
    wu = ffn_w_up.astype(BF16)
    wd = ffn_w_down.astype(BF16)
    depth = ffn_norm.shape[0]
    for layer in range(depth):
        i = layer // 2
        h, hn = _ffn(h, ffn_norm, wg, wu, wd, layer, 0, next_norm=mix_norm[layer])
        if layer % 2 == 0:
            mix = _even_mixer(hn, tabs, mla_rwkv_w_in[i], mla_q_norm[i], mla_w_uq[i], mla_kv_norm[i],
                            mla_w_ukv[i], mla_q_head_norm[i], mla_k_head_norm[i], rwkv_mu[i], rwkv_w0[i],
                            rwkv_w2[i], rwkv_a0[i], rwkv_a2[i], rwkv_g2[i], rwkv_k_k[i], rwkv_k_a[i], rwkv_r_k[i],
                            rwkv_ln_w[i], rwkv_ln_b[i], mla_rwkv_w_out[i])
        else:
            mix = _odd_mixer(hn, tabs, swa_gla_w_in[i], swa_q_head_norm[i], swa_k_head_norm[i],
                           gla_w_gate_up[i], gla_b_gate[i], gla_norm[i], swa_gla_w_out[i])
        h = _ffn(h, ffn_norm, wg, wu, wd, layer, 1, mix=mix)
    return h.reshape(B, S, D)
```
